```python
import math
import jax, jax.numpy as jnp
from jax import lax
import numpy as np

D_MODEL = 1024
BATCH = 4
SEQ = 8192
DEPTH = 1

N_META = 16
S5_GROUP = 16
S5_GROUPS = 32
S5_WIDTH = S5_GROUP * S5_GROUPS
S5_STATE = 64
DT_MIN = 0.001
DT_MAX = 0.1
HG_HEADS = 4
HG_HEAD_DIM = 128
HG_WIDTH = HG_HEADS * HG_HEAD_DIM
CHUNK = 64
N_BRANCH = 2
IN_WIDTH = S5_WIDTH + 4 * HG_WIDTH + N_BRANCH * D_MODEL
IN_SPLITS = (S5_WIDTH, S5_WIDTH + HG_WIDTH, S5_WIDTH + 2 * HG_WIDTH,
             S5_WIDTH + 3 * HG_WIDTH, S5_WIDTH + 4 * HG_WIDTH)
N_EXPERTS = 32
TOP_K = 4
D_EXPERT = D_MODEL
SWIGLU_ALPHA = 1.702
SWIGLU_LIMIT = 7.0
MOE_BLOCK = 256
EPS = 1e-6

kernel_name = "hybrid_s5_hgrn2_moe_meta"


def rmsnorm(x, gain):
    x32 = x.astype(jnp.float32)
    y = x32 * lax.rsqrt(jnp.mean(x32 * x32, axis=-1, keepdims=True) + EPS)
    return (y * gain.astype(jnp.float32)).astype(x.dtype)


def s5_mixer(u, lam_re, lam_im, log_dt, b_re, b_im, c_re, c_im, d_skip, w_glu):
    f32 = jnp.float32
    bsz, seq_len, _ = u.shape
    u32 = u.astype(f32).reshape(bsz, seq_len, S5_GROUPS, S5_GROUP)
    lam = lax.complex(lam_re.astype(f32), lam_im.astype(f32))
    dt = jnp.exp(log_dt.astype(f32))[:, None]
    lam_bar = jnp.exp(lam * dt)
    b_mat = lax.complex(b_re.astype(f32), b_im.astype(f32))
    b_bar = ((lam_bar - 1.0) / lam)[:, :, None] * b_mat
    bu = jnp.einsum('gph,blgh->blgp', b_bar, u32.astype(jnp.complex64))
    a = jnp.broadcast_to(lam_bar, bu.shape)

    def combine(left, right):
        a_l, b_l = left
        a_r, b_r = right
        return a_r * a_l, a_r * b_l + b_r

    _, states = lax.associative_scan(combine, (a, bu), axis=1)
    c_mat = lax.complex(c_re.astype(f32), c_im.astype(f32))
    y = jnp.real(jnp.einsum('ghp,blgp->blgh', c_mat, states)) + d_skip.astype(f32) * u32
    y = jax.nn.gelu(y.reshape(bsz, seq_len, S5_WIDTH))
    return y * jax.nn.sigmoid(y @ w_glu.astype(f32))


def hgrn2_mixer(q, fpre, v, g, lb, gn_gain):
    f32 = jnp.float32
    bsz, seq_len, _ = q.shape
    f = lb + (1.0 - lb) * jax.nn.sigmoid(fpre.astype(f32))
    logf = jnp.log(f)
    k = 1.0 - f
    q = jax.nn.silu(q.astype(f32))
    v = v.astype(f32)
    pad = CHUNK - N_META

    def to_chunks(t):
        t = jnp.pad(t, ((0, 0), (pad, 0), (0, 0)))
        n_chunks = t.shape[1] // CHUNK
        t = t.reshape(bsz, n_chunks, CHUNK, HG_HEADS, HG_HEAD_DIM)
        return t.transpose(1, 0, 3, 2, 4)

    qc, kc, vc, lc = to_chunks(q), to_chunks(k), to_chunks(v), to_chunks(logf)
    bc = jnp.cumsum(lc, axis=3)
    causal = jnp.tril(jnp.ones((CHUNK, CHUNK), dtype=bool))[:, :, None]

    def step(state, inp):
        qi, ki, vi, bi = inp
        diff = bi[:, :, :, None, :] - bi[:, :, None, :, :]
        decay = jnp.exp(jnp.where(causal, diff, -jnp.inf))
        scores = jnp.einsum('bhid,bhjd,bhijd->bhij', qi, ki, decay)
        o = scores @ vi + jnp.einsum('bhid,bhde->bhie', qi * jnp.exp(bi), state)
        b_last = bi[:, :, -1:, :]
        state = (jnp.exp(b_last[:, :, 0, :, None]) * state
                 + jnp.einsum('bhjd,bhje->bhde', ki * jnp.exp(b_last - bi), vi))
        return state, o

    s0 = jnp.zeros((bsz, HG_HEADS, HG_HEAD_DIM, HG_HEAD_DIM), f32)
    _, o = lax.scan(step, s0, (qc, kc, vc, bc))
    o = o.transpose(1, 0, 3, 2, 4).reshape(bsz, -1, HG_HEADS, HG_HEAD_DIM)[:, pad:]
    o = o * lax.rsqrt(jnp.mean(o * o, axis=-1, keepdims=True) + EPS)
    o = o.reshape(bsz, seq_len, HG_WIDTH) * gn_gain.astype(f32)
    return o * jax.nn.silu(g.astype(f32))


def moe(xn, w_router, b_router, w_gate_up, b_gate_up, w_down, b_down):
    f32 = jnp.float32
    bsz, seq_len, d = xn.shape
    tokens = xn.reshape(-1, d)
    n_tok = tokens.shape[0]
    logits = (tokens @ w_router + b_router).astype(f32)
    top_logits, top_idx = lax.top_k(logits, TOP_K)
    top_w = jax.nn.softmax(top_logits, axis=-1)

    n_assign = n_tok * TOP_K
    flat_e = top_idx.reshape(-1).astype(jnp.int32)
    flat_tok = jnp.repeat(jnp.arange(n_tok, dtype=jnp.int32), TOP_K)
    flat_w = top_w.reshape(-1)
    order = jnp.argsort(flat_e)
    sorted_e = flat_e[order]
    counts = jnp.bincount(flat_e, length=N_EXPERTS).astype(jnp.int32)
    padded = (counts + MOE_BLOCK - 1) // MOE_BLOCK * MOE_BLOCK
    padded_end = jnp.cumsum(padded)
    padded_start = padded_end - padded
    start = jnp.cumsum(counts) - counts
    rank = jnp.arange(n_assign, dtype=jnp.int32) - start[sorted_e]
    dest = padded_start[sorted_e] + rank
    n_blocks = -(-n_assign // MOE_BLOCK) + N_EXPERTS
    n_rows = n_blocks * MOE_BLOCK
    row_tok = jnp.full((n_rows,), n_tok, dtype=jnp.int32).at[dest].set(flat_tok[order])
    row_w = jnp.zeros((n_rows,), f32).at[dest].set(flat_w[order])
    block_start = jnp.arange(n_blocks, dtype=jnp.int32) * MOE_BLOCK
    block_e = jnp.minimum(jnp.searchsorted(padded_end, block_start, side='right'), N_EXPERTS - 1)
    tokens_pad = jnp.concatenate([tokens, jnp.zeros((1, d), tokens.dtype)], axis=0)
    x_rows = tokens_pad[row_tok].reshape(n_blocks, MOE_BLOCK, d)

    def expert_block(args):
        xb, e = args
        gu = xb @ w_gate_up[e] + b_gate_up[e]
        gate = jnp.minimum(gu[:, ::2], SWIGLU_LIMIT)
        up = jnp.clip(gu[:, 1::2], -SWIGLU_LIMIT, SWIGLU_LIMIT)
        glu = gate * jax.nn.sigmoid(gate * SWIGLU_ALPHA)
        return ((up + 1.0) * glu) @ w_down[e] + b_down[e]

    y_rows = lax.map(expert_block, (x_rows, block_e)).reshape(n_rows, d)
    y = jax.ops.segment_sum(y_rows * row_w[:, None].astype(y_rows.dtype), row_tok,
                            num_segments=n_tok + 1)[:n_tok]
    return y.reshape(bsz, seq_len, d).astype(xn.dtype)


def setup_inputs(seed: int = 0) -> dict:
    key = jax.random.key(seed)
    ks = jax.random.split(key, 32)
    f32 = jnp.float32
    nrm = lambda k, shape, scale: jax.random.normal(k, shape, f32) * scale
    L_ = DEPTH
    x = jax.random.normal(ks[0], (BATCH, SEQ, D_MODEL), f32)
    meta_tokens = nrm(ks[1], (N_META, D_MODEL), 1.0)
    norm1_gain = 1.0 + nrm(ks[2], (L_, D_MODEL), 0.02)
    w_in = nrm(ks[3], (L_, D_MODEL, IN_WIDTH), D_MODEL ** -0.5)
    s5_lambda_re = -0.5 * jnp.exp(nrm(ks[4], (L_, S5_GROUPS, S5_STATE), 0.01))
    s5_lambda_im = (jnp.pi * jnp.arange(S5_STATE, dtype=f32))[None, None, :] + nrm(ks[5], (L_, S5_GROUPS, S5_STATE), 0.01)
    s5_log_dt = jax.random.uniform(ks[6], (L_, S5_GROUPS), f32, math.log(DT_MIN), math.log(DT_MAX))
    s5_b_re = nrm(ks[7], (L_, S5_GROUPS, S5_STATE, S5_GROUP), (2 * S5_GROUP) ** -0.5)
    s5_b_im = nrm(ks[8], (L_, S5_GROUPS, S5_STATE, S5_GROUP), (2 * S5_GROUP) ** -0.5)
    s5_c_re = nrm(ks[9], (L_, S5_GROUPS, S5_GROUP, S5_STATE), (2 * S5_STATE) ** -0.5)
    s5_c_im = nrm(ks[10], (L_, S5_GROUPS, S5_GROUP, S5_STATE), (2 * S5_STATE) ** -0.5)
    s5_d = nrm(ks[11], (L_, S5_GROUPS, S5_GROUP), 1.0)
    s5_w_glu = nrm(ks[12], (L_, S5_WIDTH, S5_WIDTH), S5_WIDTH ** -0.5)
    hgrn_lb_logits = nrm(ks[13], (L_ + 1, HG_WIDTH), 0.1)
    hgrn_norm_gain = 1.0 + nrm(ks[14], (L_, HG_WIDTH), 0.02)
    w_branch_s5 = nrm(ks[15], (L_, S5_WIDTH, D_MODEL), S5_WIDTH ** -0.5)
    w_branch_hgrn = nrm(ks[16], (L_, HG_WIDTH, D_MODEL), HG_WIDTH ** -0.5)
    w_out = nrm(ks[17], (L_, D_MODEL, D_MODEL), D_MODEL ** -0.5)
    norm2_gain = 1.0 + nrm(ks[18], (L_, D_MODEL), 0.02)
    w_router = nrm(ks[19], (L_, D_MODEL, N_EXPERTS), D_MODEL ** -0.5)
    b_router = nrm(ks[20], (L_, N_EXPERTS), 0.01)
    w_gate_up = nrm(ks[21], (L_, N_EXPERTS, D_MODEL, 2 * D_EXPERT), D_MODEL ** -0.5)
    b_gate_up = nrm(ks[22], (L_, N_EXPERTS, 2 * D_EXPERT), 0.01)
    w_down = nrm(ks[23], (L_, N_EXPERTS, D_EXPERT, D_MODEL), D_EXPERT ** -0.5)
    b_down = nrm(ks[24], (L_, N_EXPERTS, D_MODEL), 0.01)
    final_norm_gain = 1.0 + nrm(ks[25], (D_MODEL,), 0.02)
    return {"x": x, "meta_tokens": meta_tokens, "norm1_gain": norm1_gain, "w_in": w_in,
            "s5_lambda_re": s5_lambda_re, "s5_lambda_im": s5_lambda_im, "s5_log_dt": s5_log_dt,
            "s5_b_re": s5_b_re, "s5_b_im": s5_b_im, "s5_c_re": s5_c_re, "s5_c_im": s5_c_im,
            "s5_d": s5_d, "s5_w_glu": s5_w_glu, "hgrn_lb_logits": hgrn_lb_logits,
            "hgrn_norm_gain": hgrn_norm_gain, "w_branch_s5": w_branch_s5, "w_branch_hgrn": w_branch_hgrn,
            "w_out": w_out, "norm2_gain": norm2_gain, "w_router": w_router, "b_router": b_router,
            "w_gate_up": w_gate_up, "b_gate_up": b_gate_up, "w_down": w_down, "b_down": b_down,
            "final_norm_gain": final_norm_gain}


def reference(x, meta_tokens, norm1_gain, w_in, s5_lambda_re, s5_lambda_im, s5_log_dt,
              s5_b_re, s5_b_im, s5_c_re, s5_c_im, s5_d, s5_w_glu, hgrn_lb_logits,
              hgrn_norm_gain, w_branch_s5, w_branch_hgrn, w_out, norm2_gain, w_router, b_router,
              w_gate_up, b_gate_up, w_down, b_down, final_norm_gain):
    f32 = jnp.float32
    bsz = x.shape[0]
    meta = jnp.broadcast_to(meta_tokens[None].astype(x.dtype), (bsz, N_META, D_MODEL))
    h = jnp.concatenate([meta, x], axis=1)
    lower_bounds = jnp.cumsum(jax.nn.softmax(hgrn_lb_logits.astype(f32), axis=0), axis=0)
    for l in range(DEPTH):
        xn = rmsnorm(h, norm1_gain[l])
        proj = xn @ w_in[l]
        u, q, fpre, v, og, gates = jnp.split(proj, IN_SPLITS, axis=-1)
        y_s5 = s5_mixer(u, s5_lambda_re[l], s5_lambda_im[l], s5_log_dt[l], s5_b_re[l], s5_b_im[l],
                        s5_c_re[l], s5_c_im[l], s5_d[l], s5_w_glu[l])
        y_hg = hgrn2_mixer(q, fpre, v, og, lower_bounds[l], hgrn_norm_gain[l])
        gate_s5, gate_hg = jnp.split(jax.nn.sigmoid(gates.astype(f32)), N_BRANCH, axis=-1)
        merged = (gate_s5 * (y_s5 @ w_branch_s5[l].astype(f32))
                  + gate_hg * (y_hg @ w_branch_hgrn[l].astype(f32)))
        h = h + merged.astype(h.dtype) @ w_out[l]
        h = h + moe(rmsnorm(h, norm2_gain[l]), w_router[l], b_router[l], w_gate_up[l],
                    b_gate_up[l], w_down[l], b_down[l])
    y = rmsnorm(h, final_norm_gain)[:, N_META:]
    return y
```

```python
import functools

import jax
import jax.numpy as jnp
from jax import lax
from jax.experimental import pallas as pl
from jax.experimental.pallas import tpu as pltpu

F32 = jnp.float32
BF16 = jnp.bfloat16

D_MODEL = 1024
N_META = 16
S5_GROUP = 16
S5_GROUPS = 32
S5_WIDTH = 512
S5_STATE = 64
HG_HEADS = 4
HG_HEAD_DIM = 128
HG_WIDTH = 512
N_EXPERTS = 32
TOP_K = 4
D_EXPERT = 1024
SWIGLU_ALPHA = 1.702
SWIGLU_LIMIT = 7.0
EPS = 1e-6

LANES = 128
SUBLANES = 8
META_PAD = 128
TIME_BLOCK = 256
ROW_BLOCK = 512
MOE_BLOCK = 256
GATHER_ROWS = 1024
HG_DIAG = 32
NEG_BIG = -1e30
VMEM_LIMIT = 56 * 1024 * 1024


def _cparams(sem):
    return pltpu.CompilerParams(dimension_semantics=sem, vmem_limit_bytes=VMEM_LIMIT)


def _sigmoid(x):
    return 1.0 / (1.0 + jnp.exp(-x))


def _inproj_kernel(x_ref, g_ref, w_ref, u_ref, qfvg_ref, gates_ref):
    x = x_ref[...]
    ms = jnp.mean(x * x, axis=-1, keepdims=True)
    xn = (x * lax.rsqrt(ms + EPS) * g_ref[...]).astype(BF16)
    u_ref[...] = jnp.dot(xn, w_ref[:, 0:S5_WIDTH], preferred_element_type=F32).astype(BF16)
    a, b = S5_WIDTH, S5_WIDTH + 4 * HG_WIDTH
    qfvg_ref[...] = jnp.dot(xn, w_ref[:, a:b], preferred_element_type=F32).astype(BF16)
    gates_ref[...] = jnp.dot(xn, w_ref[:, b:], preferred_element_type=F32).astype(BF16)


def _inproj(rows, gain, w_in_bf16, row_block):
    n = rows.shape[0]
    n_gate = 2 * D_MODEL
    return pl.pallas_call(
        _inproj_kernel,
        grid=(n // row_block,),
        in_specs=[
            pl.BlockSpec((row_block, D_MODEL), lambda i: (i, 0)),
            pl.BlockSpec((1, D_MODEL), lambda i: (0, 0)),
            pl.BlockSpec(w_in_bf16.shape, lambda i: (0, 0)),
        ],
        out_specs=[
            pl.BlockSpec((row_block, S5_WIDTH), lambda i: (i, 0)),
            pl.BlockSpec((row_block, 4 * HG_WIDTH), lambda i: (i, 0)),
            pl.BlockSpec((row_block, n_gate), lambda i: (i, 0)),
        ],
        out_shape=[
            jax.ShapeDtypeStruct((n, S5_WIDTH), BF16),
            jax.ShapeDtypeStruct((n, 4 * HG_WIDTH), BF16),
            jax.ShapeDtypeStruct((n, n_gate), BF16),
        ],
        compiler_params=_cparams(("parallel",)),
        name="inproj",
    )(rows, gain, w_in_bf16)


N_STATE = S5_GROUPS * S5_STATE
HALF_STATE = N_STATE // 2
HALF_COLS = 2 * HALF_STATE


def _s5_scan(xs_ref, tab_ref, car_ref, chains, n_tiles):
    for j in range(N_STATE // LANES):
        half, m0 = divmod(j * LANES, HALF_STATE)
        cre = half * HALF_COLS + m0
        cim = cre + HALF_STATE
        n0 = j * LANES
        tabs = [tab_ref[i, :, n0:n0 + LANES] for i in range(8)]
        init = tuple((car_ref[c, :, cre:cre + LANES], car_ref[c, :, cim:cim + LANES])
                     for _, c in chains)

        def body(i, carry, cre=cre, cim=cim, tabs=tabs):
            new = []
            for (row0, _), (cr, ci) in zip(chains, carry):
                r0 = pl.multiple_of(row0 + i * SUBLANES, SUBLANES)
                xr = xs_ref[pl.ds(r0, SUBLANES), cre:cre + LANES]
                xi = xs_ref[pl.ds(r0, SUBLANES), cim:cim + LANES]
                for k, shift in enumerate((1, 2, 4)):
                    ar, ai = tabs[2 * k], tabs[2 * k + 1]
                    rr = pltpu.roll(xr, shift, 0)
                    ri = pltpu.roll(xi, shift, 0)
                    xr, xi = xr + ar * rr - ai * ri, xi + ar * ri + ai * rr
                pr, pi = tabs[6], tabs[7]
                xr, xi = xr + pr * cr - pi * ci, xi + pr * ci + pi * cr
                xs_ref[pl.ds(r0, SUBLANES), cre:cre + LANES] = xr
                xs_ref[pl.ds(r0, SUBLANES), cim:cim + LANES] = xi
                new.append((jnp.broadcast_to(xr[SUBLANES - 1:SUBLANES, :], (SUBLANES, LANES)),
                            jnp.broadcast_to(xi[SUBLANES - 1:SUBLANES, :], (SUBLANES, LANES))))
            return tuple(new)

        fin = lax.fori_loop(0, n_tiles, body, init)
        for (_, c), (cr, ci) in zip(chains, fin):
            car_ref[c, :, cre:cre + LANES] = cr
            car_ref[c, :, cim:cim + LANES] = ci


def _s5_kernel(u_ref, um_ref, bm_ref, cm_ref, tab_ref, d_ref, wglu_ref, y_ref, xs_ref, car_ref,
               *, nb, tb):
    half_ch = S5_WIDTH // 2

    @pl.when(pl.program_id(0) == 0)
    def _():
        car_ref[...] = jnp.zeros_like(car_ref)
        um = um_ref[...]
        for hf in range(2):
            xs_ref[0:META_PAD, hf * HALF_COLS:(hf + 1) * HALF_COLS] = jnp.dot(
                um[:, hf * half_ch:(hf + 1) * half_ch], bm_ref[hf], preferred_element_type=F32)
        _s5_scan(xs_ref, tab_ref, car_ref, [(0, 0)], META_PAD // SUBLANES)
        for b in range(1, nb):
            car_ref[b] = car_ref[0]

    u = u_ref[...].reshape(nb * tb, S5_WIDTH)
    for hf in range(2):
        xs_ref[:, hf * HALF_COLS:(hf + 1) * HALF_COLS] = jnp.dot(
            u[:, hf * half_ch:(hf + 1) * half_ch], bm_ref[hf], preferred_element_type=F32)
    _s5_scan(xs_ref, tab_ref, car_ref, [(b * tb, b) for b in range(nb)], tb // SUBLANES)
    ys = [jnp.dot(xs_ref[:, hf * HALF_COLS:(hf + 1) * HALF_COLS].astype(BF16), cm_ref[hf],
                  preferred_element_type=F32) for hf in range(2)]
    y = jnp.concatenate(ys, axis=1) + d_ref[...] * u.astype(F32)
    y = 0.5 * y * (1.0 + jnp.tanh(0.7978845608028654 * (y + 0.044715 * (y * y * y))))
    z = jnp.dot(y.astype(BF16), wglu_ref[...], preferred_element_type=F32)
    y_ref[...] = (y * _sigmoid(z)).astype(BF16).reshape(nb, tb, S5_WIDTH)


def _s5_mixer(u, u_meta, bm, cm, tab, d_skip, w_glu_bf16):
    nb, seq, _ = u.shape
    tb = TIME_BLOCK
    kern = functools.partial(_s5_kernel, nb=nb, tb=tb)
    const2 = lambda t: (0, 0)
    const3 = lambda t: (0, 0, 0)
    return pl.pallas_call(
        kern,
        grid=(seq // tb,),
        in_specs=[
            pl.BlockSpec((nb, tb, S5_WIDTH), lambda t: (0, t, 0)),
            pl.BlockSpec(u_meta.shape, const2),
            pl.BlockSpec(bm.shape, const3),
            pl.BlockSpec(cm.shape, const3),
            pl.BlockSpec(tab.shape, const3),
            pl.BlockSpec(d_skip.shape, const2),
            pl.BlockSpec(w_glu_bf16.shape, const2),
        ],
        out_specs=pl.BlockSpec((nb, tb, S5_WIDTH), lambda t: (0, t, 0)),
        out_shape=jax.ShapeDtypeStruct((nb, seq, S5_WIDTH), BF16),
        scratch_shapes=[
            pltpu.VMEM((nb * tb, 2 * N_STATE), F32),
            pltpu.VMEM((nb, SUBLANES, 2 * N_STATE), F32),
        ],
        compiler_params=_cparams(("arbitrary",)),
        name="s5_mixer",
    )(u, u_meta, bm, cm, tab, d_skip, w_glu_bf16)


def _s5_params(lam_re, lam_im, log_dt, b_re, b_im, c_re, c_im):
    lam = lax.complex(lam_re.astype(F32), lam_im.astype(F32))
    dt = jnp.exp(log_dt.astype(F32))[:, None]
    lam_dt = lam * dt
    lam_bar = jnp.exp(lam_dt)
    b_bar = ((lam_bar - 1.0) / lam)[:, :, None] * lax.complex(b_re.astype(F32), b_im.astype(F32))
    gl = S5_GROUPS // 2
    eye = jnp.eye(gl, dtype=F32)

    def in_half(bh):
        def blk(part):
            t = jnp.einsum('gph,gk->ghkp', part, eye)
            return t.reshape(gl * S5_GROUP, gl * S5_STATE)
        return jnp.concatenate([blk(jnp.real(bh)), blk(jnp.imag(bh))], axis=1)

    def out_half(cr, ci):
        def blk(part):
            t = jnp.einsum('ghp,gk->gpkh', part, eye)
            return t.reshape(gl * S5_STATE, gl * S5_GROUP)
        return jnp.concatenate([blk(cr), blk(-ci)], axis=0)

    bm = jnp.stack([in_half(b_bar[:gl]), in_half(b_bar[gl:])]).astype(BF16)
    cm = jnp.stack([out_half(c_re[:gl].astype(F32), c_im[:gl].astype(F32)),
                    out_half(c_re[gl:].astype(F32), c_im[gl:].astype(F32))]).astype(BF16)
    lam_flat = lam_dt.reshape(1, N_STATE)
    rows = jnp.arange(SUBLANES, dtype=F32)[:, None]
    tabs = []
    for shift in (1, 2, 4):
        p = jnp.exp(lam_flat * float(shift)) * (rows >= shift).astype(F32)
        tabs += [jnp.real(p), jnp.imag(p)]
    p = jnp.exp(lam_flat * (rows + 1.0))
    tabs += [jnp.real(p), jnp.imag(p)]
    tab = jnp.stack([jnp.broadcast_to(t, (SUBLANES, N_STATE)) for t in tabs]).astype(F32)
    return bm, cm, tab


def _dot_nt(a, b):
    return lax.dot_general(a, b, (((1,), (1,)), ((), ())), preferred_element_type=F32)


def _dot_tn(a, b):
    return lax.dot_general(a, b, (((0,), (0,)), ((), ())), preferred_element_type=F32)


def _cumsum_rows(tri, x):
    hi = x.astype(BF16)
    r1 = x - hi.astype(F32)
    mid = r1.astype(BF16)
    lo = (r1 - mid.astype(F32)).astype(BF16)
    return (jnp.dot(tri, hi, preferred_element_type=F32)
            + jnp.dot(tri, mid, preferred_element_type=F32)
            + jnp.dot(tri, lo, preferred_element_type=F32))


def _segment_rows(x, seg, pos):
    n = x.shape[0]
    parts = [jnp.broadcast_to(x[s * seg + pos:s * seg + pos + 1, :], (seg, x.shape[1]))
             for s in range(n // seg)]
    return parts[0] if len(parts) == 1 else jnp.concatenate(parts, axis=0)


def _hg_gates(x, lb, row_valid=None):
    w = HG_WIDTH
    q = x[:, 0:w]
    f = lb + (1.0 - lb) * _sigmoid(x[:, w:2 * w])
    logf = jnp.log(f)
    k = 1.0 - f
    if row_valid is not None:
        logf = jnp.where(row_valid, logf, 0.0)
        k = jnp.where(row_valid, k, 0.0)
    return q * _sigmoid(q), k, logf


def _hg_kernel(x_ref, xm_ref, lb_ref, gain_ref, tri_ref, y_ref, st_ref, *, nb, tb):
    w, dh = HG_WIDTH, HG_HEAD_DIM
    lb = lb_ref[...]
    tri = tri_ref[...]

    @pl.when(pl.program_id(0) == 0)
    def _():
        xm = xm_ref[...].astype(F32)
        valid = lax.broadcasted_iota(jnp.int32, (META_PAD, 1), 0) >= (META_PAD - N_META)
        _, k, logf = _hg_gates(xm, lb, valid)
        bc = _cumsum_rows(tri[0:META_PAD, 0:META_PAD], logf)
        ki = (k * jnp.exp(bc[META_PAD - 1:META_PAD, :] - bc)).astype(BF16)
        v = xm[:, 2 * w:3 * w].astype(BF16)
        for h in range(HG_HEADS):
            sl = slice(h * dh, (h + 1) * dh)
            s0 = _dot_tn(v[:, sl], ki[:, sl])
            for b in range(nb):
                st_ref[b, h] = s0

    ri = lax.broadcasted_iota(jnp.int32, (tb, tb), 0)
    ci = lax.broadcasted_iota(jnp.int32, (tb, tb), 1)
    diag_shift = HG_DIAG.bit_length() - 1
    diag_mask = ((ri >> diag_shift) == (ci >> diag_shift)) & (ci <= ri)
    levels = []
    seg = 2 * HG_DIAG
    while seg <= tb:
        levels.append(seg)
        seg *= 2
    seg_masks = [None if s == tb else
                 ((ri >> (s.bit_length() - 1)) == (ci >> (s.bit_length() - 1))).astype(F32)
                 for s in levels]
    row = lax.broadcasted_iota(jnp.int32, (tb, 1), 0)

    for b in range(nb):
        x = x_ref[b].astype(F32)
        qs, k, logf = _hg_gates(x, lb)
        v = x[:, 2 * w:3 * w].astype(BF16)
        g = x[:, 3 * w:4 * w]
        bc = _cumsum_rows(tri, logf)
        dlt = bc - _segment_rows(bc, HG_DIAG, HG_DIAG // 2 - 1)
        qk = [((qs * jnp.exp(dlt)).astype(BF16), (k * jnp.exp(-dlt)).astype(BF16))]
        for s in levels:
            e = jnp.exp(-jnp.abs(bc - _segment_rows(bc, s, s // 2 - 1)))
            upper = (row & (s - 1)) >= (s // 2)
            qk.append((jnp.where(upper, qs * e, 0.0).astype(BF16),
                       jnp.where(upper, 0.0, k * e).astype(BF16)))
        b_last = bc[tb - 1:tb, :]
        qi = (qs * jnp.exp(bc)).astype(BF16)
        ki = (k * jnp.exp(b_last - bc)).astype(BF16)
        dec = jnp.exp(b_last)
        outs = []
        for h in range(HG_HEADS):
            sl = slice(h * dh, (h + 1) * dh)
            sc = jnp.where(diag_mask, _dot_nt(qk[0][0][:, sl], qk[0][1][:, sl]), 0.0)
            for (ql, kl), m in zip(qk[1:], seg_masks):
                t = _dot_nt(ql[:, sl], kl[:, sl])
                sc = sc + (t if m is None else t * m)
            st = st_ref[b, h]
            o = (jnp.dot(sc.astype(BF16), v[:, sl], preferred_element_type=F32)
                 + _dot_nt(qi[:, sl], st.astype(BF16)))
            st_ref[b, h] = dec[:, sl] * st + _dot_tn(v[:, sl], ki[:, sl])
            ms = jnp.mean(o * o, axis=-1, keepdims=True)
            outs.append(o * lax.rsqrt(ms + EPS))
        o = jnp.concatenate(outs, axis=1) * gain_ref[...]
        y_ref[b] = (o * (g * _sigmoid(g))).astype(BF16)


def _hg_mixer(x, x_meta, lb, gain, tri):
    nb, seq, _ = x.shape
    tb = TIME_BLOCK
    kern = functools.partial(_hg_kernel, nb=nb, tb=tb)
    const2 = lambda t: (0, 0)
    return pl.pallas_call(
        kern,
        grid=(seq // tb,),
        in_specs=[
            pl.BlockSpec((nb, tb, 4 * HG_WIDTH), lambda t: (0, t, 0)),
            pl.BlockSpec(x_meta.shape, const2),
            pl.BlockSpec(lb.shape, const2),
            pl.BlockSpec(gain.shape, const2),
            pl.BlockSpec(tri.shape, const2),
        ],
        out_specs=pl.BlockSpec((nb, tb, HG_WIDTH), lambda t: (0, t, 0)),
        out_shape=jax.ShapeDtypeStruct((nb, seq, HG_WIDTH), BF16),
        scratch_shapes=[pltpu.VMEM((nb, HG_HEADS, HG_HEAD_DIM, HG_HEAD_DIM), F32)],
        compiler_params=_cparams(("arbitrary",)),
        name="hgrn2_mixer",
    )(x, x_meta, lb, gain, tri)


def _split3(x):
    hi = x.astype(BF16)
    lo = (x - hi.astype(F32)).astype(BF16)
    return hi, lo


def _pack_pairs(y):
    n = y.shape[1] // 2
    lo = pltpu.bitcast(y[:, :n].astype(BF16).astype(F32), jnp.uint32)
    hi = pltpu.bitcast(y[:, n:].astype(BF16).astype(F32), jnp.uint32)
    return (lo >> 16) | (hi & jnp.uint32(0xFFFF0000))


def _unpack_pairs(p):
    lo = pltpu.bitcast(p << 16, F32)
    hi = pltpu.bitcast(p & jnp.uint32(0xFFFF0000), F32)
    return lo, hi


def _merge_kernel(x_ref, ys_ref, yh_ref, gt_ref, wbs_ref, wbh_ref, wo_ref, g2_ref, wr_hi_ref,
                  wr_lo_ref, br_ref, tri_ref, h2_ref, xp_ref, eid_ref, wt_ref, rank_ref, cnt_ref,
                  run_ref):
    @pl.when(pl.program_id(0) == 0)
    def _():
        run_ref[...] = jnp.zeros_like(run_ref)

    gt = gt_ref[...].astype(F32)
    gs = _sigmoid(gt[:, :D_MODEL])
    gh = _sigmoid(gt[:, D_MODEL:])
    merged = (gs * jnp.dot(ys_ref[...], wbs_ref[...], preferred_element_type=F32)
              + gh * jnp.dot(yh_ref[...], wbh_ref[...], preferred_element_type=F32))
    h2 = x_ref[...] + jnp.dot(merged.astype(BF16), wo_ref[...], preferred_element_type=F32)
    h2_ref[...] = h2
    ms = jnp.mean(h2 * h2, axis=-1, keepdims=True)
    xn = h2 * lax.rsqrt(ms + EPS) * g2_ref[...]
    xp_ref[...] = _pack_pairs(xn)

    x_hi, x_lo = _split3(xn)
    logits = (jnp.dot(x_hi, wr_hi_ref[...], preferred_element_type=F32)
              + jnp.dot(x_hi, wr_lo_ref[...], preferred_element_type=F32)
              + jnp.dot(x_lo, wr_hi_ref[...], preferred_element_type=F32)) + br_ref[...]
    rows = logits.shape[0]
    lane = lax.broadcasted_iota(jnp.int32, (rows, LANES), 1)
    tops, hots = [], []
    sel = jnp.zeros((rows, LANES), F32)
    for k in range(TOP_K):
        m = jnp.max(logits, axis=-1, keepdims=True)
        idx = jnp.min(jnp.where(logits == m, lane, LANES), axis=-1, keepdims=True)
        hot = lane == idx
        logits = jnp.where(hot, NEG_BIG, logits)
        tops.append(m)
        hots.append(hot)
        sel = sel + hot.astype(F32)
        eid_ref[:, k:k + 1] = idx
    es = [jnp.exp(m - tops[0]) for m in tops]
    tot = es[0] + es[1] + es[2] + es[3]
    for k in range(TOP_K):
        wt_ref[:, k:k + 1] = es[k] / tot
    prefix = jnp.dot(tri_ref[...], sel.astype(BF16), preferred_element_type=F32) + run_ref[...]
    for k in range(TOP_K):
        r = jnp.sum(jnp.where(hots[k], prefix, 0.0), axis=-1, keepdims=True)
        rank_ref[:, k:k + 1] = r.astype(jnp.int32)
    run = run_ref[...] + jnp.sum(sel, axis=0, keepdims=True)
    run_ref[...] = run
    cnt_ref[...] = run


def _merge_router(x_rows, y_s5, y_hg, gates, wbs, wbh, wo, g2, wr_hi, wr_lo, br, tri_strict):
    n = x_rows.shape[0]
    rb = ROW_BLOCK
    const2 = lambda i: (0, 0)
    rowblk = lambda width: pl.BlockSpec((rb, width), lambda i: (i, 0))
    return pl.pallas_call(
        _merge_kernel,
        grid=(n // rb,),
        in_specs=[
            rowblk(D_MODEL), rowblk(S5_WIDTH), rowblk(HG_WIDTH), rowblk(2 * D_MODEL),
            pl.BlockSpec(wbs.shape, const2), pl.BlockSpec(wbh.shape, const2),
            pl.BlockSpec(wo.shape, const2), pl.BlockSpec(g2.shape, const2),
            pl.BlockSpec(wr_hi.shape, const2), pl.BlockSpec(wr_lo.shape, const2),
            pl.BlockSpec(br.shape, const2), pl.BlockSpec(tri_strict.shape, const2),
        ],
        out_specs=[
            rowblk(D_MODEL), rowblk(D_MODEL // 2), rowblk(TOP_K), rowblk(TOP_K), rowblk(TOP_K),
            pl.BlockSpec((1, LANES), const2),
        ],
        out_shape=[
            jax.ShapeDtypeStruct((n, D_MODEL), F32),
            jax.ShapeDtypeStruct((n, D_MODEL // 2), jnp.uint32),
            jax.ShapeDtypeStruct((n, TOP_K), jnp.int32),
            jax.ShapeDtypeStruct((n, TOP_K), F32),
            jax.ShapeDtypeStruct((n, TOP_K), jnp.int32),
            jax.ShapeDtypeStruct((1, LANES), F32),
        ],
        scratch_shapes=[pltpu.VMEM((1, LANES), F32)],
        compiler_params=_cparams(("arbitrary",)),
        name="merge_router",
    )(x_rows, y_s5, y_hg, gates, wbs, wbh, wo, g2, wr_hi, wr_lo, br, tri_strict)


def _gather_kernel(idx_ref, src_ref, dst_ref, sem):
    base = pl.program_id(0) * GATHER_ROWS
    unroll = 8

    def body(i, carry):
        for j in range(unroll):
            r = i * unroll + j
            pltpu.make_async_copy(src_ref.at[pl.ds(idx_ref[r], 1)],
                                  dst_ref.at[pl.ds(base + r, 1)], sem).start()
        return carry

    lax.fori_loop(0, GATHER_ROWS // unroll, body, 0)
    blk = dst_ref.at[pl.ds(base, GATHER_ROWS)]
    pltpu.make_async_copy(blk, blk, sem).wait()


def _gather_rows(src, idx):
    n = idx.shape[0]
    return pl.pallas_call(
        _gather_kernel,
        grid=(n // GATHER_ROWS,),
        in_specs=[
            pl.BlockSpec((GATHER_ROWS,), lambda i: (i,), memory_space=pltpu.SMEM),
            pl.BlockSpec(memory_space=pl.ANY),
        ],
        out_specs=pl.BlockSpec(memory_space=pl.ANY),
        out_shape=jax.ShapeDtypeStruct((n, src.shape[1]), src.dtype),
        scratch_shapes=[pltpu.SemaphoreType.DMA(())],
        compiler_params=_cparams(("arbitrary",)),
        name="gather_rows",
    )(idx, src)


GU_GROUP = 512


def _expert_kernel(be_ref, nv_ref, x_ref, wgu_ref, bgu_ref, wd_ref, bd_ref, perm_ref, y_ref,
                   wgu_s, wd_s):
    i = pl.program_id(0)
    prev = be_ref[jnp.maximum(i - 1, 0)]
    fresh = (i == 0) | (be_ref[i] != prev)

    @pl.when(fresh & (i < nv_ref[0]))
    def _():
        for c in range(2 * D_EXPERT // GU_GROUP):
            cols = slice(c * GU_GROUP, (c + 1) * GU_GROUP)
            w = wgu_ref[0, :, cols].astype(BF16)
            wgu_s[:, cols] = jnp.dot(w, perm_ref[...], preferred_element_type=F32).astype(BF16)
        wd_s[...] = wd_ref[0].astype(BF16)

    @pl.when(i < nv_ref[0])
    def _():
        half = D_MODEL // 2
        xa, xb = _unpack_pairs(x_ref[...])
        xa = xa.astype(BF16)
        xb = xb.astype(BF16)
        acc = jnp.zeros((MOE_BLOCK, D_MODEL), F32) + bd_ref[0]
        hw = GU_GROUP // 2
        for c in range(2 * D_EXPERT // GU_GROUP):
            cols = slice(c * GU_GROUP, (c + 1) * GU_GROUP)
            gu = (jnp.dot(xa, wgu_s[0:half, cols], preferred_element_type=F32)
                  + jnp.dot(xb, wgu_s[half:, cols], preferred_element_type=F32)
                  + bgu_ref[0, :, cols])
            gate = jnp.minimum(gu[:, :hw], SWIGLU_LIMIT)
            up = jnp.clip(gu[:, hw:], -SWIGLU_LIMIT, SWIGLU_LIMIT)
            hcol = ((up + 1.0) * (gate * _sigmoid(gate * SWIGLU_ALPHA))).astype(BF16)
            acc = acc + jnp.dot(hcol, wd_s[c * hw:(c + 1) * hw, :], preferred_element_type=F32)
        y_ref[...] = _pack_pairs(acc)

    @pl.when(i >= nv_ref[0])
    def _():
        y_ref[...] = jnp.zeros_like(y_ref)


def _experts(block_e, n_valid, x_rows, w_gate_up, b_gu_grouped, w_down, b_down, perm):
    n_rows = x_rows.shape[0]
    n_blocks = n_rows // MOE_BLOCK
    half = D_MODEL // 2
    grid_spec = pltpu.PrefetchScalarGridSpec(
        num_scalar_prefetch=2,
        grid=(n_blocks,),
        in_specs=[
            pl.BlockSpec((MOE_BLOCK, half), lambda i, be, nv: (i, 0)),
            pl.BlockSpec((1, D_MODEL, 2 * D_EXPERT), lambda i, be, nv: (be[i], 0, 0)),
            pl.BlockSpec((1, 1, 2 * D_EXPERT), lambda i, be, nv: (be[i], 0, 0)),
            pl.BlockSpec((1, D_EXPERT, D_MODEL), lambda i, be, nv: (be[i], 0, 0)),
            pl.BlockSpec((1, 1, D_MODEL), lambda i, be, nv: (be[i], 0, 0)),
            pl.BlockSpec(perm.shape, lambda i, be, nv: (0, 0)),
        ],
        out_specs=pl.BlockSpec((MOE_BLOCK, half), lambda i, be, nv: (i, 0)),
        scratch_shapes=[
            pltpu.VMEM((D_MODEL, 2 * D_EXPERT), BF16),
            pltpu.VMEM((D_EXPERT, D_MODEL), BF16),
        ],
    )
    return pl.pallas_call(
        _expert_kernel,
        grid_spec=grid_spec,
        out_shape=jax.ShapeDtypeStruct((n_rows, half), jnp.uint32),
        compiler_params=_cparams(("arbitrary",)),
        name="experts",
    )(block_e, n_valid, x_rows, w_gate_up, b_gu_grouped, w_down, b_down, perm)


def _combine_kernel(h2_ref, yt_ref, wt_ref, gf_ref, out_ref):
    half = D_MODEL // 2
    wt = wt_ref[...]
    lo = jnp.zeros((h2_ref.shape[0], half), F32)
    hi = jnp.zeros((h2_ref.shape[0], half), F32)
    for k in range(TOP_K):
        a, b = _unpack_pairs(yt_ref[:, k * half:(k + 1) * half])
        lo = lo + wt[:, k:k + 1] * a
        hi = hi + wt[:, k:k + 1] * b
    y = h2_ref[...] + jnp.concatenate([lo, hi], axis=1)
    ms = jnp.mean(y * y, axis=-1, keepdims=True)
    out_ref[...] = y * lax.rsqrt(ms + EPS) * gf_ref[...]


def _combine(h2, y_tok, wts, gain):
    n = h2.shape[0]
    rb = ROW_BLOCK
    rowblk = lambda width: pl.BlockSpec((rb, width), lambda i: (i, 0))
    return pl.pallas_call(
        _combine_kernel,
        grid=(n // rb,),
        in_specs=[rowblk(D_MODEL), rowblk(TOP_K * D_MODEL // 2), rowblk(TOP_K),
                  pl.BlockSpec((1, D_MODEL), lambda i: (0, 0))],
        out_specs=rowblk(D_MODEL),
        out_shape=jax.ShapeDtypeStruct((n, D_MODEL), F32),
        compiler_params=_cparams(("parallel",)),
        name="combine_norm",
    )(h2, y_tok, wts, gain)


def _lower_tri(n, strict):
    r = lax.broadcasted_iota(jnp.int32, (n, n), 0)
    c = lax.broadcasted_iota(jnp.int32, (n, n), 1)
    return ((c < r) if strict else (c <= r)).astype(BF16)


def kernel(x, meta_tokens, norm1_gain, w_in, s5_lambda_re, s5_lambda_im, s5_log_dt, s5_b_re,
           s5_b_im, s5_c_re, s5_c_im, s5_d, s5_w_glu, hgrn_lb_logits, hgrn_norm_gain,
           w_branch_s5, w_branch_hgrn, w_out, norm2_gain, w_router, b_router, w_gate_up,
           b_gate_up, w_down, b_down, final_norm_gain):
    nb, seq, d = x.shape
    n_tok = nb * seq
    assert d == D_MODEL and seq % TIME_BLOCK == 0 and n_tok % ROW_BLOCK == 0
    x_rows = x.reshape(n_tok, d)
    w_in_b = w_in[0].astype(BF16)
    g1 = norm1_gain[0].reshape(1, d).astype(F32)

    u, qfvg, gates = _inproj(x_rows, g1, w_in_b, ROW_BLOCK)
    meta_rows = jnp.concatenate(
        [jnp.zeros((META_PAD - N_META, d), F32), meta_tokens.astype(F32)], axis=0)
    u_m, qfvg_m, _ = _inproj(meta_rows, g1, w_in_b, META_PAD)

    bm, cm, tab = _s5_params(s5_lambda_re[0], s5_lambda_im[0], s5_log_dt[0], s5_b_re[0],
                             s5_b_im[0], s5_c_re[0], s5_c_im[0])
    y_s5 = _s5_mixer(u.reshape(nb, seq, S5_WIDTH), u_m, bm, cm, tab,
                     s5_d[0].reshape(1, S5_WIDTH).astype(F32), s5_w_glu[0].astype(BF16))

    lower_bounds = jnp.cumsum(jax.nn.softmax(hgrn_lb_logits.astype(F32), axis=0), axis=0)
    lb = lower_bounds[0].reshape(1, HG_WIDTH)
    y_hg = _hg_mixer(qfvg.reshape(nb, seq, 4 * HG_WIDTH), qfvg_m, lb,
                     hgrn_norm_gain[0].reshape(1, HG_WIDTH).astype(F32),
                     _lower_tri(TIME_BLOCK, strict=False))

    wr = jnp.zeros((d, LANES), F32).at[:, :N_EXPERTS].set(w_router[0].astype(F32))
    wr_hi = wr.astype(BF16)
    wr_lo = (wr - wr_hi.astype(F32)).astype(BF16)
    br = jnp.full((1, LANES), NEG_BIG, F32).at[0, :N_EXPERTS].set(b_router[0].astype(F32))
    h2, xn_packed, eid, wts, rank, counts = _merge_router(
        x_rows, y_s5.reshape(n_tok, S5_WIDTH), y_hg.reshape(n_tok, HG_WIDTH), gates,
        w_branch_s5[0].astype(BF16), w_branch_hgrn[0].astype(BF16), w_out[0].astype(BF16),
        norm2_gain[0].reshape(1, d).astype(F32), wr_hi, wr_lo, br,
        _lower_tri(ROW_BLOCK, strict=True))

    n_assign = n_tok * TOP_K
    n_blocks = n_assign // MOE_BLOCK + N_EXPERTS
    n_rows = n_blocks * MOE_BLOCK
    n_rows_pad = -(-n_rows // GATHER_ROWS) * GATHER_ROWS
    cnt = counts[0, :N_EXPERTS].astype(jnp.int32)
    padded = (cnt + MOE_BLOCK - 1) // MOE_BLOCK * MOE_BLOCK
    padded_end = jnp.cumsum(padded)
    padded_start = padded_end - padded
    dest = padded_start[eid] + rank
    tok_ids = jnp.broadcast_to(jnp.arange(n_tok, dtype=jnp.int32)[:, None], (n_tok, TOP_K))
    row_tok = jnp.zeros((n_rows_pad,), jnp.int32).at[dest.reshape(-1)].set(tok_ids.reshape(-1))
    block_start = jnp.arange(n_rows_pad // MOE_BLOCK, dtype=jnp.int32) * MOE_BLOCK
    block_e = jnp.minimum(jnp.searchsorted(padded_end, block_start, side='right'),
                          N_EXPERTS - 1).astype(jnp.int32)
    n_valid = (padded_end[-1] // MOE_BLOCK).astype(jnp.int32).reshape(1)

    x_sorted = _gather_rows(xn_packed, row_tok)
    hw = GU_GROUP // 2
    pr = jnp.arange(GU_GROUP)
    src = jnp.where(pr < hw, 2 * pr, 2 * (pr - hw) + 1)
    perm = (jnp.arange(GU_GROUP)[:, None] == src[None, :]).astype(BF16)
    b_gu = b_gate_up[0].astype(F32).reshape(N_EXPERTS, 2 * D_EXPERT // GU_GROUP, hw, 2)
    b_gu = b_gu.transpose(0, 1, 3, 2).reshape(N_EXPERTS, 1, 2 * D_EXPERT)
    y_sorted = _experts(block_e, n_valid, x_sorted, w_gate_up[0], b_gu, w_down[0],
                        b_down[0].astype(F32).reshape(N_EXPERTS, 1, d), perm)
    y_tok = _gather_rows(y_sorted, dest.reshape(-1))

    out = _combine(h2, y_tok.reshape(n_tok, TOP_K * d // 2), wts,
                   final_norm_gain.reshape(1, d).astype(F32))
    return out.reshape(nb, seq, d)
```

```python
import functools

import jax
import jax.numpy as jnp
from jax import lax
from jax.experimental import pallas as pl
from jax.experimental.pallas import tpu as pltpu
from jax.experimental.pallas import tpu_sc as plsc

F32 = jnp.float32
BF16 = jnp.bfloat16

D_MODEL = 1024
N_META = 16
S5_GROUP = 16
S5_GROUPS = 32
S5_WIDTH = 512
S5_STATE = 64
HG_HEADS = 4
HG_HEAD_DIM = 128
HG_WIDTH = 512
N_EXPERTS = 32
TOP_K = 4
D_EXPERT = 1024
SWIGLU_ALPHA = 1.702
SWIGLU_LIMIT = 7.0
EPS = 1e-6

LANES = 128
SUBLANES = 8
META_PAD = 128
TIME_BLOCK = 256
ROW_BLOCK = 512
MOE_BLOCK = 256
GATHER_WINDOW = 64
INDEX_BLOCK = 128
INDEX_SPLIT = INDEX_BLOCK // GATHER_WINDOW
SC_SUBCORES = 32
GATHER_ROWS = INDEX_BLOCK * SC_SUBCORES
HG_DIAG = 32
NEG_BIG = -1e30
VMEM_LIMIT = 56 * 1024 * 1024


def _cparams(sem):
    return pltpu.CompilerParams(dimension_semantics=sem, vmem_limit_bytes=VMEM_LIMIT)


def _sigmoid(x):
    return 1.0 / (1.0 + jnp.exp(-x))


def _inproj_kernel(x_ref, g_ref, w_ref, u_ref, qfvg_ref, gates_ref):
    x = x_ref[...]
    ms = jnp.mean(x * x, axis=-1, keepdims=True)
    xn = (x * lax.rsqrt(ms + EPS) * g_ref[...]).astype(BF16)
    u_ref[...] = jnp.dot(xn, w_ref[:, 0:S5_WIDTH], preferred_element_type=F32).astype(BF16)
    a, b = S5_WIDTH, S5_WIDTH + 4 * HG_WIDTH
    qfvg_ref[...] = jnp.dot(xn, w_ref[:, a:b], preferred_element_type=F32).astype(BF16)
    gates_ref[...] = jnp.dot(xn, w_ref[:, b:], preferred_element_type=F32).astype(BF16)


def _inproj(rows, gain, w_in_bf16, row_block):
    n = rows.shape[0]
    n_gate = 2 * D_MODEL
    return pl.pallas_call(
        _inproj_kernel,
        grid=(n // row_block,),
        in_specs=[
            pl.BlockSpec((row_block, D_MODEL), lambda i: (i, 0)),
            pl.BlockSpec((1, D_MODEL), lambda i: (0, 0)),
            pl.BlockSpec(w_in_bf16.shape, lambda i: (0, 0)),
        ],
        out_specs=[
            pl.BlockSpec((row_block, S5_WIDTH), lambda i: (i, 0)),
            pl.BlockSpec((row_block, 4 * HG_WIDTH), lambda i: (i, 0)),
            pl.BlockSpec((row_block, n_gate), lambda i: (i, 0)),
        ],
        out_shape=[
            jax.ShapeDtypeStruct((n, S5_WIDTH), BF16),
            jax.ShapeDtypeStruct((n, 4 * HG_WIDTH), BF16),
            jax.ShapeDtypeStruct((n, n_gate), BF16),
        ],
        compiler_params=_cparams(("parallel",)),
        name="inproj",
    )(rows, gain, w_in_bf16)


N_STATE = S5_GROUPS * S5_STATE
HALF_STATE = N_STATE // 2
HALF_COLS = 2 * HALF_STATE


def _s5_scan(xs_ref, tab_ref, car_ref, chains, n_tiles):
    for j in range(N_STATE // LANES):
        half, m0 = divmod(j * LANES, HALF_STATE)
        cre = half * HALF_COLS + m0
        cim = cre + HALF_STATE
        n0 = j * LANES
        tabs = [tab_ref[i, :, n0:n0 + LANES] for i in range(8)]
        init = tuple((car_ref[c, :, cre:cre + LANES], car_ref[c, :, cim:cim + LANES])
                     for _, c in chains)

        def body(i, carry, cre=cre, cim=cim, tabs=tabs):
            new = []
            for (row0, _), (cr, ci) in zip(chains, carry):
                r0 = pl.multiple_of(row0 + i * SUBLANES, SUBLANES)
                xr = xs_ref[pl.ds(r0, SUBLANES), cre:cre + LANES]
                xi = xs_ref[pl.ds(r0, SUBLANES), cim:cim + LANES]
                for k, shift in enumerate((1, 2, 4)):
                    ar, ai = tabs[2 * k], tabs[2 * k + 1]
                    rr = pltpu.roll(xr, shift, 0)
                    ri = pltpu.roll(xi, shift, 0)
                    xr, xi = xr + ar * rr - ai * ri, xi + ar * ri + ai * rr
                pr, pi = tabs[6], tabs[7]
                xr, xi = xr + pr * cr - pi * ci, xi + pr * ci + pi * cr
                xs_ref[pl.ds(r0, SUBLANES), cre:cre + LANES] = xr
                xs_ref[pl.ds(r0, SUBLANES), cim:cim + LANES] = xi
                new.append((jnp.broadcast_to(xr[SUBLANES - 1:SUBLANES, :], (SUBLANES, LANES)),
                            jnp.broadcast_to(xi[SUBLANES - 1:SUBLANES, :], (SUBLANES, LANES))))
            return tuple(new)

        fin = lax.fori_loop(0, n_tiles, body, init)
        for (_, c), (cr, ci) in zip(chains, fin):
            car_ref[c, :, cre:cre + LANES] = cr
            car_ref[c, :, cim:cim + LANES] = ci


def _s5_kernel(u_ref, um_ref, bm_ref, cm_ref, tab_ref, d_ref, wglu_ref, y_ref, xs_ref, car_ref,
               *, nb, tb):
    half_ch = S5_WIDTH // 2

    @pl.when(pl.program_id(0) == 0)
    def _():
        car_ref[...] = jnp.zeros_like(car_ref)
        um = um_ref[...]
        for hf in range(2):
            xs_ref[0:META_PAD, hf * HALF_COLS:(hf + 1) * HALF_COLS] = jnp.dot(
                um[:, hf * half_ch:(hf + 1) * half_ch], bm_ref[hf], preferred_element_type=F32)
        _s5_scan(xs_ref, tab_ref, car_ref, [(0, 0)], META_PAD // SUBLANES)
        for b in range(1, nb):
            car_ref[b] = car_ref[0]

    u = u_ref[...].reshape(nb * tb, S5_WIDTH)
    for hf in range(2):
        xs_ref[:, hf * HALF_COLS:(hf + 1) * HALF_COLS] = jnp.dot(
            u[:, hf * half_ch:(hf + 1) * half_ch], bm_ref[hf], preferred_element_type=F32)
    _s5_scan(xs_ref, tab_ref, car_ref, [(b * tb, b) for b in range(nb)], tb // SUBLANES)
    ys = [jnp.dot(xs_ref[:, hf * HALF_COLS:(hf + 1) * HALF_COLS].astype(BF16), cm_ref[hf],
                  preferred_element_type=F32) for hf in range(2)]
    y = jnp.concatenate(ys, axis=1) + d_ref[...] * u.astype(F32)
    y = 0.5 * y * (1.0 + jnp.tanh(0.7978845608028654 * (y + 0.044715 * (y * y * y))))
    z = jnp.dot(y.astype(BF16), wglu_ref[...], preferred_element_type=F32)
    y_ref[...] = (y * _sigmoid(z)).astype(BF16).reshape(nb, tb, S5_WIDTH)


def _s5_mixer(u, u_meta, bm, cm, tab, d_skip, w_glu_bf16):
    nb, seq, _ = u.shape
    tb = TIME_BLOCK
    kern = functools.partial(_s5_kernel, nb=nb, tb=tb)
    const2 = lambda t: (0, 0)
    const3 = lambda t: (0, 0, 0)
    return pl.pallas_call(
        kern,
        grid=(seq // tb,),
        in_specs=[
            pl.BlockSpec((nb, tb, S5_WIDTH), lambda t: (0, t, 0)),
            pl.BlockSpec(u_meta.shape, const2),
            pl.BlockSpec(bm.shape, const3),
            pl.BlockSpec(cm.shape, const3),
            pl.BlockSpec(tab.shape, const3),
            pl.BlockSpec(d_skip.shape, const2),
            pl.BlockSpec(w_glu_bf16.shape, const2),
        ],
        out_specs=pl.BlockSpec((nb, tb, S5_WIDTH), lambda t: (0, t, 0)),
        out_shape=jax.ShapeDtypeStruct((nb, seq, S5_WIDTH), BF16),
        scratch_shapes=[
            pltpu.VMEM((nb * tb, 2 * N_STATE), F32),
            pltpu.VMEM((nb, SUBLANES, 2 * N_STATE), F32),
        ],
        compiler_params=_cparams(("arbitrary",)),
        name="s5_mixer",
    )(u, u_meta, bm, cm, tab, d_skip, w_glu_bf16)


def _s5_params(lam_re, lam_im, log_dt, b_re, b_im, c_re, c_im):
    lam = lax.complex(lam_re.astype(F32), lam_im.astype(F32))
    dt = jnp.exp(log_dt.astype(F32))[:, None]
    lam_dt = lam * dt
    lam_bar = jnp.exp(lam_dt)
    b_bar = ((lam_bar - 1.0) / lam)[:, :, None] * lax.complex(b_re.astype(F32), b_im.astype(F32))
    gl = S5_GROUPS // 2
    eye = jnp.eye(gl, dtype=F32)

    def in_half(bh):
        def blk(part):
            t = jnp.einsum('gph,gk->ghkp', part, eye)
            return t.reshape(gl * S5_GROUP, gl * S5_STATE)
        return jnp.concatenate([blk(jnp.real(bh)), blk(jnp.imag(bh))], axis=1)

    def out_half(cr, ci):
        def blk(part):
            t = jnp.einsum('ghp,gk->gpkh', part, eye)
            return t.reshape(gl * S5_STATE, gl * S5_GROUP)
        return jnp.concatenate([blk(cr), blk(-ci)], axis=0)

    bm = jnp.stack([in_half(b_bar[:gl]), in_half(b_bar[gl:])]).astype(BF16)
    cm = jnp.stack([out_half(c_re[:gl].astype(F32), c_im[:gl].astype(F32)),
                    out_half(c_re[gl:].astype(F32), c_im[gl:].astype(F32))]).astype(BF16)
    lam_flat = lam_dt.reshape(1, N_STATE)
    rows = jnp.arange(SUBLANES, dtype=F32)[:, None]
    tabs = []
    for shift in (1, 2, 4):
        p = jnp.exp(lam_flat * float(shift)) * (rows >= shift).astype(F32)
        tabs += [jnp.real(p), jnp.imag(p)]
    p = jnp.exp(lam_flat * (rows + 1.0))
    tabs += [jnp.real(p), jnp.imag(p)]
    tab = jnp.stack([jnp.broadcast_to(t, (SUBLANES, N_STATE)) for t in tabs]).astype(F32)
    return bm, cm, tab


def _dot_nt(a, b):
    return lax.dot_general(a, b, (((1,), (1,)), ((), ())), preferred_element_type=F32)


def _dot_tn(a, b):
    return lax.dot_general(a, b, (((0,), (0,)), ((), ())), preferred_element_type=F32)


def _cumsum_rows(tri, x):
    hi = x.astype(BF16)
    r1 = x - hi.astype(F32)
    mid = r1.astype(BF16)
    lo = (r1 - mid.astype(F32)).astype(BF16)
    return (jnp.dot(tri, hi, preferred_element_type=F32)
            + jnp.dot(tri, mid, preferred_element_type=F32)
            + jnp.dot(tri, lo, preferred_element_type=F32))


def _segment_rows(x, seg, pos):
    n = x.shape[0]
    parts = [jnp.broadcast_to(x[s * seg + pos:s * seg + pos + 1, :], (seg, x.shape[1]))
             for s in range(n // seg)]
    return parts[0] if len(parts) == 1 else jnp.concatenate(parts, axis=0)


def _hg_gates(x, lb, row_valid=None):
    w = HG_WIDTH
    q = x[:, 0:w]
    f = lb + (1.0 - lb) * _sigmoid(x[:, w:2 * w])
    logf = jnp.log(f)
    k = 1.0 - f
    if row_valid is not None:
        logf = jnp.where(row_valid, logf, 0.0)
        k = jnp.where(row_valid, k, 0.0)
    return q * _sigmoid(q), k, logf


def _hg_kernel(x_ref, xm_ref, lb_ref, gain_ref, tri_ref, y_ref, st_ref, *, nb, tb):
    w, dh = HG_WIDTH, HG_HEAD_DIM
    lb = lb_ref[...]
    tri = tri_ref[...]

    @pl.when(pl.program_id(0) == 0)
    def _():
        xm = xm_ref[...].astype(F32)
        valid = lax.broadcasted_iota(jnp.int32, (META_PAD, 1), 0) >= (META_PAD - N_META)
        _, k, logf = _hg_gates(xm, lb, valid)
        bc = _cumsum_rows(tri[0:META_PAD, 0:META_PAD], logf)
        ki = (k * jnp.exp(bc[META_PAD - 1:META_PAD, :] - bc)).astype(BF16)
        v = xm[:, 2 * w:3 * w].astype(BF16)
        for h in range(HG_HEADS):
            sl = slice(h * dh, (h + 1) * dh)
            s0 = _dot_tn(v[:, sl], ki[:, sl])
            for b in range(nb):
                st_ref[b, h] = s0

    ri = lax.broadcasted_iota(jnp.int32, (tb, tb), 0)
    ci = lax.broadcasted_iota(jnp.int32, (tb, tb), 1)
    diag_shift = HG_DIAG.bit_length() - 1
    diag_mask = ((ri >> diag_shift) == (ci >> diag_shift)) & (ci <= ri)
    levels = []
    seg = 2 * HG_DIAG
    while seg <= tb:
        levels.append(seg)
        seg *= 2
    seg_masks = [None if s == tb else
                 ((ri >> (s.bit_length() - 1)) == (ci >> (s.bit_length() - 1))).astype(F32)
                 for s in levels]
    row = lax.broadcasted_iota(jnp.int32, (tb, 1), 0)

    for b in range(nb):
        x = x_ref[b].astype(F32)
        qs, k, logf = _hg_gates(x, lb)
        v = x[:, 2 * w:3 * w].astype(BF16)
        g = x[:, 3 * w:4 * w]
        bc = _cumsum_rows(tri, logf)
        dlt = bc - _segment_rows(bc, HG_DIAG, HG_DIAG // 2 - 1)
        qk = [((qs * jnp.exp(dlt)).astype(BF16), (k * jnp.exp(-dlt)).astype(BF16))]
        for s in levels:
            e = jnp.exp(-jnp.abs(bc - _segment_rows(bc, s, s // 2 - 1)))
            upper = (row & (s - 1)) >= (s // 2)
            qk.append((jnp.where(upper, qs * e, 0.0).astype(BF16),
                       jnp.where(upper, 0.0, k * e).astype(BF16)))
        b_last = bc[tb - 1:tb, :]
        qi = (qs * jnp.exp(bc)).astype(BF16)
        ki = (k * jnp.exp(b_last - bc)).astype(BF16)
        dec = jnp.exp(b_last)
        outs = []
        for h in range(HG_HEADS):
            sl = slice(h * dh, (h + 1) * dh)
            sc = jnp.where(diag_mask, _dot_nt(qk[0][0][:, sl], qk[0][1][:, sl]), 0.0)
            for (ql, kl), m in zip(qk[1:], seg_masks):
                t = _dot_nt(ql[:, sl], kl[:, sl])
                sc = sc + (t if m is None else t * m)
            st = st_ref[b, h]
            o = (jnp.dot(sc.astype(BF16), v[:, sl], preferred_element_type=F32)
                 + _dot_nt(qi[:, sl], st.astype(BF16)))
            st_ref[b, h] = dec[:, sl] * st + _dot_tn(v[:, sl], ki[:, sl])
            ms = jnp.mean(o * o, axis=-1, keepdims=True)
            outs.append(o * lax.rsqrt(ms + EPS))
        o = jnp.concatenate(outs, axis=1) * gain_ref[...]
        y_ref[b] = (o * (g * _sigmoid(g))).astype(BF16)


def _hg_mixer(x, x_meta, lb, gain, tri):
    nb, seq, _ = x.shape
    tb = TIME_BLOCK
    kern = functools.partial(_hg_kernel, nb=nb, tb=tb)
    const2 = lambda t: (0, 0)
    return pl.pallas_call(
        kern,
        grid=(seq // tb,),
        in_specs=[
            pl.BlockSpec((nb, tb, 4 * HG_WIDTH), lambda t: (0, t, 0)),
            pl.BlockSpec(x_meta.shape, const2),
            pl.BlockSpec(lb.shape, const2),
            pl.BlockSpec(gain.shape, const2),
            pl.BlockSpec(tri.shape, const2),
        ],
        out_specs=pl.BlockSpec((nb, tb, HG_WIDTH), lambda t: (0, t, 0)),
        out_shape=jax.ShapeDtypeStruct((nb, seq, HG_WIDTH), BF16),
        scratch_shapes=[pltpu.VMEM((nb, HG_HEADS, HG_HEAD_DIM, HG_HEAD_DIM), F32)],
        compiler_params=_cparams(("arbitrary",)),
        name="hgrn2_mixer",
    )(x, x_meta, lb, gain, tri)


def _split3(x):
    hi = x.astype(BF16)
    lo = (x - hi.astype(F32)).astype(BF16)
    return hi, lo


def _pack_pairs(y):
    n = y.shape[1] // 2
    lo = pltpu.bitcast(y[:, :n].astype(BF16).astype(F32), jnp.uint32)
    hi = pltpu.bitcast(y[:, n:].astype(BF16).astype(F32), jnp.uint32)
    return (lo >> 16) | (hi & jnp.uint32(0xFFFF0000))


def _unpack_pairs(p):
    lo = pltpu.bitcast(p << 16, F32)
    hi = pltpu.bitcast(p & jnp.uint32(0xFFFF0000), F32)
    return lo, hi


def _merge_kernel(x_ref, ys_ref, yh_ref, gt_ref, wbs_ref, wbh_ref, wo_ref, g2_ref, wr_hi_ref,
                  wr_lo_ref, br_ref, tri_ref, h2_ref, xp_ref, eid_ref, wt_ref, rank_ref, cnt_ref,
                  run_ref):
    @pl.when(pl.program_id(0) == 0)
    def _():
        run_ref[...] = jnp.zeros_like(run_ref)

    gt = gt_ref[...].astype(F32)
    gs = _sigmoid(gt[:, :D_MODEL])
    gh = _sigmoid(gt[:, D_MODEL:])
    merged = (gs * jnp.dot(ys_ref[...], wbs_ref[...], preferred_element_type=F32)
              + gh * jnp.dot(yh_ref[...], wbh_ref[...], preferred_element_type=F32))
    h2 = x_ref[...] + jnp.dot(merged.astype(BF16), wo_ref[...], preferred_element_type=F32)
    h2_ref[...] = h2
    ms = jnp.mean(h2 * h2, axis=-1, keepdims=True)
    xn = h2 * lax.rsqrt(ms + EPS) * g2_ref[...]
    xp_ref[...] = _pack_pairs(xn)

    x_hi, x_lo = _split3(xn)
    logits = (jnp.dot(x_hi, wr_hi_ref[...], preferred_element_type=F32)
              + jnp.dot(x_hi, wr_lo_ref[...], preferred_element_type=F32)
              + jnp.dot(x_lo, wr_hi_ref[...], preferred_element_type=F32)) + br_ref[...]
    rows = logits.shape[0]
    lane = lax.broadcasted_iota(jnp.int32, (rows, LANES), 1)
    tops, hots = [], []
    sel = jnp.zeros((rows, LANES), F32)
    for k in range(TOP_K):
        m = jnp.max(logits, axis=-1, keepdims=True)
        idx = jnp.min(jnp.where(logits == m, lane, LANES), axis=-1, keepdims=True)
        hot = lane == idx
        logits = jnp.where(hot, NEG_BIG, logits)
        tops.append(m)
        hots.append(hot)
        sel = sel + hot.astype(F32)
        eid_ref[:, k:k + 1] = idx
    es = [jnp.exp(m - tops[0]) for m in tops]
    tot = es[0] + es[1] + es[2] + es[3]
    for k in range(TOP_K):
        wt_ref[:, k:k + 1] = es[k] / tot
    prefix = jnp.dot(tri_ref[...], sel.astype(BF16), preferred_element_type=F32) + run_ref[...]
    for k in range(TOP_K):
        r = jnp.sum(jnp.where(hots[k], prefix, 0.0), axis=-1, keepdims=True)
        rank_ref[:, k:k + 1] = r.astype(jnp.int32)
    run = run_ref[...] + jnp.sum(sel, axis=0, keepdims=True)
    run_ref[...] = run
    cnt_ref[...] = run


def _merge_router(x_rows, y_s5, y_hg, gates, wbs, wbh, wo, g2, wr_hi, wr_lo, br, tri_strict):
    n = x_rows.shape[0]
    rb = ROW_BLOCK
    const2 = lambda i: (0, 0)
    rowblk = lambda width: pl.BlockSpec((rb, width), lambda i: (i, 0))
    return pl.pallas_call(
        _merge_kernel,
        grid=(n // rb,),
        in_specs=[
            rowblk(D_MODEL), rowblk(S5_WIDTH), rowblk(HG_WIDTH), rowblk(2 * D_MODEL),
            pl.BlockSpec(wbs.shape, const2), pl.BlockSpec(wbh.shape, const2),
            pl.BlockSpec(wo.shape, const2), pl.BlockSpec(g2.shape, const2),
            pl.BlockSpec(wr_hi.shape, const2), pl.BlockSpec(wr_lo.shape, const2),
            pl.BlockSpec(br.shape, const2), pl.BlockSpec(tri_strict.shape, const2),
        ],
        out_specs=[
            rowblk(D_MODEL), rowblk(D_MODEL // 2), rowblk(TOP_K), rowblk(TOP_K), rowblk(TOP_K),
            pl.BlockSpec((1, LANES), const2),
        ],
        out_shape=[
            jax.ShapeDtypeStruct((n, D_MODEL), F32),
            jax.ShapeDtypeStruct((n, D_MODEL // 2), jnp.uint32),
            jax.ShapeDtypeStruct((n, TOP_K), jnp.int32),
            jax.ShapeDtypeStruct((n, TOP_K), F32),
            jax.ShapeDtypeStruct((n, TOP_K), jnp.int32),
            jax.ShapeDtypeStruct((1, LANES), F32),
        ],
        scratch_shapes=[pltpu.VMEM((1, LANES), F32)],
        compiler_params=_cparams(("arbitrary",)),
        name="merge_router",
    )(x_rows, y_s5, y_hg, gates, wbs, wbh, wo, g2, wr_hi, wr_lo, br, tri_strict)


def _gather_rows(src, idx):
    n = idx.shape[0]
    width = src.shape[1]
    assert n % GATHER_ROWS == 0
    mesh = plsc.VectorSubcoreMesh(core_axis_name="core", subcore_axis_name="subcore")

    @functools.partial(pl.kernel, out_type=jax.ShapeDtypeStruct((n, width), src.dtype),
                       mesh=mesh, scratch_types=[], name="sc_gather_rows")
    def gather(src_hbm, idx_hbm, out_hbm):
        def body(idx_vmem, out_vmem):
            off = pl.multiple_of(pl.program_id(1) * GATHER_WINDOW, GATHER_WINDOW)
            pltpu.sync_copy(src_hbm.at[idx_vmem.at[0, pl.ds(off, GATHER_WINDOW)]], out_vmem)

        pltpu.emit_pipeline(
            body,
            grid=(n // INDEX_BLOCK, INDEX_SPLIT),
            in_specs=[pl.BlockSpec((1, INDEX_BLOCK), lambda i, j: (0, i))],
            out_specs=[pl.BlockSpec((GATHER_WINDOW, width), lambda i, j: (INDEX_SPLIT * i + j, 0))],
            core_axis_name=("core", "subcore"),
            dimension_semantics=(pltpu.PARALLEL, pltpu.ARBITRARY),
        )(idx_hbm, out_hbm)

    return gather(src, idx.reshape(1, n))


def _scatter_rows(src, dest, n_out):
    n_src, width = src.shape
    n = dest.shape[0]
    assert n == TOP_K * n_src and n % GATHER_ROWS == 0 and n_src % GATHER_WINDOW == 0
    steps_per_copy = n_src // GATHER_WINDOW
    mesh = plsc.VectorSubcoreMesh(core_axis_name="core", subcore_axis_name="subcore")

    @functools.partial(pl.kernel, out_type=jax.ShapeDtypeStruct((n_out, width), src.dtype),
                       mesh=mesh, scratch_types=[], name="sc_scatter_rows")
    def scatter(src_hbm, idx_hbm, out_hbm):
        def body(src_vmem, idx_vmem):
            off = pl.multiple_of(pl.program_id(1) * GATHER_WINDOW, GATHER_WINDOW)
            pltpu.sync_copy(src_vmem, out_hbm.at[idx_vmem.at[0, pl.ds(off, GATHER_WINDOW)]])

        pltpu.emit_pipeline(
            body,
            grid=(n // INDEX_BLOCK, INDEX_SPLIT),
            in_specs=[pl.BlockSpec((GATHER_WINDOW, width),
                                   lambda i, j: ((INDEX_SPLIT * i + j) % steps_per_copy, 0)),
                      pl.BlockSpec((1, INDEX_BLOCK), lambda i, j: (0, i))],
            out_specs=[],
            core_axis_name=("core", "subcore"),
            dimension_semantics=(pltpu.PARALLEL, pltpu.ARBITRARY),
        )(src_hbm, idx_hbm)

    return scatter(src, dest.reshape(1, n))


GU_GROUP = 512


def _expert_kernel(be_ref, nv_ref, nr_ref, x_ref, wgu_ref, bgu_ref, wd_ref, bd_ref, perm_ref,
                   y_ref, wgu_s, wd_s):
    i = pl.program_id(0)
    prev = be_ref[jnp.maximum(i - 1, 0)]
    fresh = (i == 0) | (be_ref[i] != prev)

    @pl.when(fresh & (i < nv_ref[0]))
    def _():
        for c in range(2 * D_EXPERT // GU_GROUP):
            cols = slice(c * GU_GROUP, (c + 1) * GU_GROUP)
            w = wgu_ref[0, :, cols].astype(BF16)
            wgu_s[:, cols] = jnp.dot(w, perm_ref[...], preferred_element_type=F32).astype(BF16)
        wd_s[...] = wd_ref[0].astype(BF16)

    @pl.when(i < nv_ref[0])
    def _():
        half = D_MODEL // 2
        live = lax.broadcasted_iota(jnp.int32, (MOE_BLOCK, 1), 0) < nr_ref[i]
        xa, xb = _unpack_pairs(jnp.where(live, x_ref[...], jnp.uint32(0)))
        xa = xa.astype(BF16)
        xb = xb.astype(BF16)
        acc = jnp.zeros((MOE_BLOCK, D_MODEL), F32) + bd_ref[0]
        hw = GU_GROUP // 2
        for c in range(2 * D_EXPERT // GU_GROUP):
            cols = slice(c * GU_GROUP, (c + 1) * GU_GROUP)
            gu = (jnp.dot(xa, wgu_s[0:half, cols], preferred_element_type=F32)
                  + jnp.dot(xb, wgu_s[half:, cols], preferred_element_type=F32)
                  + bgu_ref[0, :, cols])
            gate = jnp.minimum(gu[:, :hw], SWIGLU_LIMIT)
            up = jnp.clip(gu[:, hw:], -SWIGLU_LIMIT, SWIGLU_LIMIT)
            hcol = ((up + 1.0) * (gate * _sigmoid(gate * SWIGLU_ALPHA))).astype(BF16)
            acc = acc + jnp.dot(hcol, wd_s[c * hw:(c + 1) * hw, :], preferred_element_type=F32)
        y_ref[...] = _pack_pairs(acc)

    @pl.when(i >= nv_ref[0])
    def _():
        y_ref[...] = jnp.zeros_like(y_ref)


def _experts(block_e, n_valid, block_rows, x_rows, w_gate_up, b_gu_grouped, w_down, b_down, perm):
    n_rows = x_rows.shape[0]
    n_blocks = n_rows // MOE_BLOCK
    half = D_MODEL // 2
    by_expert = lambda i, be, nv, nr: (be[i], 0, 0)
    grid_spec = pltpu.PrefetchScalarGridSpec(
        num_scalar_prefetch=3,
        grid=(n_blocks,),
        in_specs=[
            pl.BlockSpec((MOE_BLOCK, half), lambda i, be, nv, nr: (i, 0)),
            pl.BlockSpec((1, D_MODEL, 2 * D_EXPERT), by_expert),
            pl.BlockSpec((1, 1, 2 * D_EXPERT), by_expert),
            pl.BlockSpec((1, D_EXPERT, D_MODEL), by_expert),
            pl.BlockSpec((1, 1, D_MODEL), by_expert),
            pl.BlockSpec(perm.shape, lambda i, be, nv, nr: (0, 0)),
        ],
        out_specs=pl.BlockSpec((MOE_BLOCK, half), lambda i, be, nv, nr: (i, 0)),
        scratch_shapes=[
            pltpu.VMEM((D_MODEL, 2 * D_EXPERT), BF16),
            pltpu.VMEM((D_EXPERT, D_MODEL), BF16),
        ],
    )
    return pl.pallas_call(
        _expert_kernel,
        grid_spec=grid_spec,
        out_shape=jax.ShapeDtypeStruct((n_rows, half), jnp.uint32),
        compiler_params=_cparams(("arbitrary",)),
        name="experts",
    )(block_e, n_valid, block_rows, x_rows, w_gate_up, b_gu_grouped, w_down, b_down, perm)


def _combine_kernel(h2_ref, y0_ref, y1_ref, y2_ref, y3_ref, wt_ref, gf_ref, out_ref):
    half = D_MODEL // 2
    wt = wt_ref[...]
    lo = jnp.zeros((h2_ref.shape[0], half), F32)
    hi = jnp.zeros((h2_ref.shape[0], half), F32)
    for k, yk_ref in enumerate((y0_ref, y1_ref, y2_ref, y3_ref)):
        a, b = _unpack_pairs(yk_ref[...])
        lo = lo + wt[:, k:k + 1] * a
        hi = hi + wt[:, k:k + 1] * b
    y = h2_ref[...] + jnp.concatenate([lo, hi], axis=1)
    ms = jnp.mean(y * y, axis=-1, keepdims=True)
    out_ref[...] = y * lax.rsqrt(ms + EPS) * gf_ref[...]


def _combine(h2, y_tok, wts, gain):
    n = h2.shape[0]
    rb = ROW_BLOCK
    rowblk = lambda width: pl.BlockSpec((rb, width), lambda i: (i, 0))
    steps = n // rb
    choice = lambda k: pl.BlockSpec((rb, D_MODEL // 2), lambda i: (k * steps + i, 0))
    return pl.pallas_call(
        _combine_kernel,
        grid=(steps,),
        in_specs=[rowblk(D_MODEL)] + [choice(k) for k in range(TOP_K)]
        + [rowblk(TOP_K), pl.BlockSpec((1, D_MODEL), lambda i: (0, 0))],
        out_specs=rowblk(D_MODEL),
        out_shape=jax.ShapeDtypeStruct((n, D_MODEL), F32),
        compiler_params=_cparams(("parallel",)),
        name="combine_norm",
    )(h2, y_tok, y_tok, y_tok, y_tok, wts, gain)


def _lower_tri(n, strict):
    r = lax.broadcasted_iota(jnp.int32, (n, n), 0)
    c = lax.broadcasted_iota(jnp.int32, (n, n), 1)
    return ((c < r) if strict else (c <= r)).astype(BF16)


def kernel(x, meta_tokens, norm1_gain, w_in, s5_lambda_re, s5_lambda_im, s5_log_dt, s5_b_re,
           s5_b_im, s5_c_re, s5_c_im, s5_d, s5_w_glu, hgrn_lb_logits, hgrn_norm_gain,
           w_branch_s5, w_branch_hgrn, w_out, norm2_gain, w_router, b_router, w_gate_up,
           b_gate_up, w_down, b_down, final_norm_gain):
    nb, seq, d = x.shape
    n_tok = nb * seq
    assert d == D_MODEL and seq % TIME_BLOCK == 0 and n_tok % ROW_BLOCK == 0
    x_rows = x.reshape(n_tok, d)
    w_in_b = w_in[0].astype(BF16)
    g1 = norm1_gain[0].reshape(1, d).astype(F32)

    u, qfvg, gates = _inproj(x_rows, g1, w_in_b, ROW_BLOCK)
    meta_rows = jnp.concatenate(
        [jnp.zeros((META_PAD - N_META, d), F32), meta_tokens.astype(F32)], axis=0)
    u_m, qfvg_m, _ = _inproj(meta_rows, g1, w_in_b, META_PAD)

    bm, cm, tab = _s5_params(s5_lambda_re[0], s5_lambda_im[0], s5_log_dt[0], s5_b_re[0],
                             s5_b_im[0], s5_c_re[0], s5_c_im[0])
    y_s5 = _s5_mixer(u.reshape(nb, seq, S5_WIDTH), u_m, bm, cm, tab,
                     s5_d[0].reshape(1, S5_WIDTH).astype(F32), s5_w_glu[0].astype(BF16))

    lower_bounds = jnp.cumsum(jax.nn.softmax(hgrn_lb_logits.astype(F32), axis=0), axis=0)
    lb = lower_bounds[0].reshape(1, HG_WIDTH)
    y_hg = _hg_mixer(qfvg.reshape(nb, seq, 4 * HG_WIDTH), qfvg_m, lb,
                     hgrn_norm_gain[0].reshape(1, HG_WIDTH).astype(F32),
                     _lower_tri(TIME_BLOCK, strict=False))

    wr = jnp.zeros((d, LANES), F32).at[:, :N_EXPERTS].set(w_router[0].astype(F32))
    wr_hi = wr.astype(BF16)
    wr_lo = (wr - wr_hi.astype(F32)).astype(BF16)
    br = jnp.full((1, LANES), NEG_BIG, F32).at[0, :N_EXPERTS].set(b_router[0].astype(F32))
    h2, xn_packed, eid, wts, rank, counts = _merge_router(
        x_rows, y_s5.reshape(n_tok, S5_WIDTH), y_hg.reshape(n_tok, HG_WIDTH), gates,
        w_branch_s5[0].astype(BF16), w_branch_hgrn[0].astype(BF16), w_out[0].astype(BF16),
        norm2_gain[0].reshape(1, d).astype(F32), wr_hi, wr_lo, br,
        _lower_tri(ROW_BLOCK, strict=True))

    n_assign = n_tok * TOP_K
    n_blocks = n_assign // MOE_BLOCK + N_EXPERTS
    n_rows = n_blocks * MOE_BLOCK
    n_rows_pad = -(-n_rows // GATHER_ROWS) * GATHER_ROWS
    cnt = counts[0, :N_EXPERTS].astype(jnp.int32)
    padded = (cnt + MOE_BLOCK - 1) // MOE_BLOCK * MOE_BLOCK
    padded_end = jnp.cumsum(padded)
    padded_start = padded_end - padded
    onehot = (eid[:, :, None] == jnp.arange(N_EXPERTS, dtype=jnp.int32)).astype(jnp.int32)
    dest = (jnp.sum(onehot * padded_start, axis=-1) + rank).T.reshape(-1)
    block_start = jnp.arange(n_rows_pad // MOE_BLOCK, dtype=jnp.int32) * MOE_BLOCK
    block_e = jnp.minimum(jnp.sum((block_start[:, None] >= padded_end[None, :]).astype(jnp.int32),
                                  axis=1), N_EXPERTS - 1)
    block_rows = jnp.clip(cnt[block_e] - (block_start - padded_start[block_e]), 0, MOE_BLOCK)
    n_valid = (padded_end[-1] // MOE_BLOCK).astype(jnp.int32).reshape(1)

    x_sorted = _scatter_rows(xn_packed, dest, n_rows_pad)
    hw = GU_GROUP // 2
    pr = jnp.arange(GU_GROUP)
    src = jnp.where(pr < hw, 2 * pr, 2 * (pr - hw) + 1)
    perm = (jnp.arange(GU_GROUP)[:, None] == src[None, :]).astype(BF16)
    b_gu = b_gate_up[0].astype(F32).reshape(N_EXPERTS, 2 * D_EXPERT // GU_GROUP, hw, 2)
    b_gu = b_gu.transpose(0, 1, 3, 2).reshape(N_EXPERTS, 1, 2 * D_EXPERT)
    y_sorted = _experts(block_e, n_valid, block_rows, x_sorted, w_gate_up[0], b_gu, w_down[0],
                        b_down[0].astype(F32).reshape(N_EXPERTS, 1, d), perm)
    y_tok = _gather_rows(y_sorted, dest)

    out = _combine(h2, y_tok, wts, final_norm_gain.reshape(1, d).astype(F32))
    return out.reshape(nb, seq, d)
```

```python
import functools

import jax
import jax.numpy as jnp
from jax import lax
from jax.experimental import pallas as pl
from jax.experimental.pallas import tpu as pltpu
from jax.experimental.pallas import tpu_sc as plsc

F32 = jnp.float32
BF16 = jnp.bfloat16

D_MODEL = 1024
N_META = 16
S5_GROUP = 16
S5_GROUPS = 32
S5_WIDTH = 512
S5_STATE = 64
HG_HEADS = 4
HG_HEAD_DIM = 128
HG_WIDTH = 512
N_EXPERTS = 32
TOP_K = 4
D_EXPERT = 1024
SWIGLU_ALPHA = 1.702
SWIGLU_LIMIT = 7.0
EPS = 1e-6

LANES = 128
SUBLANES = 8
META_PAD = 128
TIME_BLOCK = 256
ROW_BLOCK = 512
MOE_BLOCK = 512
GATHER_WINDOW = 64
INDEX_BLOCK = 128
INDEX_SPLIT = INDEX_BLOCK // GATHER_WINDOW
SC_SUBCORES = 32
GATHER_ROWS = INDEX_BLOCK * SC_SUBCORES
HG_DIAG = 32
NEG_BIG = -1e30
VMEM_LIMIT = 56 * 1024 * 1024


def _cparams(sem):
    return pltpu.CompilerParams(dimension_semantics=sem, vmem_limit_bytes=VMEM_LIMIT)


def _sigmoid(x):
    return 1.0 / (1.0 + jnp.exp(-x))


def _inproj_kernel(x_ref, g_ref, w_ref, u_ref, qfvg_ref, gates_ref):
    x = x_ref[...]
    ms = jnp.mean(x * x, axis=-1, keepdims=True)
    xn = (x * lax.rsqrt(ms + EPS) * g_ref[...]).astype(BF16)
    u_ref[...] = jnp.dot(xn, w_ref[:, 0:S5_WIDTH], preferred_element_type=F32).astype(BF16)
    a, b = S5_WIDTH, S5_WIDTH + 4 * HG_WIDTH
    qfvg_ref[...] = jnp.dot(xn, w_ref[:, a:b], preferred_element_type=F32).astype(BF16)
    gates_ref[...] = jnp.dot(xn, w_ref[:, b:], preferred_element_type=F32).astype(BF16)


def _inproj(rows, gain, w_in_bf16, row_block):
    n = rows.shape[0]
    n_gate = 2 * D_MODEL
    return pl.pallas_call(
        _inproj_kernel,
        grid=(n // row_block,),
        in_specs=[
            pl.BlockSpec((row_block, D_MODEL), lambda i: (i, 0)),
            pl.BlockSpec((1, D_MODEL), lambda i: (0, 0)),
            pl.BlockSpec(w_in_bf16.shape, lambda i: (0, 0)),
        ],
        out_specs=[
            pl.BlockSpec((row_block, S5_WIDTH), lambda i: (i, 0)),
            pl.BlockSpec((row_block, 4 * HG_WIDTH), lambda i: (i, 0)),
            pl.BlockSpec((row_block, n_gate), lambda i: (i, 0)),
        ],
        out_shape=[
            jax.ShapeDtypeStruct((n, S5_WIDTH), BF16),
            jax.ShapeDtypeStruct((n, 4 * HG_WIDTH), BF16),
            jax.ShapeDtypeStruct((n, n_gate), BF16),
        ],
        compiler_params=_cparams(("parallel",)),
        name="inproj",
    )(rows, gain, w_in_bf16)


N_STATE = S5_GROUPS * S5_STATE
HALF_STATE = N_STATE // 2
HALF_COLS = 2 * HALF_STATE


def _s5_scan(xs_refs, tab_ref, car_ref, n_tiles):
    n_chain = len(xs_refs)
    for j in range(N_STATE // LANES):
        half, m0 = divmod(j * LANES, HALF_STATE)
        cre = half * HALF_COLS + m0
        cim = cre + HALF_STATE
        n0 = j * LANES
        tabs = [tab_ref[i, :, n0:n0 + LANES] for i in range(8)]
        init = tuple((car_ref[c, :, cre:cre + LANES], car_ref[c, :, cim:cim + LANES])
                     for c in range(n_chain))

        def body(i, carry, cre=cre, cim=cim, tabs=tabs):
            r0 = pl.multiple_of(i * SUBLANES, SUBLANES)
            vals = [(ref[pl.ds(r0, SUBLANES), cre:cre + LANES],
                     ref[pl.ds(r0, SUBLANES), cim:cim + LANES]) for ref in xs_refs]
            outs = []
            for (xr, xi), (cr, ci) in zip(vals, carry):
                for k, shift in enumerate((1, 2, 4)):
                    ar, ai = tabs[2 * k], tabs[2 * k + 1]
                    rr = pltpu.roll(xr, shift, 0)
                    ri = pltpu.roll(xi, shift, 0)
                    xr, xi = xr + ar * rr - ai * ri, xi + ar * ri + ai * rr
                pr, pi = tabs[6], tabs[7]
                outs.append((xr + pr * cr - pi * ci, xi + pr * ci + pi * cr))
            for ref, (xr, xi) in zip(xs_refs, outs):
                ref[pl.ds(r0, SUBLANES), cre:cre + LANES] = xr
                ref[pl.ds(r0, SUBLANES), cim:cim + LANES] = xi
            return tuple((jnp.broadcast_to(xr[SUBLANES - 1:SUBLANES, :], (SUBLANES, LANES)),
                          jnp.broadcast_to(xi[SUBLANES - 1:SUBLANES, :], (SUBLANES, LANES)))
                         for xr, xi in outs)

        fin = lax.fori_loop(0, n_tiles, body, init)
        for c, (cr, ci) in enumerate(fin):
            car_ref[c, :, cre:cre + LANES] = cr
            car_ref[c, :, cim:cim + LANES] = ci


def _s5_kernel(u_ref, um_ref, bm_ref, cm_ref, tab_ref, d_ref, wglu_ref, y_ref, car_ref, *xs_refs,
               nb, tb):
    half_ch = S5_WIDTH // 2

    def project_in(u, xs_ref, rows):
        for hf in range(2):
            xs_ref[0:rows, hf * HALF_COLS:(hf + 1) * HALF_COLS] = jnp.dot(
                u[:, hf * half_ch:(hf + 1) * half_ch], bm_ref[hf], preferred_element_type=F32)

    @pl.when(pl.program_id(0) == 0)
    def _():
        car_ref[...] = jnp.zeros_like(car_ref)
        project_in(um_ref[...], xs_refs[0], META_PAD)
        _s5_scan(xs_refs[:1], tab_ref, car_ref, META_PAD // SUBLANES)
        for b in range(1, nb):
            car_ref[b] = car_ref[0]

    for b in range(nb):
        project_in(u_ref[b], xs_refs[b], tb)
    _s5_scan(xs_refs, tab_ref, car_ref, tb // SUBLANES)
    for b in range(nb):
        ys = [jnp.dot(xs_refs[b][:, hf * HALF_COLS:(hf + 1) * HALF_COLS].astype(BF16), cm_ref[hf],
                      preferred_element_type=F32) for hf in range(2)]
        y = jnp.concatenate(ys, axis=1) + d_ref[...] * u_ref[b].astype(F32)
        y = 0.5 * y * (1.0 + jnp.tanh(0.7978845608028654 * (y + 0.044715 * (y * y * y))))
        z = jnp.dot(y.astype(BF16), wglu_ref[...], preferred_element_type=F32)
        y_ref[b] = (y * _sigmoid(z)).astype(BF16)


def _s5_mixer(u, u_meta, bm, cm, tab, d_skip, w_glu_bf16):
    nb, seq, _ = u.shape
    tb = TIME_BLOCK
    kern = functools.partial(_s5_kernel, nb=nb, tb=tb)
    const2 = lambda t: (0, 0)
    const3 = lambda t: (0, 0, 0)
    return pl.pallas_call(
        kern,
        grid=(seq // tb,),
        in_specs=[
            pl.BlockSpec((nb, tb, S5_WIDTH), lambda t: (0, t, 0)),
            pl.BlockSpec(u_meta.shape, const2),
            pl.BlockSpec(bm.shape, const3),
            pl.BlockSpec(cm.shape, const3),
            pl.BlockSpec(tab.shape, const3),
            pl.BlockSpec(d_skip.shape, const2),
            pl.BlockSpec(w_glu_bf16.shape, const2),
        ],
        out_specs=pl.BlockSpec((nb, tb, S5_WIDTH), lambda t: (0, t, 0)),
        out_shape=jax.ShapeDtypeStruct((nb, seq, S5_WIDTH), BF16),
        scratch_shapes=[pltpu.VMEM((nb, SUBLANES, 2 * N_STATE), F32)]
        + [pltpu.VMEM((tb, 2 * N_STATE), F32) for _ in range(nb)],
        compiler_params=_cparams(("arbitrary",)),
        name="s5_mixer",
    )(u, u_meta, bm, cm, tab, d_skip, w_glu_bf16)


def _s5_params(lam_re, lam_im, log_dt, b_re, b_im, c_re, c_im):
    lam = lax.complex(lam_re.astype(F32), lam_im.astype(F32))
    dt = jnp.exp(log_dt.astype(F32))[:, None]
    lam_dt = lam * dt
    lam_bar = jnp.exp(lam_dt)
    b_bar = ((lam_bar - 1.0) / lam)[:, :, None] * lax.complex(b_re.astype(F32), b_im.astype(F32))
    gl = S5_GROUPS // 2
    eye = jnp.eye(gl, dtype=F32)

    def in_half(bh):
        def blk(part):
            t = jnp.einsum('gph,gk->ghkp', part, eye)
            return t.reshape(gl * S5_GROUP, gl * S5_STATE)
        return jnp.concatenate([blk(jnp.real(bh)), blk(jnp.imag(bh))], axis=1)

    def out_half(cr, ci):
        def blk(part):
            t = jnp.einsum('ghp,gk->gpkh', part, eye)
            return t.reshape(gl * S5_STATE, gl * S5_GROUP)
        return jnp.concatenate([blk(cr), blk(-ci)], axis=0)

    bm = jnp.stack([in_half(b_bar[:gl]), in_half(b_bar[gl:])]).astype(BF16)
    cm = jnp.stack([out_half(c_re[:gl].astype(F32), c_im[:gl].astype(F32)),
                    out_half(c_re[gl:].astype(F32), c_im[gl:].astype(F32))]).astype(BF16)
    lam_flat = lam_dt.reshape(1, N_STATE)
    rows = jnp.arange(SUBLANES, dtype=F32)[:, None]
    tabs = []
    for shift in (1, 2, 4):
        p = jnp.exp(lam_flat * float(shift)) * (rows >= shift).astype(F32)
        tabs += [jnp.real(p), jnp.imag(p)]
    p = jnp.exp(lam_flat * (rows + 1.0))
    tabs += [jnp.real(p), jnp.imag(p)]
    tab = jnp.stack([jnp.broadcast_to(t, (SUBLANES, N_STATE)) for t in tabs]).astype(F32)
    return bm, cm, tab


def _dot_nt(a, b):
    return lax.dot_general(a, b, (((1,), (1,)), ((), ())), preferred_element_type=F32)


def _dot_tn(a, b):
    return lax.dot_general(a, b, (((0,), (0,)), ((), ())), preferred_element_type=F32)


def _cumsum_rows(tri, x):
    hi = x.astype(BF16)
    r1 = x - hi.astype(F32)
    mid = r1.astype(BF16)
    lo = (r1 - mid.astype(F32)).astype(BF16)
    return (jnp.dot(tri, hi, preferred_element_type=F32)
            + jnp.dot(tri, mid, preferred_element_type=F32)
            + jnp.dot(tri, lo, preferred_element_type=F32))


def _segment_rows(x, seg, pos):
    n = x.shape[0]
    parts = [jnp.broadcast_to(x[s * seg + pos:s * seg + pos + 1, :], (seg, x.shape[1]))
             for s in range(n // seg)]
    return parts[0] if len(parts) == 1 else jnp.concatenate(parts, axis=0)


def _hg_gates(x, lb, row_valid=None):
    w = HG_WIDTH
    q = x[:, 0:w]
    f = lb + (1.0 - lb) * _sigmoid(x[:, w:2 * w])
    logf = jnp.log(f)
    k = 1.0 - f
    if row_valid is not None:
        logf = jnp.where(row_valid, logf, 0.0)
        k = jnp.where(row_valid, k, 0.0)
    return q * _sigmoid(q), k, logf


def _hg_kernel(x_ref, xm_ref, lb_ref, gain_ref, tri_ref, y_ref, st_ref, *, nb, tb):
    w, dh = HG_WIDTH, HG_HEAD_DIM
    lb = lb_ref[...]
    tri = tri_ref[...]

    @pl.when(pl.program_id(0) == 0)
    def _():
        xm = xm_ref[...].astype(F32)
        valid = lax.broadcasted_iota(jnp.int32, (META_PAD, 1), 0) >= (META_PAD - N_META)
        _, k, logf = _hg_gates(xm, lb, valid)
        bc = _cumsum_rows(tri[0:META_PAD, 0:META_PAD], logf)
        ki = (k * jnp.exp(bc[META_PAD - 1:META_PAD, :] - bc)).astype(BF16)
        v = xm[:, 2 * w:3 * w].astype(BF16)
        for h in range(HG_HEADS):
            sl = slice(h * dh, (h + 1) * dh)
            s0 = _dot_tn(v[:, sl], ki[:, sl])
            for b in range(nb):
                st_ref[b, h] = s0

    ri = lax.broadcasted_iota(jnp.int32, (tb, tb), 0)
    ci = lax.broadcasted_iota(jnp.int32, (tb, tb), 1)
    diag_shift = HG_DIAG.bit_length() - 1
    diag_mask = ((ri >> diag_shift) == (ci >> diag_shift)) & (ci <= ri)
    levels = []
    seg = 2 * HG_DIAG
    while seg <= tb:
        levels.append(seg)
        seg *= 2
    seg_masks = [None if s == tb else
                 ((ri >> (s.bit_length() - 1)) == (ci >> (s.bit_length() - 1))).astype(F32)
                 for s in levels]
    row = lax.broadcasted_iota(jnp.int32, (tb, 1), 0)

    for b in range(nb):
        x = x_ref[b].astype(F32)
        qs, k, logf = _hg_gates(x, lb)
        v = x[:, 2 * w:3 * w].astype(BF16)
        g = x[:, 3 * w:4 * w]
        bc = _cumsum_rows(tri, logf)
        dlt = bc - _segment_rows(bc, HG_DIAG, HG_DIAG // 2 - 1)
        qk = [((qs * jnp.exp(dlt)).astype(BF16), (k * jnp.exp(-dlt)).astype(BF16))]
        for s in levels:
            e = jnp.exp(-jnp.abs(bc - _segment_rows(bc, s, s // 2 - 1)))
            upper = (row & (s - 1)) >= (s // 2)
            qk.append((jnp.where(upper, qs * e, 0.0).astype(BF16),
                       jnp.where(upper, 0.0, k * e).astype(BF16)))
        b_last = bc[tb - 1:tb, :]
        qi = (qs * jnp.exp(bc)).astype(BF16)
        ki = (k * jnp.exp(b_last - bc)).astype(BF16)
        dec = jnp.exp(b_last)
        outs = []
        for h in range(HG_HEADS):
            sl = slice(h * dh, (h + 1) * dh)
            sc = jnp.where(diag_mask, _dot_nt(qk[0][0][:, sl], qk[0][1][:, sl]), 0.0)
            for (ql, kl), m in zip(qk[1:], seg_masks):
                t = _dot_nt(ql[:, sl], kl[:, sl])
                sc = sc + (t if m is None else t * m)
            st = st_ref[b, h]
            o = (jnp.dot(sc.astype(BF16), v[:, sl], preferred_element_type=F32)
                 + _dot_nt(qi[:, sl], st.astype(BF16)))
            st_ref[b, h] = dec[:, sl] * st + _dot_tn(v[:, sl], ki[:, sl])
            ms = jnp.mean(o * o, axis=-1, keepdims=True)
            outs.append(o * lax.rsqrt(ms + EPS))
        o = jnp.concatenate(outs, axis=1) * gain_ref[...]
        y_ref[b] = (o * (g * _sigmoid(g))).astype(BF16)


def _hg_mixer(x, x_meta, lb, gain, tri):
    nb, seq, _ = x.shape
    tb = TIME_BLOCK
    kern = functools.partial(_hg_kernel, nb=nb, tb=tb)
    const2 = lambda t: (0, 0)
    return pl.pallas_call(
        kern,
        grid=(seq // tb,),
        in_specs=[
            pl.BlockSpec((nb, tb, 4 * HG_WIDTH), lambda t: (0, t, 0)),
            pl.BlockSpec(x_meta.shape, const2),
            pl.BlockSpec(lb.shape, const2),
            pl.BlockSpec(gain.shape, const2),
            pl.BlockSpec(tri.shape, const2),
        ],
        out_specs=pl.BlockSpec((nb, tb, HG_WIDTH), lambda t: (0, t, 0)),
        out_shape=jax.ShapeDtypeStruct((nb, seq, HG_WIDTH), BF16),
        scratch_shapes=[pltpu.VMEM((nb, HG_HEADS, HG_HEAD_DIM, HG_HEAD_DIM), F32)],
        compiler_params=_cparams(("arbitrary",)),
        name="hgrn2_mixer",
    )(x, x_meta, lb, gain, tri)


def _split3(x):
    hi = x.astype(BF16)
    lo = (x - hi.astype(F32)).astype(BF16)
    return hi, lo


def _pack_pairs(y):
    n = y.shape[1] // 2
    lo = pltpu.bitcast(y[:, :n].astype(BF16).astype(F32), jnp.uint32)
    hi = pltpu.bitcast(y[:, n:].astype(BF16).astype(F32), jnp.uint32)
    return (lo >> 16) | (hi & jnp.uint32(0xFFFF0000))


def _unpack_pairs(p):
    lo = pltpu.bitcast(p << 16, F32)
    hi = pltpu.bitcast(p & jnp.uint32(0xFFFF0000), F32)
    return lo, hi


def _merge_kernel(x_ref, ys_ref, yh_ref, gt_ref, wbs_ref, wbh_ref, wo_ref, g2_ref, wr_hi_ref,
                  wr_lo_ref, br_ref, tri_ref, h2_ref, xp_ref, eid_ref, wt_ref, rank_ref, cnt_ref,
                  run_ref):
    @pl.when(pl.program_id(0) == 0)
    def _():
        run_ref[...] = jnp.zeros_like(run_ref)

    gt = gt_ref[...].astype(F32)
    gs = _sigmoid(gt[:, :D_MODEL])
    gh = _sigmoid(gt[:, D_MODEL:])
    merged = (gs * jnp.dot(ys_ref[...], wbs_ref[...], preferred_element_type=F32)
              + gh * jnp.dot(yh_ref[...], wbh_ref[...], preferred_element_type=F32))
    h2 = x_ref[...] + jnp.dot(merged.astype(BF16), wo_ref[...], preferred_element_type=F32)
    h2_ref[...] = h2
    ms = jnp.mean(h2 * h2, axis=-1, keepdims=True)
    xn = h2 * lax.rsqrt(ms + EPS) * g2_ref[...]
    xp_ref[...] = _pack_pairs(xn)

    x_hi, x_lo = _split3(xn)
    logits = (jnp.dot(x_hi, wr_hi_ref[...], preferred_element_type=F32)
              + jnp.dot(x_hi, wr_lo_ref[...], preferred_element_type=F32)
              + jnp.dot(x_lo, wr_hi_ref[...], preferred_element_type=F32)) + br_ref[...]
    rows = logits.shape[0]
    lane = lax.broadcasted_iota(jnp.int32, (rows, LANES), 1)
    tops, hots = [], []
    sel = jnp.zeros((rows, LANES), F32)
    for k in range(TOP_K):
        m = jnp.max(logits, axis=-1, keepdims=True)
        idx = jnp.min(jnp.where(logits == m, lane, LANES), axis=-1, keepdims=True)
        hot = lane == idx
        logits = jnp.where(hot, NEG_BIG, logits)
        tops.append(m)
        hots.append(hot)
        sel = sel + hot.astype(F32)
        eid_ref[:, k:k + 1] = idx
    es = [jnp.exp(m - tops[0]) for m in tops]
    tot = es[0] + es[1] + es[2] + es[3]
    for k in range(TOP_K):
        wt_ref[:, k:k + 1] = es[k] / tot
    prefix = jnp.dot(tri_ref[...], sel.astype(BF16), preferred_element_type=F32) + run_ref[...]
    for k in range(TOP_K):
        r = jnp.sum(jnp.where(hots[k], prefix, 0.0), axis=-1, keepdims=True)
        rank_ref[:, k:k + 1] = r.astype(jnp.int32)
    run = run_ref[...] + jnp.sum(sel, axis=0, keepdims=True)
    run_ref[...] = run
    cnt_ref[...] = run


def _merge_router(x_rows, y_s5, y_hg, gates, wbs, wbh, wo, g2, wr_hi, wr_lo, br, tri_strict):
    n = x_rows.shape[0]
    rb = ROW_BLOCK
    const2 = lambda i: (0, 0)
    rowblk = lambda width: pl.BlockSpec((rb, width), lambda i: (i, 0))
    return pl.pallas_call(
        _merge_kernel,
        grid=(n // rb,),
        in_specs=[
            rowblk(D_MODEL), rowblk(S5_WIDTH), rowblk(HG_WIDTH), rowblk(2 * D_MODEL),
            pl.BlockSpec(wbs.shape, const2), pl.BlockSpec(wbh.shape, const2),
            pl.BlockSpec(wo.shape, const2), pl.BlockSpec(g2.shape, const2),
            pl.BlockSpec(wr_hi.shape, const2), pl.BlockSpec(wr_lo.shape, const2),
            pl.BlockSpec(br.shape, const2), pl.BlockSpec(tri_strict.shape, const2),
        ],
        out_specs=[
            rowblk(D_MODEL), rowblk(D_MODEL // 2), rowblk(TOP_K), rowblk(TOP_K), rowblk(TOP_K),
            pl.BlockSpec((1, LANES), const2),
        ],
        out_shape=[
            jax.ShapeDtypeStruct((n, D_MODEL), F32),
            jax.ShapeDtypeStruct((n, D_MODEL // 2), jnp.uint32),
            jax.ShapeDtypeStruct((n, TOP_K), jnp.int32),
            jax.ShapeDtypeStruct((n, TOP_K), F32),
            jax.ShapeDtypeStruct((n, TOP_K), jnp.int32),
            jax.ShapeDtypeStruct((1, LANES), F32),
        ],
        scratch_shapes=[pltpu.VMEM((1, LANES), F32)],
        compiler_params=_cparams(("arbitrary",)),
        name="merge_router",
    )(x_rows, y_s5, y_hg, gates, wbs, wbh, wo, g2, wr_hi, wr_lo, br, tri_strict)


def _gather_rows(src, idx):
    n = idx.shape[0]
    width = src.shape[1]
    assert n % GATHER_ROWS == 0
    mesh = plsc.VectorSubcoreMesh(core_axis_name="core", subcore_axis_name="subcore")

    @functools.partial(pl.kernel, out_type=jax.ShapeDtypeStruct((n, width), src.dtype),
                       mesh=mesh, scratch_types=[], name="sc_gather_rows")
    def gather(src_hbm, idx_hbm, out_hbm):
        def body(idx_vmem, out_vmem):
            off = pl.multiple_of(pl.program_id(1) * GATHER_WINDOW, GATHER_WINDOW)
            pltpu.sync_copy(src_hbm.at[idx_vmem.at[0, pl.ds(off, GATHER_WINDOW)]], out_vmem)

        pltpu.emit_pipeline(
            body,
            grid=(n // INDEX_BLOCK, INDEX_SPLIT),
            in_specs=[pl.BlockSpec((1, INDEX_BLOCK), lambda i, j: (0, i))],
            out_specs=[pl.BlockSpec((GATHER_WINDOW, width), lambda i, j: (INDEX_SPLIT * i + j, 0))],
            core_axis_name=("core", "subcore"),
            dimension_semantics=(pltpu.PARALLEL, pltpu.ARBITRARY),
        )(idx_hbm, out_hbm)

    return gather(src, idx.reshape(1, n))


def _scatter_rows(src, dest, n_out):
    n_src, width = src.shape
    n = dest.shape[0]
    assert n == TOP_K * n_src and n % GATHER_ROWS == 0 and n_src % GATHER_WINDOW == 0
    steps_per_copy = n_src // GATHER_WINDOW
    mesh = plsc.VectorSubcoreMesh(core_axis_name="core", subcore_axis_name="subcore")

    @functools.partial(pl.kernel, out_type=jax.ShapeDtypeStruct((n_out, width), src.dtype),
                       mesh=mesh, scratch_types=[], name="sc_scatter_rows")
    def scatter(src_hbm, idx_hbm, out_hbm):
        def body(src_vmem, idx_vmem):
            off = pl.multiple_of(pl.program_id(1) * GATHER_WINDOW, GATHER_WINDOW)
            pltpu.sync_copy(src_vmem, out_hbm.at[idx_vmem.at[0, pl.ds(off, GATHER_WINDOW)]])

        pltpu.emit_pipeline(
            body,
            grid=(n // INDEX_BLOCK, INDEX_SPLIT),
            in_specs=[pl.BlockSpec((GATHER_WINDOW, width),
                                   lambda i, j: ((INDEX_SPLIT * i + j) % steps_per_copy, 0)),
                      pl.BlockSpec((1, INDEX_BLOCK), lambda i, j: (0, i))],
            out_specs=[],
            core_axis_name=("core", "subcore"),
            dimension_semantics=(pltpu.PARALLEL, pltpu.ARBITRARY),
        )(src_hbm, idx_hbm)

    return scatter(src, dest.reshape(1, n))


GU_GROUP = 512


def _expert_kernel(be_ref, nv_ref, nr_ref, x_ref, wgu_ref, bgu_ref, wd_ref, bd_ref, perm_ref,
                   y_ref, wgu_s, wd_s):
    i = pl.program_id(0)
    prev = be_ref[jnp.maximum(i - 1, 0)]
    fresh = (i == 0) | (be_ref[i] != prev)

    @pl.when(fresh & (i < nv_ref[0]))
    def _():
        for c in range(2 * D_EXPERT // GU_GROUP):
            cols = slice(c * GU_GROUP, (c + 1) * GU_GROUP)
            w = wgu_ref[0, :, cols].astype(BF16)
            wgu_s[:, cols] = jnp.dot(w, perm_ref[...], preferred_element_type=F32).astype(BF16)
        wd_s[...] = wd_ref[0].astype(BF16)

    @pl.when(i < nv_ref[0])
    def _():
        half = D_MODEL // 2
        live = lax.broadcasted_iota(jnp.int32, (MOE_BLOCK, 1), 0) < nr_ref[i]
        xa, xb = _unpack_pairs(jnp.where(live, x_ref[...], jnp.uint32(0)))
        xa = xa.astype(BF16)
        xb = xb.astype(BF16)
        acc = jnp.zeros((MOE_BLOCK, D_MODEL), F32) + bd_ref[0]
        hw = GU_GROUP // 2
        for c in range(2 * D_EXPERT // GU_GROUP):
            cols = slice(c * GU_GROUP, (c + 1) * GU_GROUP)
            gu = (jnp.dot(xa, wgu_s[0:half, cols], preferred_element_type=F32)
                  + jnp.dot(xb, wgu_s[half:, cols], preferred_element_type=F32)
                  + bgu_ref[0, :, cols])
            gate = jnp.minimum(gu[:, :hw], SWIGLU_LIMIT)
            up = jnp.clip(gu[:, hw:], -SWIGLU_LIMIT, SWIGLU_LIMIT)
            hcol = ((up + 1.0) * (gate * _sigmoid(gate * SWIGLU_ALPHA))).astype(BF16)
            acc = acc + jnp.dot(hcol, wd_s[c * hw:(c + 1) * hw, :], preferred_element_type=F32)
        y_ref[...] = _pack_pairs(acc)

    @pl.when(i >= nv_ref[0])
    def _():
        y_ref[...] = jnp.zeros_like(y_ref)


def _experts(block_e, n_valid, block_rows, x_rows, w_gate_up, b_gu_grouped, w_down, b_down, perm):
    n_rows = x_rows.shape[0]
    n_blocks = n_rows // MOE_BLOCK
    half = D_MODEL // 2
    by_expert = lambda i, be, nv, nr: (be[i], 0, 0)
    grid_spec = pltpu.PrefetchScalarGridSpec(
        num_scalar_prefetch=3,
        grid=(n_blocks,),
        in_specs=[
            pl.BlockSpec((MOE_BLOCK, half), lambda i, be, nv, nr: (i, 0)),
            pl.BlockSpec((1, D_MODEL, 2 * D_EXPERT), by_expert),
            pl.BlockSpec((1, 1, 2 * D_EXPERT), by_expert),
            pl.BlockSpec((1, D_EXPERT, D_MODEL), by_expert),
            pl.BlockSpec((1, 1, D_MODEL), by_expert),
            pl.BlockSpec(perm.shape, lambda i, be, nv, nr: (0, 0)),
        ],
        out_specs=pl.BlockSpec((MOE_BLOCK, half), lambda i, be, nv, nr: (i, 0)),
        scratch_shapes=[
            pltpu.VMEM((D_MODEL, 2 * D_EXPERT), BF16),
            pltpu.VMEM((D_EXPERT, D_MODEL), BF16),
        ],
    )
    return pl.pallas_call(
        _expert_kernel,
        grid_spec=grid_spec,
        out_shape=jax.ShapeDtypeStruct((n_rows, half), jnp.uint32),
        compiler_params=_cparams(("arbitrary",)),
        name="experts",
    )(block_e, n_valid, block_rows, x_rows, w_gate_up, b_gu_grouped, w_down, b_down, perm)


def _combine_kernel(h2_ref, y0_ref, y1_ref, y2_ref, y3_ref, wt_ref, gf_ref, out_ref):
    half = D_MODEL // 2
    wt = wt_ref[...]
    lo = jnp.zeros((h2_ref.shape[0], half), F32)
    hi = jnp.zeros((h2_ref.shape[0], half), F32)
    for k, yk_ref in enumerate((y0_ref, y1_ref, y2_ref, y3_ref)):
        a, b = _unpack_pairs(yk_ref[...])
        lo = lo + wt[:, k:k + 1] * a
        hi = hi + wt[:, k:k + 1] * b
    y = h2_ref[...] + jnp.concatenate([lo, hi], axis=1)
    ms = jnp.mean(y * y, axis=-1, keepdims=True)
    out_ref[...] = y * lax.rsqrt(ms + EPS) * gf_ref[...]


def _combine(h2, y_tok, wts, gain):
    n = h2.shape[0]
    rb = ROW_BLOCK
    rowblk = lambda width: pl.BlockSpec((rb, width), lambda i: (i, 0))
    steps = n // rb
    choice = lambda k: pl.BlockSpec((rb, D_MODEL // 2), lambda i: (k * steps + i, 0))
    return pl.pallas_call(
        _combine_kernel,
        grid=(steps,),
        in_specs=[rowblk(D_MODEL)] + [choice(k) for k in range(TOP_K)]
        + [rowblk(TOP_K), pl.BlockSpec((1, D_MODEL), lambda i: (0, 0))],
        out_specs=rowblk(D_MODEL),
        out_shape=jax.ShapeDtypeStruct((n, D_MODEL), F32),
        compiler_params=_cparams(("parallel",)),
        name="combine_norm",
    )(h2, y_tok, y_tok, y_tok, y_tok, wts, gain)


def _lower_tri(n, strict):
    r = lax.broadcasted_iota(jnp.int32, (n, n), 0)
    c = lax.broadcasted_iota(jnp.int32, (n, n), 1)
    return ((c < r) if strict else (c <= r)).astype(BF16)


def kernel(x, meta_tokens, norm1_gain, w_in, s5_lambda_re, s5_lambda_im, s5_log_dt, s5_b_re,
           s5_b_im, s5_c_re, s5_c_im, s5_d, s5_w_glu, hgrn_lb_logits, hgrn_norm_gain,
           w_branch_s5, w_branch_hgrn, w_out, norm2_gain, w_router, b_router, w_gate_up,
           b_gate_up, w_down, b_down, final_norm_gain):
    nb, seq, d = x.shape
    n_tok = nb * seq
    assert d == D_MODEL and seq % TIME_BLOCK == 0 and n_tok % ROW_BLOCK == 0
    x_rows = x.reshape(n_tok, d)
    w_in_b = w_in[0].astype(BF16)
    g1 = norm1_gain[0].reshape(1, d).astype(F32)

    u, qfvg, gates = _inproj(x_rows, g1, w_in_b, ROW_BLOCK)
    meta_rows = jnp.concatenate(
        [jnp.zeros((META_PAD - N_META, d), F32), meta_tokens.astype(F32)], axis=0)
    u_m, qfvg_m, _ = _inproj(meta_rows, g1, w_in_b, META_PAD)

    bm, cm, tab = _s5_params(s5_lambda_re[0], s5_lambda_im[0], s5_log_dt[0], s5_b_re[0],
                             s5_b_im[0], s5_c_re[0], s5_c_im[0])
    y_s5 = _s5_mixer(u.reshape(nb, seq, S5_WIDTH), u_m, bm, cm, tab,
                     s5_d[0].reshape(1, S5_WIDTH).astype(F32), s5_w_glu[0].astype(BF16))

    lower_bounds = jnp.cumsum(jax.nn.softmax(hgrn_lb_logits.astype(F32), axis=0), axis=0)
    lb = lower_bounds[0].reshape(1, HG_WIDTH)
    y_hg = _hg_mixer(qfvg.reshape(nb, seq, 4 * HG_WIDTH), qfvg_m, lb,
                     hgrn_norm_gain[0].reshape(1, HG_WIDTH).astype(F32),
                     _lower_tri(TIME_BLOCK, strict=False))

    wr = jnp.zeros((d, LANES), F32).at[:, :N_EXPERTS].set(w_router[0].astype(F32))
    wr_hi = wr.astype(BF16)
    wr_lo = (wr - wr_hi.astype(F32)).astype(BF16)
    br = jnp.full((1, LANES), NEG_BIG, F32).at[0, :N_EXPERTS].set(b_router[0].astype(F32))
    h2, xn_packed, eid, wts, rank, counts = _merge_router(
        x_rows, y_s5.reshape(n_tok, S5_WIDTH), y_hg.reshape(n_tok, HG_WIDTH), gates,
        w_branch_s5[0].astype(BF16), w_branch_hgrn[0].astype(BF16), w_out[0].astype(BF16),
        norm2_gain[0].reshape(1, d).astype(F32), wr_hi, wr_lo, br,
        _lower_tri(ROW_BLOCK, strict=True))

    n_assign = n_tok * TOP_K
    n_blocks = n_assign // MOE_BLOCK + N_EXPERTS
    n_rows = n_blocks * MOE_BLOCK
    n_rows_pad = -(-n_rows // GATHER_ROWS) * GATHER_ROWS
    cnt = counts[0, :N_EXPERTS].astype(jnp.int32)
    padded = (cnt + MOE_BLOCK - 1) // MOE_BLOCK * MOE_BLOCK
    padded_end = jnp.cumsum(padded)
    padded_start = padded_end - padded
    onehot = (eid[:, :, None] == jnp.arange(N_EXPERTS, dtype=jnp.int32)).astype(jnp.int32)
    dest = (jnp.sum(onehot * padded_start, axis=-1) + rank).T.reshape(-1)
    block_start = jnp.arange(n_rows_pad // MOE_BLOCK, dtype=jnp.int32) * MOE_BLOCK
    block_e = jnp.minimum(jnp.sum((block_start[:, None] >= padded_end[None, :]).astype(jnp.int32),
                                  axis=1), N_EXPERTS - 1)
    block_rows = jnp.clip(cnt[block_e] - (block_start - padded_start[block_e]), 0, MOE_BLOCK)
    n_valid = (padded_end[-1] // MOE_BLOCK).astype(jnp.int32).reshape(1)

    x_sorted = _scatter_rows(xn_packed, dest, n_rows_pad)
    hw = GU_GROUP // 2
    pr = jnp.arange(GU_GROUP)
    src = jnp.where(pr < hw, 2 * pr, 2 * (pr - hw) + 1)
    perm = (jnp.arange(GU_GROUP)[:, None] == src[None, :]).astype(BF16)
    b_gu = b_gate_up[0].astype(F32).reshape(N_EXPERTS, 2 * D_EXPERT // GU_GROUP, hw, 2)
    b_gu = b_gu.transpose(0, 1, 3, 2).reshape(N_EXPERTS, 1, 2 * D_EXPERT)
    y_sorted = _experts(block_e, n_valid, block_rows, x_sorted, w_gate_up[0], b_gu, w_down[0],
                        b_down[0].astype(F32).reshape(N_EXPERTS, 1, d), perm)
    y_tok = _gather_rows(y_sorted, dest)

    out = _combine(h2, y_tok, wts, final_norm_gain.reshape(1, d).astype(F32))
    return out.reshape(nb, seq, d)
```

```python
import functools

import jax
import jax.numpy as jnp
from jax import lax
from jax.experimental import pallas as pl
from jax.experimental.pallas import tpu as pltpu
from jax.experimental.pallas import tpu_sc as plsc

F32 = jnp.float32
BF16 = jnp.bfloat16

D_MODEL = 1024
N_META = 16
S5_GROUP = 16
S5_GROUPS = 32
S5_WIDTH = 512
S5_STATE = 64
HG_HEADS = 4
HG_HEAD_DIM = 128
HG_WIDTH = 512
N_EXPERTS = 32
TOP_K = 4
D_EXPERT = 1024
SWIGLU_ALPHA = 1.702
SWIGLU_LIMIT = 7.0
EPS = 1e-6

LANES = 128
SUBLANES = 8
TIME_BLOCK = 256
META_PAD = TIME_BLOCK
ROW_BLOCK = 512
MOE_BLOCK = 512
GATHER_WINDOW = 64
INDEX_BLOCK = 128
INDEX_SPLIT = INDEX_BLOCK // GATHER_WINDOW
SC_SUBCORES = 32
GATHER_ROWS = INDEX_BLOCK * SC_SUBCORES
HG_DIAG = 32
NEG_BIG = -1e30
VMEM_LIMIT = 56 * 1024 * 1024


def _cparams(sem):
    return pltpu.CompilerParams(dimension_semantics=sem, vmem_limit_bytes=VMEM_LIMIT)


def _sigmoid(x):
    return 1.0 / (1.0 + jnp.exp(-x))


def _inproj_kernel(x_ref, g_ref, w_ref, u_ref, qfvg_ref, gates_ref):
    x = x_ref[...]
    ms = jnp.mean(x * x, axis=-1, keepdims=True)
    xn = (x * lax.rsqrt(ms + EPS) * g_ref[...]).astype(BF16)
    u_ref[...] = jnp.dot(xn, w_ref[:, 0:S5_WIDTH], preferred_element_type=F32).astype(BF16)
    a, b = S5_WIDTH, S5_WIDTH + 4 * HG_WIDTH
    qfvg_ref[...] = jnp.dot(xn, w_ref[:, a:b], preferred_element_type=F32).astype(BF16)
    gates_ref[...] = jnp.dot(xn, w_ref[:, b:], preferred_element_type=F32).astype(BF16)


def _inproj(rows, gain, w_in_bf16, row_block):
    n = rows.shape[0]
    n_gate = 2 * D_MODEL
    return pl.pallas_call(
        _inproj_kernel,
        grid=(n // row_block,),
        in_specs=[
            pl.BlockSpec((row_block, D_MODEL), lambda i: (i, 0)),
            pl.BlockSpec((1, D_MODEL), lambda i: (0, 0)),
            pl.BlockSpec(w_in_bf16.shape, lambda i: (0, 0)),
        ],
        out_specs=[
            pl.BlockSpec((row_block, S5_WIDTH), lambda i: (i, 0)),
            pl.BlockSpec((row_block, 4 * HG_WIDTH), lambda i: (i, 0)),
            pl.BlockSpec((row_block, n_gate), lambda i: (i, 0)),
        ],
        out_shape=[
            jax.ShapeDtypeStruct((n, S5_WIDTH), BF16),
            jax.ShapeDtypeStruct((n, 4 * HG_WIDTH), BF16),
            jax.ShapeDtypeStruct((n, n_gate), BF16),
        ],
        compiler_params=_cparams(("parallel",)),
        name="inproj",
    )(rows, gain, w_in_bf16)


N_STATE = S5_GROUPS * S5_STATE
HALF_STATE = N_STATE // 2
HALF_COLS = 2 * HALF_STATE


SEG_COUNT = SUBLANES
SEG_LEN = TIME_BLOCK // SEG_COUNT
SCAN_LANE_GROUPS = 2
SCAN_UNROLL = 2
FIXUP_UNROLL = 4


def _state_cols(j):
    half, m0 = divmod(j * LANES, HALF_STATE)
    cre = half * HALF_COLS + m0
    return cre, cre + HALF_STATE, j * LANES


def _s5_local_scan(xs_refs, tab_ref, car_ref, cs_ref):
    sub = lax.broadcasted_iota(jnp.int32, (SUBLANES, LANES), 0)
    zero = jnp.zeros((SUBLANES, LANES), F32)
    for j0 in range(0, N_STATE // LANES, SCAN_LANE_GROUPS):
        groups = [_state_cols(j) for j in range(j0, j0 + SCAN_LANE_GROUPS)]
        lams = [(tab_ref[8, :, n0:n0 + LANES], tab_ref[9, :, n0:n0 + LANES]) for _, _, n0 in groups]
        init = tuple((zero, zero) for _ in groups for _ in xs_refs)

        def body(trip, carry, groups=groups, lams=lams):
            coef = [lam for lam in lams for _ in xs_refs]
            rows = [pl.multiple_of((trip * SCAN_UNROLL + q) * SUBLANES, SUBLANES)
                    for q in range(SCAN_UNROLL)]
            vals = [[(ref[pl.ds(r0, SUBLANES), cre:cre + LANES],
                      ref[pl.ds(r0, SUBLANES), cim:cim + LANES])
                     for cre, cim, _ in groups for ref in xs_refs] for r0 in rows]
            steps = []
            for q in range(SCAN_UNROLL):
                carry = tuple((lr * xr - li * xi + br, lr * xi + li * xr + bi)
                              for (lr, li), (xr, xi), (br, bi) in zip(coef, carry, vals[q]))
                steps.append(carry)
            for r0, outs in zip(rows, steps):
                k = 0
                for cre, cim, _ in groups:
                    for ref in xs_refs:
                        ref[pl.ds(r0, SUBLANES), cre:cre + LANES] = outs[k][0]
                        ref[pl.ds(r0, SUBLANES), cim:cim + LANES] = outs[k][1]
                        k += 1
            return carry

        ends = lax.fori_loop(0, SEG_LEN // SCAN_UNROLL, body, init)
        k = 0
        for cre, cim, n0 in groups:
            m = [tab_ref[i, :, n0:n0 + LANES] for i in range(8)]
            for c in range(len(xs_refs)):
                er, ei = ends[k]
                k += 1
                gr = jnp.where(sub == 0, car_ref[c, :, cre:cre + LANES], pltpu.roll(er, 1, 0))
                gi = jnp.where(sub == 0, car_ref[c, :, cim:cim + LANES], pltpu.roll(ei, 1, 0))
                for q, shift in enumerate((1, 2, 4)):
                    ar, ai = m[2 * q], m[2 * q + 1]
                    rr = pltpu.roll(gr, shift, 0)
                    ri = pltpu.roll(gi, shift, 0)
                    gr, gi = gr + ar * rr - ai * ri, gi + ar * ri + ai * rr
                cs_ref[c, :, cre:cre + LANES] = gr
                cs_ref[c, :, cim:cim + LANES] = gi
                xr = m[6] * gr - m[7] * gi + er
                xi = m[6] * gi + m[7] * gr + ei
                car_ref[c, :, cre:cre + LANES] = jnp.broadcast_to(
                    xr[SUBLANES - 1:SUBLANES, :], (SUBLANES, LANES))
                car_ref[c, :, cim:cim + LANES] = jnp.broadcast_to(
                    xi[SUBLANES - 1:SUBLANES, :], (SUBLANES, LANES))


def _s5_add_start_states(xs_refs, ptab_ref, cs_ref):
    for j in range(N_STATE // LANES):
        cre, cim, n0 = _state_cols(j)
        starts = [(cs_ref[c, :, cre:cre + LANES], cs_ref[c, :, cim:cim + LANES])
                  for c in range(len(xs_refs))]

        def body(trip, carry, cre=cre, cim=cim, n0=n0, starts=starts):
            rows = [pl.multiple_of((trip * FIXUP_UNROLL + q) * SUBLANES, SUBLANES)
                    for q in range(FIXUP_UNROLL)]
            loaded = [(ptab_ref[0, pl.ds(r0, SUBLANES), n0:n0 + LANES],
                       ptab_ref[1, pl.ds(r0, SUBLANES), n0:n0 + LANES],
                       [(ref[pl.ds(r0, SUBLANES), cre:cre + LANES],
                         ref[pl.ds(r0, SUBLANES), cim:cim + LANES]) for ref in xs_refs])
                      for r0 in rows]
            results = [[(xr + pr * cr - pi * ci, xi + pr * ci + pi * cr)
                        for (xr, xi), (cr, ci) in zip(vals, starts)] for pr, pi, vals in loaded]
            for r0, outs in zip(rows, results):
                for ref, (xr, xi) in zip(xs_refs, outs):
                    ref[pl.ds(r0, SUBLANES), cre:cre + LANES] = xr
                    ref[pl.ds(r0, SUBLANES), cim:cim + LANES] = xi
            return carry

        lax.fori_loop(0, SEG_LEN // FIXUP_UNROLL, body, 0)


def _s5_kernel(u_ref, um_ref, perm_ref, unperm_ref, bm_ref, cm_ref, tab_ref, ptab_ref, d_ref,
               wglu_ref, y_ref, car_ref, cs_ref, up_ref, *xs_refs, nb):
    half_ch = S5_WIDTH // 2

    def project_in(u, c):
        up = jnp.dot(perm_ref[...], u, preferred_element_type=F32).astype(BF16)
        up_ref[c] = up
        for hf in range(2):
            xs_refs[c][:, hf * HALF_COLS:(hf + 1) * HALF_COLS] = jnp.dot(
                up[:, hf * half_ch:(hf + 1) * half_ch], bm_ref[hf], preferred_element_type=F32)

    @pl.when(pl.program_id(0) == 0)
    def _():
        car_ref[...] = jnp.zeros_like(car_ref)
        project_in(um_ref[...], 0)
        _s5_local_scan(xs_refs[:1], tab_ref, car_ref, cs_ref)
        for b in range(1, nb):
            car_ref[b] = car_ref[0]

    for b in range(nb):
        project_in(u_ref[b], b)
    _s5_local_scan(xs_refs, tab_ref, car_ref, cs_ref)
    _s5_add_start_states(xs_refs, ptab_ref, cs_ref)
    for b in range(nb):
        ys = [jnp.dot(xs_refs[b][:, hf * HALF_COLS:(hf + 1) * HALF_COLS].astype(BF16), cm_ref[hf],
                      preferred_element_type=F32) for hf in range(2)]
        y = jnp.concatenate(ys, axis=1) + d_ref[...] * up_ref[b].astype(F32)
        y = 0.5 * y * (1.0 + jnp.tanh(0.7978845608028654 * (y + 0.044715 * (y * y * y))))
        z = jnp.dot(y.astype(BF16), wglu_ref[...], preferred_element_type=F32)
        out = (y * _sigmoid(z)).astype(BF16)
        y_ref[b] = jnp.dot(unperm_ref[...], out, preferred_element_type=F32).astype(BF16)


def _s5_mixer(u, u_meta, bm, cm, tab, ptab, d_skip, w_glu_bf16):
    nb, seq, _ = u.shape
    tb = TIME_BLOCK
    r = jnp.arange(tb)
    perm = (((r % SEG_COUNT) * SEG_LEN + r // SEG_COUNT)[:, None] == r[None, :]).astype(BF16)
    kern = functools.partial(_s5_kernel, nb=nb)
    const2 = lambda t: (0, 0)
    const3 = lambda t: (0, 0, 0)
    return pl.pallas_call(
        kern,
        grid=(seq // tb,),
        in_specs=[
            pl.BlockSpec((nb, tb, S5_WIDTH), lambda t: (0, t, 0)),
            pl.BlockSpec(u_meta.shape, const2),
            pl.BlockSpec(perm.shape, const2),
            pl.BlockSpec(perm.shape, const2),
            pl.BlockSpec(bm.shape, const3),
            pl.BlockSpec(cm.shape, const3),
            pl.BlockSpec(tab.shape, const3),
            pl.BlockSpec(ptab.shape, const3),
            pl.BlockSpec(d_skip.shape, const2),
            pl.BlockSpec(w_glu_bf16.shape, const2),
        ],
        out_specs=pl.BlockSpec((nb, tb, S5_WIDTH), lambda t: (0, t, 0)),
        out_shape=jax.ShapeDtypeStruct((nb, seq, S5_WIDTH), BF16),
        scratch_shapes=[pltpu.VMEM((nb, SUBLANES, 2 * N_STATE), F32),
                        pltpu.VMEM((nb, SUBLANES, 2 * N_STATE), F32),
                        pltpu.VMEM((nb, tb, S5_WIDTH), BF16)]
        + [pltpu.VMEM((tb, 2 * N_STATE), F32) for _ in range(nb)],
        compiler_params=_cparams(("arbitrary",)),
        name="s5_mixer",
    )(u, u_meta, perm, perm.T, bm, cm, tab, ptab, d_skip, w_glu_bf16)


def _s5_params(lam_re, lam_im, log_dt, b_re, b_im, c_re, c_im):
    lam = lax.complex(lam_re.astype(F32), lam_im.astype(F32))
    dt = jnp.exp(log_dt.astype(F32))[:, None]
    lam_dt = lam * dt
    lam_bar = jnp.exp(lam_dt)
    b_bar = ((lam_bar - 1.0) / lam)[:, :, None] * lax.complex(b_re.astype(F32), b_im.astype(F32))
    gl = S5_GROUPS // 2
    eye = jnp.eye(gl, dtype=F32)

    def in_half(bh):
        def blk(part):
            t = jnp.einsum('gph,gk->ghkp', part, eye)
            return t.reshape(gl * S5_GROUP, gl * S5_STATE)
        return jnp.concatenate([blk(jnp.real(bh)), blk(jnp.imag(bh))], axis=1)

    def out_half(cr, ci):
        def blk(part):
            t = jnp.einsum('ghp,gk->gpkh', part, eye)
            return t.reshape(gl * S5_STATE, gl * S5_GROUP)
        return jnp.concatenate([blk(cr), blk(-ci)], axis=0)

    bm = jnp.stack([in_half(b_bar[:gl]), in_half(b_bar[gl:])]).astype(BF16)
    cm = jnp.stack([out_half(c_re[:gl].astype(F32), c_im[:gl].astype(F32)),
                    out_half(c_re[gl:].astype(F32), c_im[gl:].astype(F32))]).astype(BF16)
    lam_flat = lam_dt.reshape(1, N_STATE)
    rows = jnp.arange(SUBLANES, dtype=F32)[:, None]
    tabs = []
    for shift in (1, 2, 4):
        p = jnp.exp(lam_flat * float(SEG_LEN * shift)) * (rows >= shift).astype(F32)
        tabs += [jnp.real(p), jnp.imag(p)]
    for p in (jnp.exp(lam_flat * float(SEG_LEN)), jnp.exp(lam_flat)):
        tabs += [jnp.real(p), jnp.imag(p)]
    tab = jnp.stack([jnp.broadcast_to(t, (SUBLANES, N_STATE)) for t in tabs]).astype(F32)
    steps = (jnp.arange(SEG_LEN * SUBLANES) // SUBLANES + 1).astype(F32)[:, None]
    p = jnp.exp(lam_flat * steps)
    ptab = jnp.stack([jnp.real(p), jnp.imag(p)]).astype(F32)
    return bm, cm, tab, ptab


def _dot_nt(a, b):
    return lax.dot_general(a, b, (((1,), (1,)), ((), ())), preferred_element_type=F32)


def _dot_tn(a, b):
    return lax.dot_general(a, b, (((0,), (0,)), ((), ())), preferred_element_type=F32)


def _cumsum_rows(tri, x):
    hi = x.astype(BF16)
    r1 = x - hi.astype(F32)
    mid = r1.astype(BF16)
    lo = (r1 - mid.astype(F32)).astype(BF16)
    return (jnp.dot(tri, hi, preferred_element_type=F32)
            + jnp.dot(tri, mid, preferred_element_type=F32)
            + jnp.dot(tri, lo, preferred_element_type=F32))


def _segment_rows(x, seg, pos):
    n = x.shape[0]
    parts = [jnp.broadcast_to(x[s * seg + pos:s * seg + pos + 1, :], (seg, x.shape[1]))
             for s in range(n // seg)]
    return parts[0] if len(parts) == 1 else jnp.concatenate(parts, axis=0)


def _hg_gates(x, lb, row_valid=None):
    w = HG_WIDTH
    q = x[:, 0:w]
    f = lb + (1.0 - lb) * _sigmoid(x[:, w:2 * w])
    logf = jnp.log(f)
    k = 1.0 - f
    if row_valid is not None:
        logf = jnp.where(row_valid, logf, 0.0)
        k = jnp.where(row_valid, k, 0.0)
    return q * _sigmoid(q), k, logf


def _hg_kernel(x_ref, xm_ref, lb_ref, gain_ref, tri_ref, y_ref, st_ref, *, nb, tb):
    w, dh = HG_WIDTH, HG_HEAD_DIM
    lb = lb_ref[...]
    tri = tri_ref[...]

    @pl.when(pl.program_id(0) == 0)
    def _():
        xm = xm_ref[...].astype(F32)
        valid = lax.broadcasted_iota(jnp.int32, (META_PAD, 1), 0) >= (META_PAD - N_META)
        _, k, logf = _hg_gates(xm, lb, valid)
        bc = _cumsum_rows(tri[0:META_PAD, 0:META_PAD], logf)
        ki = (k * jnp.exp(bc[META_PAD - 1:META_PAD, :] - bc)).astype(BF16)
        v = xm[:, 2 * w:3 * w].astype(BF16)
        for h in range(HG_HEADS):
            sl = slice(h * dh, (h + 1) * dh)
            s0 = _dot_tn(v[:, sl], ki[:, sl])
            for b in range(nb):
                st_ref[b, h] = s0

    ri = lax.broadcasted_iota(jnp.int32, (tb, tb), 0)
    ci = lax.broadcasted_iota(jnp.int32, (tb, tb), 1)
    diag_shift = HG_DIAG.bit_length() - 1
    diag_mask = ((ri >> diag_shift) == (ci >> diag_shift)) & (ci <= ri)
    levels = []
    seg = 2 * HG_DIAG
    while seg <= tb:
        levels.append(seg)
        seg *= 2
    seg_masks = [None if s == tb else
                 ((ri >> (s.bit_length() - 1)) == (ci >> (s.bit_length() - 1))).astype(F32)
                 for s in levels]
    row = lax.broadcasted_iota(jnp.int32, (tb, 1), 0)

    for b in range(nb):
        x = x_ref[b].astype(F32)
        qs, k, logf = _hg_gates(x, lb)
        v = x[:, 2 * w:3 * w].astype(BF16)
        g = x[:, 3 * w:4 * w]
        bc = _cumsum_rows(tri, logf)
        dlt = bc - _segment_rows(bc, HG_DIAG, HG_DIAG // 2 - 1)
        qk = [((qs * jnp.exp(dlt)).astype(BF16), (k * jnp.exp(-dlt)).astype(BF16))]
        for s in levels:
            e = jnp.exp(-jnp.abs(bc - _segment_rows(bc, s, s // 2 - 1)))
            upper = (row & (s - 1)) >= (s // 2)
            qk.append((jnp.where(upper, qs * e, 0.0).astype(BF16),
                       jnp.where(upper, 0.0, k * e).astype(BF16)))
        b_last = bc[tb - 1:tb, :]
        qi = (qs * jnp.exp(bc)).astype(BF16)
        ki = (k * jnp.exp(b_last - bc)).astype(BF16)
        dec = jnp.exp(b_last)
        outs = []
        for h in range(HG_HEADS):
            sl = slice(h * dh, (h + 1) * dh)
            sc = jnp.where(diag_mask, _dot_nt(qk[0][0][:, sl], qk[0][1][:, sl]), 0.0)
            for (ql, kl), m in zip(qk[1:], seg_masks):
                t = _dot_nt(ql[:, sl], kl[:, sl])
                sc = sc + (t if m is None else t * m)
            st = st_ref[b, h]
            o = (jnp.dot(sc.astype(BF16), v[:, sl], preferred_element_type=F32)
                 + _dot_nt(qi[:, sl], st.astype(BF16)))
            st_ref[b, h] = dec[:, sl] * st + _dot_tn(v[:, sl], ki[:, sl])
            ms = jnp.mean(o * o, axis=-1, keepdims=True)
            outs.append(o * lax.rsqrt(ms + EPS))
        o = jnp.concatenate(outs, axis=1) * gain_ref[...]
        y_ref[b] = (o * (g * _sigmoid(g))).astype(BF16)


def _hg_mixer(x, x_meta, lb, gain, tri):
    nb, seq, _ = x.shape
    tb = TIME_BLOCK
    kern = functools.partial(_hg_kernel, nb=nb, tb=tb)
    const2 = lambda t: (0, 0)
    return pl.pallas_call(
        kern,
        grid=(seq // tb,),
        in_specs=[
            pl.BlockSpec((nb, tb, 4 * HG_WIDTH), lambda t: (0, t, 0)),
            pl.BlockSpec(x_meta.shape, const2),
            pl.BlockSpec(lb.shape, const2),
            pl.BlockSpec(gain.shape, const2),
            pl.BlockSpec(tri.shape, const2),
        ],
        out_specs=pl.BlockSpec((nb, tb, HG_WIDTH), lambda t: (0, t, 0)),
        out_shape=jax.ShapeDtypeStruct((nb, seq, HG_WIDTH), BF16),
        scratch_shapes=[pltpu.VMEM((nb, HG_HEADS, HG_HEAD_DIM, HG_HEAD_DIM), F32)],
        compiler_params=_cparams(("arbitrary",)),
        name="hgrn2_mixer",
    )(x, x_meta, lb, gain, tri)


def _split3(x):
    hi = x.astype(BF16)
    lo = (x - hi.astype(F32)).astype(BF16)
    return hi, lo


def _pack_pairs(y):
    n = y.shape[1] // 2
    lo = pltpu.bitcast(y[:, :n].astype(BF16).astype(F32), jnp.uint32)
    hi = pltpu.bitcast(y[:, n:].astype(BF16).astype(F32), jnp.uint32)
    return (lo >> 16) | (hi & jnp.uint32(0xFFFF0000))


def _unpack_pairs(p):
    lo = pltpu.bitcast(p << 16, F32)
    hi = pltpu.bitcast(p & jnp.uint32(0xFFFF0000), F32)
    return lo, hi


def _merge_kernel(x_ref, ys_ref, yh_ref, gt_ref, wbs_ref, wbh_ref, wo_ref, g2_ref, wr_hi_ref,
                  wr_lo_ref, br_ref, tri_ref, h2_ref, xp_ref, eid_ref, wt_ref, rank_ref, cnt_ref,
                  run_ref):
    @pl.when(pl.program_id(0) == 0)
    def _():
        run_ref[...] = jnp.zeros_like(run_ref)

    gt = gt_ref[...].astype(F32)
    gs = _sigmoid(gt[:, :D_MODEL])
    gh = _sigmoid(gt[:, D_MODEL:])
    merged = (gs * jnp.dot(ys_ref[...], wbs_ref[...], preferred_element_type=F32)
              + gh * jnp.dot(yh_ref[...], wbh_ref[...], preferred_element_type=F32))
    h2 = x_ref[...] + jnp.dot(merged.astype(BF16), wo_ref[...], preferred_element_type=F32)
    h2_ref[...] = h2
    ms = jnp.mean(h2 * h2, axis=-1, keepdims=True)
    xn = h2 * lax.rsqrt(ms + EPS) * g2_ref[...]
    xp_ref[...] = _pack_pairs(xn)

    x_hi, x_lo = _split3(xn)
    logits = (jnp.dot(x_hi, wr_hi_ref[...], preferred_element_type=F32)
              + jnp.dot(x_hi, wr_lo_ref[...], preferred_element_type=F32)
              + jnp.dot(x_lo, wr_hi_ref[...], preferred_element_type=F32)) + br_ref[...]
    rows = logits.shape[0]
    lane = lax.broadcasted_iota(jnp.int32, (rows, LANES), 1)
    tops, hots = [], []
    sel = jnp.zeros((rows, LANES), F32)
    for k in range(TOP_K):
        m = jnp.max(logits, axis=-1, keepdims=True)
        idx = jnp.min(jnp.where(logits == m, lane, LANES), axis=-1, keepdims=True)
        hot = lane == idx
        logits = jnp.where(hot, NEG_BIG, logits)
        tops.append(m)
        hots.append(hot)
        sel = sel + hot.astype(F32)
        eid_ref[:, k:k + 1] = idx
    es = [jnp.exp(m - tops[0]) for m in tops]
    tot = es[0] + es[1] + es[2] + es[3]
    for k in range(TOP_K):
        wt_ref[:, k:k + 1] = es[k] / tot
    prefix = jnp.dot(tri_ref[...], sel.astype(BF16), preferred_element_type=F32) + run_ref[...]
    for k in range(TOP_K):
        r = jnp.sum(jnp.where(hots[k], prefix, 0.0), axis=-1, keepdims=True)
        rank_ref[:, k:k + 1] = r.astype(jnp.int32)
    run = run_ref[...] + jnp.sum(sel, axis=0, keepdims=True)
    run_ref[...] = run
    cnt_ref[...] = run


def _merge_router(x_rows, y_s5, y_hg, gates, wbs, wbh, wo, g2, wr_hi, wr_lo, br, tri_strict):
    n = x_rows.shape[0]
    rb = ROW_BLOCK
    const2 = lambda i: (0, 0)
    rowblk = lambda width: pl.BlockSpec((rb, width), lambda i: (i, 0))
    return pl.pallas_call(
        _merge_kernel,
        grid=(n // rb,),
        in_specs=[
            rowblk(D_MODEL), rowblk(S5_WIDTH), rowblk(HG_WIDTH), rowblk(2 * D_MODEL),
            pl.BlockSpec(wbs.shape, const2), pl.BlockSpec(wbh.shape, const2),
            pl.BlockSpec(wo.shape, const2), pl.BlockSpec(g2.shape, const2),
            pl.BlockSpec(wr_hi.shape, const2), pl.BlockSpec(wr_lo.shape, const2),
            pl.BlockSpec(br.shape, const2), pl.BlockSpec(tri_strict.shape, const2),
        ],
        out_specs=[
            rowblk(D_MODEL), rowblk(D_MODEL // 2), rowblk(TOP_K), rowblk(TOP_K), rowblk(TOP_K),
            pl.BlockSpec((1, LANES), const2),
        ],
        out_shape=[
            jax.ShapeDtypeStruct((n, D_MODEL), F32),
            jax.ShapeDtypeStruct((n, D_MODEL // 2), jnp.uint32),
            jax.ShapeDtypeStruct((n, TOP_K), jnp.int32),
            jax.ShapeDtypeStruct((n, TOP_K), F32),
            jax.ShapeDtypeStruct((n, TOP_K), jnp.int32),
            jax.ShapeDtypeStruct((1, LANES), F32),
        ],
        scratch_shapes=[pltpu.VMEM((1, LANES), F32)],
        compiler_params=_cparams(("arbitrary",)),
        name="merge_router",
    )(x_rows, y_s5, y_hg, gates, wbs, wbh, wo, g2, wr_hi, wr_lo, br, tri_strict)


def _gather_rows(src, idx):
    n = idx.shape[0]
    width = src.shape[1]
    assert n % GATHER_ROWS == 0
    mesh = plsc.VectorSubcoreMesh(core_axis_name="core", subcore_axis_name="subcore")

    @functools.partial(pl.kernel, out_type=jax.ShapeDtypeStruct((n, width), src.dtype),
                       mesh=mesh, scratch_types=[], name="sc_gather_rows")
    def gather(src_hbm, idx_hbm, out_hbm):
        def body(idx_vmem, out_vmem):
            off = pl.multiple_of(pl.program_id(1) * GATHER_WINDOW, GATHER_WINDOW)
            pltpu.sync_copy(src_hbm.at[idx_vmem.at[0, pl.ds(off, GATHER_WINDOW)]], out_vmem)

        pltpu.emit_pipeline(
            body,
            grid=(n // INDEX_BLOCK, INDEX_SPLIT),
            in_specs=[pl.BlockSpec((1, INDEX_BLOCK), lambda i, j: (0, i))],
            out_specs=[pl.BlockSpec((GATHER_WINDOW, width), lambda i, j: (INDEX_SPLIT * i + j, 0))],
            core_axis_name=("core", "subcore"),
            dimension_semantics=(pltpu.PARALLEL, pltpu.ARBITRARY),
        )(idx_hbm, out_hbm)

    return gather(src, idx.reshape(1, n))


def _scatter_rows(src, dest, n_out):
    n_src, width = src.shape
    n = dest.shape[0]
    assert n == TOP_K * n_src and n % GATHER_ROWS == 0 and n_src % GATHER_WINDOW == 0
    steps_per_copy = n_src // GATHER_WINDOW
    mesh = plsc.VectorSubcoreMesh(core_axis_name="core", subcore_axis_name="subcore")

    @functools.partial(pl.kernel, out_type=jax.ShapeDtypeStruct((n_out, width), src.dtype),
                       mesh=mesh, scratch_types=[], name="sc_scatter_rows")
    def scatter(src_hbm, idx_hbm, out_hbm):
        def body(src_vmem, idx_vmem):
            off = pl.multiple_of(pl.program_id(1) * GATHER_WINDOW, GATHER_WINDOW)
            pltpu.sync_copy(src_vmem, out_hbm.at[idx_vmem.at[0, pl.ds(off, GATHER_WINDOW)]])

        pltpu.emit_pipeline(
            body,
            grid=(n // INDEX_BLOCK, INDEX_SPLIT),
            in_specs=[pl.BlockSpec((GATHER_WINDOW, width),
                                   lambda i, j: ((INDEX_SPLIT * i + j) % steps_per_copy, 0)),
                      pl.BlockSpec((1, INDEX_BLOCK), lambda i, j: (0, i))],
            out_specs=[],
            core_axis_name=("core", "subcore"),
            dimension_semantics=(pltpu.PARALLEL, pltpu.ARBITRARY),
        )(src_hbm, idx_hbm)

    return scatter(src, dest.reshape(1, n))


GU_GROUP = 512


def _expert_kernel(be_ref, nv_ref, nr_ref, x_ref, wgu_ref, bgu_ref, wd_ref, bd_ref, perm_ref,
                   y_ref, wgu_s, wd_s):
    i = pl.program_id(0)
    prev = be_ref[jnp.maximum(i - 1, 0)]
    fresh = (i == 0) | (be_ref[i] != prev)

    @pl.when(fresh & (i < nv_ref[0]))
    def _():
        for c in range(2 * D_EXPERT // GU_GROUP):
            cols = slice(c * GU_GROUP, (c + 1) * GU_GROUP)
            w = wgu_ref[0, :, cols].astype(BF16)
            wgu_s[:, cols] = jnp.dot(w, perm_ref[...], preferred_element_type=F32).astype(BF16)
        wd_s[...] = wd_ref[0].astype(BF16)

    @pl.when(i < nv_ref[0])
    def _():
        half = D_MODEL // 2
        live = lax.broadcasted_iota(jnp.int32, (MOE_BLOCK, 1), 0) < nr_ref[i]
        xa, xb = _unpack_pairs(jnp.where(live, x_ref[...], jnp.uint32(0)))
        xa = xa.astype(BF16)
        xb = xb.astype(BF16)
        acc = jnp.zeros((MOE_BLOCK, D_MODEL), F32) + bd_ref[0]
        hw = GU_GROUP // 2
        for c in range(2 * D_EXPERT // GU_GROUP):
            cols = slice(c * GU_GROUP, (c + 1) * GU_GROUP)
            gu = (jnp.dot(xa, wgu_s[0:half, cols], preferred_element_type=F32)
                  + jnp.dot(xb, wgu_s[half:, cols], preferred_element_type=F32)
                  + bgu_ref[0, :, cols])
            gate = jnp.minimum(gu[:, :hw], SWIGLU_LIMIT)
            up = jnp.clip(gu[:, hw:], -SWIGLU_LIMIT, SWIGLU_LIMIT)
            hcol = ((up + 1.0) * (gate * _sigmoid(gate * SWIGLU_ALPHA))).astype(BF16)
            acc = acc + jnp.dot(hcol, wd_s[c * hw:(c + 1) * hw, :], preferred_element_type=F32)
        y_ref[...] = _pack_pairs(acc)

    @pl.when(i >= nv_ref[0])
    def _():
        y_ref[...] = jnp.zeros_like(y_ref)


def _experts(block_e, n_valid, block_rows, x_rows, w_gate_up, b_gu_grouped, w_down, b_down, perm):
    n_rows = x_rows.shape[0]
    n_blocks = n_rows // MOE_BLOCK
    half = D_MODEL // 2
    by_expert = lambda i, be, nv, nr: (be[i], 0, 0)
    grid_spec = pltpu.PrefetchScalarGridSpec(
        num_scalar_prefetch=3,
        grid=(n_blocks,),
        in_specs=[
            pl.BlockSpec((MOE_BLOCK, half), lambda i, be, nv, nr: (i, 0)),
            pl.BlockSpec((1, D_MODEL, 2 * D_EXPERT), by_expert),
            pl.BlockSpec((1, 1, 2 * D_EXPERT), by_expert),
            pl.BlockSpec((1, D_EXPERT, D_MODEL), by_expert),
            pl.BlockSpec((1, 1, D_MODEL), by_expert),
            pl.BlockSpec(perm.shape, lambda i, be, nv, nr: (0, 0)),
        ],
        out_specs=pl.BlockSpec((MOE_BLOCK, half), lambda i, be, nv, nr: (i, 0)),
        scratch_shapes=[
            pltpu.VMEM((D_MODEL, 2 * D_EXPERT), BF16),
            pltpu.VMEM((D_EXPERT, D_MODEL), BF16),
        ],
    )
    return pl.pallas_call(
        _expert_kernel,
        grid_spec=grid_spec,
        out_shape=jax.ShapeDtypeStruct((n_rows, half), jnp.uint32),
        compiler_params=_cparams(("arbitrary",)),
        name="experts",
    )(block_e, n_valid, block_rows, x_rows, w_gate_up, b_gu_grouped, w_down, b_down, perm)


def _combine_kernel(h2_ref, y0_ref, y1_ref, y2_ref, y3_ref, wt_ref, gf_ref, out_ref):
    half = D_MODEL // 2
    wt = wt_ref[...]
    lo = jnp.zeros((h2_ref.shape[0], half), F32)
    hi = jnp.zeros((h2_ref.shape[0], half), F32)
    for k, yk_ref in enumerate((y0_ref, y1_ref, y2_ref, y3_ref)):
        a, b = _unpack_pairs(yk_ref[...])
        lo = lo + wt[:, k:k + 1] * a
        hi = hi + wt[:, k:k + 1] * b
    y = h2_ref[...] + jnp.concatenate([lo, hi], axis=1)
    ms = jnp.mean(y * y, axis=-1, keepdims=True)
    out_ref[...] = y * lax.rsqrt(ms + EPS) * gf_ref[...]


def _combine(h2, y_tok, wts, gain):
    n = h2.shape[0]
    rb = ROW_BLOCK
    rowblk = lambda width: pl.BlockSpec((rb, width), lambda i: (i, 0))
    steps = n // rb
    choice = lambda k: pl.BlockSpec((rb, D_MODEL // 2), lambda i: (k * steps + i, 0))
    return pl.pallas_call(
        _combine_kernel,
        grid=(steps,),
        in_specs=[rowblk(D_MODEL)] + [choice(k) for k in range(TOP_K)]
        + [rowblk(TOP_K), pl.BlockSpec((1, D_MODEL), lambda i: (0, 0))],
        out_specs=rowblk(D_MODEL),
        out_shape=jax.ShapeDtypeStruct((n, D_MODEL), F32),
        compiler_params=_cparams(("parallel",)),
        name="combine_norm",
    )(h2, y_tok, y_tok, y_tok, y_tok, wts, gain)


def _lower_tri(n, strict):
    r = lax.broadcasted_iota(jnp.int32, (n, n), 0)
    c = lax.broadcasted_iota(jnp.int32, (n, n), 1)
    return ((c < r) if strict else (c <= r)).astype(BF16)


def kernel(x, meta_tokens, norm1_gain, w_in, s5_lambda_re, s5_lambda_im, s5_log_dt, s5_b_re,
           s5_b_im, s5_c_re, s5_c_im, s5_d, s5_w_glu, hgrn_lb_logits, hgrn_norm_gain,
           w_branch_s5, w_branch_hgrn, w_out, norm2_gain, w_router, b_router, w_gate_up,
           b_gate_up, w_down, b_down, final_norm_gain):
    nb, seq, d = x.shape
    n_tok = nb * seq
    assert d == D_MODEL and seq % TIME_BLOCK == 0 and n_tok % ROW_BLOCK == 0
    x_rows = x.reshape(n_tok, d)
    w_in_b = w_in[0].astype(BF16)
    g1 = norm1_gain[0].reshape(1, d).astype(F32)

    u, qfvg, gates = _inproj(x_rows, g1, w_in_b, ROW_BLOCK)
    meta_rows = jnp.concatenate(
        [jnp.zeros((META_PAD - N_META, d), F32), meta_tokens.astype(F32)], axis=0)
    u_m, qfvg_m, _ = _inproj(meta_rows, g1, w_in_b, META_PAD)

    bm, cm, tab, ptab = _s5_params(s5_lambda_re[0], s5_lambda_im[0], s5_log_dt[0], s5_b_re[0],
                                   s5_b_im[0], s5_c_re[0], s5_c_im[0])
    y_s5 = _s5_mixer(u.reshape(nb, seq, S5_WIDTH), u_m, bm, cm, tab, ptab,
                     s5_d[0].reshape(1, S5_WIDTH).astype(F32), s5_w_glu[0].astype(BF16))

    lower_bounds = jnp.cumsum(jax.nn.softmax(hgrn_lb_logits.astype(F32), axis=0), axis=0)
    lb = lower_bounds[0].reshape(1, HG_WIDTH)
    y_hg = _hg_mixer(qfvg.reshape(nb, seq, 4 * HG_WIDTH), qfvg_m, lb,
                     hgrn_norm_gain[0].reshape(1, HG_WIDTH).astype(F32),
                     _lower_tri(TIME_BLOCK, strict=False))

    wr = jnp.zeros((d, LANES), F32).at[:, :N_EXPERTS].set(w_router[0].astype(F32))
    wr_hi = wr.astype(BF16)
    wr_lo = (wr - wr_hi.astype(F32)).astype(BF16)
    br = jnp.full((1, LANES), NEG_BIG, F32).at[0, :N_EXPERTS].set(b_router[0].astype(F32))
    h2, xn_packed, eid, wts, rank, counts = _merge_router(
        x_rows, y_s5.reshape(n_tok, S5_WIDTH), y_hg.reshape(n_tok, HG_WIDTH), gates,
        w_branch_s5[0].astype(BF16), w_branch_hgrn[0].astype(BF16), w_out[0].astype(BF16),
        norm2_gain[0].reshape(1, d).astype(F32), wr_hi, wr_lo, br,
        _lower_tri(ROW_BLOCK, strict=True))

    n_assign = n_tok * TOP_K
    n_blocks = n_assign // MOE_BLOCK + N_EXPERTS
    n_rows = n_blocks * MOE_BLOCK
    n_rows_pad = -(-n_rows // GATHER_ROWS) * GATHER_ROWS
    cnt = counts[0, :N_EXPERTS].astype(jnp.int32)
    padded = (cnt + MOE_BLOCK - 1) // MOE_BLOCK * MOE_BLOCK
    padded_end = jnp.cumsum(padded)
    padded_start = padded_end - padded
    onehot = (eid[:, :, None] == jnp.arange(N_EXPERTS, dtype=jnp.int32)).astype(jnp.int32)
    dest = (jnp.sum(onehot * padded_start, axis=-1) + rank).T.reshape(-1)
    block_start = jnp.arange(n_rows_pad // MOE_BLOCK, dtype=jnp.int32) * MOE_BLOCK
    block_e = jnp.minimum(jnp.sum((block_start[:, None] >= padded_end[None, :]).astype(jnp.int32),
                                  axis=1), N_EXPERTS - 1)
    block_rows = jnp.clip(cnt[block_e] - (block_start - padded_start[block_e]), 0, MOE_BLOCK)
    n_valid = (padded_end[-1] // MOE_BLOCK).astype(jnp.int32).reshape(1)

    x_sorted = _scatter_rows(xn_packed, dest, n_rows_pad)
    hw = GU_GROUP // 2
    pr = jnp.arange(GU_GROUP)
    src = jnp.where(pr < hw, 2 * pr, 2 * (pr - hw) + 1)
    perm = (jnp.arange(GU_GROUP)[:, None] == src[None, :]).astype(BF16)
    b_gu = b_gate_up[0].astype(F32).reshape(N_EXPERTS, 2 * D_EXPERT // GU_GROUP, hw, 2)
    b_gu = b_gu.transpose(0, 1, 3, 2).reshape(N_EXPERTS, 1, 2 * D_EXPERT)
    y_sorted = _experts(block_e, n_valid, block_rows, x_sorted, w_gate_up[0], b_gu, w_down[0],
                        b_down[0].astype(F32).reshape(N_EXPERTS, 1, d), perm)
    y_tok = _gather_rows(y_sorted, dest)

    out = _combine(h2, y_tok, wts, final_norm_gain.reshape(1, d).astype(F32))
    return out.reshape(nb, seq, d)
```

```python
import functools

import jax
import jax.numpy as jnp
from jax import lax
from jax.experimental import pallas as pl
from jax.experimental.pallas import tpu as pltpu
from jax.experimental.pallas import tpu_sc as plsc

F32 = jnp.float32
BF16 = jnp.bfloat16

D_MODEL = 1024
N_META = 16
S5_GROUP = 16
S5_GROUPS = 32
S5_WIDTH = 512
S5_STATE = 64
HG_HEADS = 4
HG_HEAD_DIM = 128
HG_WIDTH = 512
N_EXPERTS = 32
TOP_K = 4
D_EXPERT = 1024
SWIGLU_ALPHA = 1.702
SWIGLU_LIMIT = 7.0
EPS = 1e-6

LANES = 128
SUBLANES = 8
TIME_BLOCK = 256
META_PAD = TIME_BLOCK
ROW_BLOCK = 512
MOE_BLOCK = 512
GATHER_WINDOW = 64
INDEX_BLOCK = 128
INDEX_SPLIT = INDEX_BLOCK // GATHER_WINDOW
SC_SUBCORES = 32
GATHER_ROWS = INDEX_BLOCK * SC_SUBCORES
HG_DIAG = 32
NEG_BIG = -1e30
VMEM_LIMIT = 56 * 1024 * 1024


def _cparams(sem):
    return pltpu.CompilerParams(dimension_semantics=sem, vmem_limit_bytes=VMEM_LIMIT)


def _sigmoid(x):
    return 0.5 * jnp.tanh(0.5 * x) + 0.5


def _silu(x):
    h = 0.5 * x
    return h + h * jnp.tanh(h)


def _inproj_kernel(x_ref, g_ref, w_ref, u_ref, qfvg_ref, gates_ref):
    x = x_ref[...]
    ms = jnp.mean(x * x, axis=-1, keepdims=True)
    xn = (x * lax.rsqrt(ms + EPS) * g_ref[...]).astype(BF16)
    u_ref[...] = jnp.dot(xn, w_ref[:, 0:S5_WIDTH], preferred_element_type=F32).astype(BF16)
    a, b = S5_WIDTH, S5_WIDTH + 4 * HG_WIDTH
    qfvg_ref[...] = jnp.dot(xn, w_ref[:, a:b], preferred_element_type=F32).astype(BF16)
    gates_ref[...] = jnp.dot(xn, w_ref[:, b:], preferred_element_type=F32).astype(BF16)


def _inproj(rows, gain, w_in_bf16, row_block):
    n = rows.shape[0]
    n_gate = 2 * D_MODEL
    return pl.pallas_call(
        _inproj_kernel,
        grid=(n // row_block,),
        in_specs=[
            pl.BlockSpec((row_block, D_MODEL), lambda i: (i, 0)),
            pl.BlockSpec((1, D_MODEL), lambda i: (0, 0)),
            pl.BlockSpec(w_in_bf16.shape, lambda i: (0, 0)),
        ],
        out_specs=[
            pl.BlockSpec((row_block, S5_WIDTH), lambda i: (i, 0)),
            pl.BlockSpec((row_block, 4 * HG_WIDTH), lambda i: (i, 0)),
            pl.BlockSpec((row_block, n_gate), lambda i: (i, 0)),
        ],
        out_shape=[
            jax.ShapeDtypeStruct((n, S5_WIDTH), BF16),
            jax.ShapeDtypeStruct((n, 4 * HG_WIDTH), BF16),
            jax.ShapeDtypeStruct((n, n_gate), BF16),
        ],
        compiler_params=_cparams(("parallel",)),
        name="inproj",
    )(rows, gain, w_in_bf16)


N_STATE = S5_GROUPS * S5_STATE
HALF_STATE = N_STATE // 2
HALF_COLS = 2 * HALF_STATE


SEG_COUNT = SUBLANES
SEG_LEN = TIME_BLOCK // SEG_COUNT
SCAN_LANE_GROUPS = 2
SCAN_UNROLL = 2
FIXUP_UNROLL = 4


def _state_cols(j):
    half, m0 = divmod(j * LANES, HALF_STATE)
    cre = half * HALF_COLS + m0
    return cre, cre + HALF_STATE, j * LANES


def _s5_local_scan(xs_refs, tab_ref, car_ref, cs_ref):
    sub = lax.broadcasted_iota(jnp.int32, (SUBLANES, LANES), 0)
    zero = jnp.zeros((SUBLANES, LANES), F32)
    for j0 in range(0, N_STATE // LANES, SCAN_LANE_GROUPS):
        groups = [_state_cols(j) for j in range(j0, j0 + SCAN_LANE_GROUPS)]
        lams = [(tab_ref[8, :, n0:n0 + LANES], tab_ref[9, :, n0:n0 + LANES]) for _, _, n0 in groups]
        init = tuple((zero, zero) for _ in groups for _ in xs_refs)

        def body(trip, carry, groups=groups, lams=lams):
            coef = [lam for lam in lams for _ in xs_refs]
            rows = [pl.multiple_of((trip * SCAN_UNROLL + q) * SUBLANES, SUBLANES)
                    for q in range(SCAN_UNROLL)]
            vals = [[(ref[pl.ds(r0, SUBLANES), cre:cre + LANES],
                      ref[pl.ds(r0, SUBLANES), cim:cim + LANES])
                     for cre, cim, _ in groups for ref in xs_refs] for r0 in rows]
            steps = []
            for q in range(SCAN_UNROLL):
                carry = tuple((lr * xr - li * xi + br, lr * xi + li * xr + bi)
                              for (lr, li), (xr, xi), (br, bi) in zip(coef, carry, vals[q]))
                steps.append(carry)
            for r0, outs in zip(rows, steps):
                k = 0
                for cre, cim, _ in groups:
                    for ref in xs_refs:
                        ref[pl.ds(r0, SUBLANES), cre:cre + LANES] = outs[k][0]
                        ref[pl.ds(r0, SUBLANES), cim:cim + LANES] = outs[k][1]
                        k += 1
            return carry

        ends = lax.fori_loop(0, SEG_LEN // SCAN_UNROLL, body, init)
        k = 0
        for cre, cim, n0 in groups:
            m = [tab_ref[i, :, n0:n0 + LANES] for i in range(8)]
            for c in range(len(xs_refs)):
                er, ei = ends[k]
                k += 1
                gr = jnp.where(sub == 0, car_ref[c, :, cre:cre + LANES], pltpu.roll(er, 1, 0))
                gi = jnp.where(sub == 0, car_ref[c, :, cim:cim + LANES], pltpu.roll(ei, 1, 0))
                for q, shift in enumerate((1, 2, 4)):
                    ar, ai = m[2 * q], m[2 * q + 1]
                    rr = pltpu.roll(gr, shift, 0)
                    ri = pltpu.roll(gi, shift, 0)
                    gr, gi = gr + ar * rr - ai * ri, gi + ar * ri + ai * rr
                cs_ref[c, :, cre:cre + LANES] = gr
                cs_ref[c, :, cim:cim + LANES] = gi
                xr = m[6] * gr - m[7] * gi + er
                xi = m[6] * gi + m[7] * gr + ei
                car_ref[c, :, cre:cre + LANES] = jnp.broadcast_to(
                    xr[SUBLANES - 1:SUBLANES, :], (SUBLANES, LANES))
                car_ref[c, :, cim:cim + LANES] = jnp.broadcast_to(
                    xi[SUBLANES - 1:SUBLANES, :], (SUBLANES, LANES))


def _s5_add_start_states(xs_refs, ptab_ref, cs_ref):
    for j in range(N_STATE // LANES):
        cre, cim, n0 = _state_cols(j)
        starts = [(cs_ref[c, :, cre:cre + LANES], cs_ref[c, :, cim:cim + LANES])
                  for c in range(len(xs_refs))]

        def body(trip, carry, cre=cre, cim=cim, n0=n0, starts=starts):
            rows = [pl.multiple_of((trip * FIXUP_UNROLL + q) * SUBLANES, SUBLANES)
                    for q in range(FIXUP_UNROLL)]
            loaded = [(ptab_ref[0, pl.ds(r0, SUBLANES), n0:n0 + LANES],
                       ptab_ref[1, pl.ds(r0, SUBLANES), n0:n0 + LANES],
                       [(ref[pl.ds(r0, SUBLANES), cre:cre + LANES],
                         ref[pl.ds(r0, SUBLANES), cim:cim + LANES]) for ref in xs_refs])
                      for r0 in rows]
            results = [[(xr + pr * cr - pi * ci, xi + pr * ci + pi * cr)
                        for (xr, xi), (cr, ci) in zip(vals, starts)] for pr, pi, vals in loaded]
            for r0, outs in zip(rows, results):
                for ref, (xr, xi) in zip(xs_refs, outs):
                    ref[pl.ds(r0, SUBLANES), cre:cre + LANES] = xr
                    ref[pl.ds(r0, SUBLANES), cim:cim + LANES] = xi
            return carry

        lax.fori_loop(0, SEG_LEN // FIXUP_UNROLL, body, 0)


def _s5_kernel(u_ref, um_ref, perm_ref, unperm_ref, bm_ref, cm_ref, tab_ref, ptab_ref, d_ref,
               wglu_ref, y_ref, car_ref, cs_ref, up_ref, *xs_refs, nb):
    half_ch = S5_WIDTH // 2

    def project_in(u, c):
        up = jnp.dot(perm_ref[...], u, preferred_element_type=F32).astype(BF16)
        up_ref[c] = up
        for hf in range(2):
            xs_refs[c][:, hf * HALF_COLS:(hf + 1) * HALF_COLS] = jnp.dot(
                up[:, hf * half_ch:(hf + 1) * half_ch], bm_ref[hf], preferred_element_type=F32)

    @pl.when(pl.program_id(0) == 0)
    def _():
        car_ref[...] = jnp.zeros_like(car_ref)
        project_in(um_ref[...], 0)
        _s5_local_scan(xs_refs[:1], tab_ref, car_ref, cs_ref)
        for b in range(1, nb):
            car_ref[b] = car_ref[0]

    for b in range(nb):
        project_in(u_ref[b], b)
    _s5_local_scan(xs_refs, tab_ref, car_ref, cs_ref)
    _s5_add_start_states(xs_refs, ptab_ref, cs_ref)
    for b in range(nb):
        ys = [jnp.dot(xs_refs[b][:, hf * HALF_COLS:(hf + 1) * HALF_COLS].astype(BF16), cm_ref[hf],
                      preferred_element_type=F32) for hf in range(2)]
        y = jnp.concatenate(ys, axis=1) + d_ref[...] * up_ref[b].astype(F32)
        y = 0.5 * y * (1.0 + jnp.tanh(0.7978845608028654 * (y + 0.044715 * (y * y * y))))
        z = jnp.dot(y.astype(BF16), wglu_ref[...], preferred_element_type=F32)
        out = (y * _sigmoid(z)).astype(BF16)
        y_ref[b] = jnp.dot(unperm_ref[...], out, preferred_element_type=F32).astype(BF16)


def _s5_mixer(u, u_meta, bm, cm, tab, ptab, d_skip, w_glu_bf16):
    nb, seq, _ = u.shape
    tb = TIME_BLOCK
    r = jnp.arange(tb)
    perm = (((r % SEG_COUNT) * SEG_LEN + r // SEG_COUNT)[:, None] == r[None, :]).astype(BF16)
    kern = functools.partial(_s5_kernel, nb=nb)
    const2 = lambda t: (0, 0)
    const3 = lambda t: (0, 0, 0)
    return pl.pallas_call(
        kern,
        grid=(seq // tb,),
        in_specs=[
            pl.BlockSpec((nb, tb, S5_WIDTH), lambda t: (0, t, 0)),
            pl.BlockSpec(u_meta.shape, const2),
            pl.BlockSpec(perm.shape, const2),
            pl.BlockSpec(perm.shape, const2),
            pl.BlockSpec(bm.shape, const3),
            pl.BlockSpec(cm.shape, const3),
            pl.BlockSpec(tab.shape, const3),
            pl.BlockSpec(ptab.shape, const3),
            pl.BlockSpec(d_skip.shape, const2),
            pl.BlockSpec(w_glu_bf16.shape, const2),
        ],
        out_specs=pl.BlockSpec((nb, tb, S5_WIDTH), lambda t: (0, t, 0)),
        out_shape=jax.ShapeDtypeStruct((nb, seq, S5_WIDTH), BF16),
        scratch_shapes=[pltpu.VMEM((nb, SUBLANES, 2 * N_STATE), F32),
                        pltpu.VMEM((nb, SUBLANES, 2 * N_STATE), F32),
                        pltpu.VMEM((nb, tb, S5_WIDTH), BF16)]
        + [pltpu.VMEM((tb, 2 * N_STATE), F32) for _ in range(nb)],
        compiler_params=_cparams(("arbitrary",)),
        name="s5_mixer",
    )(u, u_meta, perm, perm.T, bm, cm, tab, ptab, d_skip, w_glu_bf16)


def _s5_params(lam_re, lam_im, log_dt, b_re, b_im, c_re, c_im):
    lam = lax.complex(lam_re.astype(F32), lam_im.astype(F32))
    dt = jnp.exp(log_dt.astype(F32))[:, None]
    lam_dt = lam * dt
    lam_bar = jnp.exp(lam_dt)
    b_bar = ((lam_bar - 1.0) / lam)[:, :, None] * lax.complex(b_re.astype(F32), b_im.astype(F32))
    gl = S5_GROUPS // 2
    eye = jnp.eye(gl, dtype=F32)

    def in_half(bh):
        def blk(part):
            t = jnp.einsum('gph,gk->ghkp', part, eye)
            return t.reshape(gl * S5_GROUP, gl * S5_STATE)
        return jnp.concatenate([blk(jnp.real(bh)), blk(jnp.imag(bh))], axis=1)

    def out_half(cr, ci):
        def blk(part):
            t = jnp.einsum('ghp,gk->gpkh', part, eye)
            return t.reshape(gl * S5_STATE, gl * S5_GROUP)
        return jnp.concatenate([blk(cr), blk(-ci)], axis=0)

    bm = jnp.stack([in_half(b_bar[:gl]), in_half(b_bar[gl:])]).astype(BF16)
    cm = jnp.stack([out_half(c_re[:gl].astype(F32), c_im[:gl].astype(F32)),
                    out_half(c_re[gl:].astype(F32), c_im[gl:].astype(F32))]).astype(BF16)
    lam_flat = lam_dt.reshape(1, N_STATE)
    rows = jnp.arange(SUBLANES, dtype=F32)[:, None]
    tabs = []
    for shift in (1, 2, 4):
        p = jnp.exp(lam_flat * float(SEG_LEN * shift)) * (rows >= shift).astype(F32)
        tabs += [jnp.real(p), jnp.imag(p)]
    for p in (jnp.exp(lam_flat * float(SEG_LEN)), jnp.exp(lam_flat)):
        tabs += [jnp.real(p), jnp.imag(p)]
    tab = jnp.stack([jnp.broadcast_to(t, (SUBLANES, N_STATE)) for t in tabs]).astype(F32)
    steps = (jnp.arange(SEG_LEN * SUBLANES) // SUBLANES + 1).astype(F32)[:, None]
    p = jnp.exp(lam_flat * steps)
    ptab = jnp.stack([jnp.real(p), jnp.imag(p)]).astype(F32)
    return bm, cm, tab, ptab


def _dot_nt(a, b):
    return lax.dot_general(a, b, (((1,), (1,)), ((), ())), preferred_element_type=F32)


def _dot_tn(a, b):
    return lax.dot_general(a, b, (((0,), (0,)), ((), ())), preferred_element_type=F32)


def _cumsum_rows(tri, x):
    hi = x.astype(BF16)
    r1 = x - hi.astype(F32)
    mid = r1.astype(BF16)
    lo = (r1 - mid.astype(F32)).astype(BF16)
    return (jnp.dot(tri, hi, preferred_element_type=F32)
            + jnp.dot(tri, mid, preferred_element_type=F32)
            + jnp.dot(tri, lo, preferred_element_type=F32))


def _segment_rows(x, seg, pos):
    n = x.shape[0]
    parts = [jnp.broadcast_to(x[s * seg + pos:s * seg + pos + 1, :], (seg, x.shape[1]))
             for s in range(n // seg)]
    return parts[0] if len(parts) == 1 else jnp.concatenate(parts, axis=0)


def _hg_gates(x, lb, row_valid=None):
    w = HG_WIDTH
    q = x[:, 0:w]
    f = lb + (1.0 - lb) * _sigmoid(x[:, w:2 * w])
    logf = jnp.log(f)
    k = 1.0 - f
    if row_valid is not None:
        logf = jnp.where(row_valid, logf, 0.0)
        k = jnp.where(row_valid, k, 0.0)
    return _silu(q), k, logf


def _hg_kernel(x_ref, xm_ref, lb_ref, gain_ref, tri_ref, y_ref, st_ref, *, nb, tb):
    w, dh = HG_WIDTH, HG_HEAD_DIM
    lb = lb_ref[...]
    tri = tri_ref[...]

    @pl.when(pl.program_id(0) == 0)
    def _():
        xm = xm_ref[...].astype(F32)
        valid = lax.broadcasted_iota(jnp.int32, (META_PAD, 1), 0) >= (META_PAD - N_META)
        _, k, logf = _hg_gates(xm, lb, valid)
        bc = _cumsum_rows(tri[0:META_PAD, 0:META_PAD], logf)
        ki = (k * jnp.exp(bc[META_PAD - 1:META_PAD, :] - bc)).astype(BF16)
        v = xm[:, 2 * w:3 * w].astype(BF16)
        for h in range(HG_HEADS):
            sl = slice(h * dh, (h + 1) * dh)
            s0 = _dot_tn(v[:, sl], ki[:, sl])
            for b in range(nb):
                st_ref[b, h] = s0

    ri = lax.broadcasted_iota(jnp.int32, (tb, tb), 0)
    ci = lax.broadcasted_iota(jnp.int32, (tb, tb), 1)
    diag_shift = HG_DIAG.bit_length() - 1
    diag_mask = ((ri >> diag_shift) == (ci >> diag_shift)) & (ci <= ri)
    levels = []
    seg = 2 * HG_DIAG
    while seg <= tb:
        levels.append(seg)
        seg *= 2
    seg_masks = [None if s == tb else
                 ((ri >> (s.bit_length() - 1)) == (ci >> (s.bit_length() - 1))).astype(F32)
                 for s in levels]
    row = lax.broadcasted_iota(jnp.int32, (tb, 1), 0)

    for b in range(nb):
        x = x_ref[b].astype(F32)
        qs, k, logf = _hg_gates(x, lb)
        v = x[:, 2 * w:3 * w].astype(BF16)
        g = x[:, 3 * w:4 * w]
        bc = _cumsum_rows(tri, logf)
        dlt = bc - _segment_rows(bc, HG_DIAG, HG_DIAG // 2 - 1)
        qk = [((qs * jnp.exp(dlt)).astype(BF16), (k * jnp.exp(-dlt)).astype(BF16))]
        for s in levels:
            e = jnp.exp(-jnp.abs(bc - _segment_rows(bc, s, s // 2 - 1)))
            upper = (row & (s - 1)) >= (s // 2)
            qk.append((jnp.where(upper, qs * e, 0.0).astype(BF16),
                       jnp.where(upper, 0.0, k * e).astype(BF16)))
        b_last = bc[tb - 1:tb, :]
        qi = (qs * jnp.exp(bc)).astype(BF16)
        ki = (k * jnp.exp(b_last - bc)).astype(BF16)
        dec = jnp.exp(b_last)
        outs = []
        for h in range(HG_HEADS):
            sl = slice(h * dh, (h + 1) * dh)
            sc = jnp.where(diag_mask, _dot_nt(qk[0][0][:, sl], qk[0][1][:, sl]), 0.0)
            for (ql, kl), m in zip(qk[1:], seg_masks):
                t = _dot_nt(ql[:, sl], kl[:, sl])
                sc = sc + (t if m is None else t * m)
            st = st_ref[b, h]
            o = (jnp.dot(sc.astype(BF16), v[:, sl], preferred_element_type=F32)
                 + _dot_nt(qi[:, sl], st.astype(BF16)))
            st_ref[b, h] = dec[:, sl] * st + _dot_tn(v[:, sl], ki[:, sl])
            ms = jnp.mean(o * o, axis=-1, keepdims=True)
            outs.append(o * lax.rsqrt(ms + EPS))
        o = jnp.concatenate(outs, axis=1) * gain_ref[...]
        y_ref[b] = (o * _silu(g)).astype(BF16)


def _hg_mixer(x, x_meta, lb, gain, tri):
    nb, seq, _ = x.shape
    tb = TIME_BLOCK
    kern = functools.partial(_hg_kernel, nb=nb, tb=tb)
    const2 = lambda t: (0, 0)
    return pl.pallas_call(
        kern,
        grid=(seq // tb,),
        in_specs=[
            pl.BlockSpec((nb, tb, 4 * HG_WIDTH), lambda t: (0, t, 0)),
            pl.BlockSpec(x_meta.shape, const2),
            pl.BlockSpec(lb.shape, const2),
            pl.BlockSpec(gain.shape, const2),
            pl.BlockSpec(tri.shape, const2),
        ],
        out_specs=pl.BlockSpec((nb, tb, HG_WIDTH), lambda t: (0, t, 0)),
        out_shape=jax.ShapeDtypeStruct((nb, seq, HG_WIDTH), BF16),
        scratch_shapes=[pltpu.VMEM((nb, HG_HEADS, HG_HEAD_DIM, HG_HEAD_DIM), F32)],
        compiler_params=_cparams(("arbitrary",)),
        name="hgrn2_mixer",
    )(x, x_meta, lb, gain, tri)


def _split3(x):
    hi = x.astype(BF16)
    lo = (x - hi.astype(F32)).astype(BF16)
    return hi, lo


def _pack_pairs(y):
    n = y.shape[1] // 2
    lo = pltpu.bitcast(y[:, :n].astype(BF16).astype(F32), jnp.uint32)
    hi = pltpu.bitcast(y[:, n:].astype(BF16).astype(F32), jnp.uint32)
    return (lo >> 16) | (hi & jnp.uint32(0xFFFF0000))


def _unpack_pairs(p):
    lo = pltpu.bitcast(p << 16, F32)
    hi = pltpu.bitcast(p & jnp.uint32(0xFFFF0000), F32)
    return lo, hi


def _merge_kernel(x_ref, ys_ref, yh_ref, gt_ref, wbs_ref, wbh_ref, wo_ref, g2_ref, wr_hi_ref,
                  wr_lo_ref, br_ref, tri_ref, h2_ref, xp_ref, eid_ref, wt_ref, rank_ref, cnt_ref,
                  run_ref):
    @pl.when(pl.program_id(0) == 0)
    def _():
        run_ref[...] = jnp.zeros_like(run_ref)

    gt = gt_ref[...].astype(F32)
    gs = _sigmoid(gt[:, :D_MODEL])
    gh = _sigmoid(gt[:, D_MODEL:])
    merged = (gs * jnp.dot(ys_ref[...], wbs_ref[...], preferred_element_type=F32)
              + gh * jnp.dot(yh_ref[...], wbh_ref[...], preferred_element_type=F32))
    h2 = x_ref[...] + jnp.dot(merged.astype(BF16), wo_ref[...], preferred_element_type=F32)
    h2_ref[...] = h2
    ms = jnp.mean(h2 * h2, axis=-1, keepdims=True)
    xn = h2 * lax.rsqrt(ms + EPS) * g2_ref[...]
    xp_ref[...] = _pack_pairs(xn)

    x_hi, x_lo = _split3(xn)
    logits = (jnp.dot(x_hi, wr_hi_ref[...], preferred_element_type=F32)
              + jnp.dot(x_hi, wr_lo_ref[...], preferred_element_type=F32)
              + jnp.dot(x_lo, wr_hi_ref[...], preferred_element_type=F32)) + br_ref[...]
    rows = logits.shape[0]
    lane = lax.broadcasted_iota(jnp.int32, (rows, LANES), 1)
    tops, hots = [], []
    sel = jnp.zeros((rows, LANES), F32)
    for k in range(TOP_K):
        m = jnp.max(logits, axis=-1, keepdims=True)
        idx = jnp.min(jnp.where(logits == m, lane, LANES), axis=-1, keepdims=True)
        hot = lane == idx
        logits = jnp.where(hot, NEG_BIG, logits)
        tops.append(m)
        hots.append(hot)
        sel = sel + hot.astype(F32)
        eid_ref[:, k:k + 1] = idx
    es = [jnp.exp(m - tops[0]) for m in tops]
    tot = es[0] + es[1] + es[2] + es[3]
    for k in range(TOP_K):
        wt_ref[:, k:k + 1] = es[k] / tot
    prefix = jnp.dot(tri_ref[...], sel.astype(BF16), preferred_element_type=F32) + run_ref[...]
    for k in range(TOP_K):
        r = jnp.sum(jnp.where(hots[k], prefix, 0.0), axis=-1, keepdims=True)
        rank_ref[:, k:k + 1] = r.astype(jnp.int32)
    run = run_ref[...] + jnp.sum(sel, axis=0, keepdims=True)
    run_ref[...] = run
    cnt_ref[...] = run


def _merge_router(x_rows, y_s5, y_hg, gates, wbs, wbh, wo, g2, wr_hi, wr_lo, br, tri_strict):
    n = x_rows.shape[0]
    rb = ROW_BLOCK
    const2 = lambda i: (0, 0)
    rowblk = lambda width: pl.BlockSpec((rb, width), lambda i: (i, 0))
    return pl.pallas_call(
        _merge_kernel,
        grid=(n // rb,),
        in_specs=[
            rowblk(D_MODEL), rowblk(S5_WIDTH), rowblk(HG_WIDTH), rowblk(2 * D_MODEL),
            pl.BlockSpec(wbs.shape, const2), pl.BlockSpec(wbh.shape, const2),
            pl.BlockSpec(wo.shape, const2), pl.BlockSpec(g2.shape, const2),
            pl.BlockSpec(wr_hi.shape, const2), pl.BlockSpec(wr_lo.shape, const2),
            pl.BlockSpec(br.shape, const2), pl.BlockSpec(tri_strict.shape, const2),
        ],
        out_specs=[
            rowblk(D_MODEL), rowblk(D_MODEL // 2), rowblk(TOP_K), rowblk(TOP_K), rowblk(TOP_K),
            pl.BlockSpec((1, LANES), const2),
        ],
        out_shape=[
            jax.ShapeDtypeStruct((n, D_MODEL), F32),
            jax.ShapeDtypeStruct((n, D_MODEL // 2), jnp.uint32),
            jax.ShapeDtypeStruct((n, TOP_K), jnp.int32),
            jax.ShapeDtypeStruct((n, TOP_K), F32),
            jax.ShapeDtypeStruct((n, TOP_K), jnp.int32),
            jax.ShapeDtypeStruct((1, LANES), F32),
        ],
        scratch_shapes=[pltpu.VMEM((1, LANES), F32)],
        compiler_params=_cparams(("arbitrary",)),
        name="merge_router",
    )(x_rows, y_s5, y_hg, gates, wbs, wbh, wo, g2, wr_hi, wr_lo, br, tri_strict)


def _gather_rows(src, idx):
    n = idx.shape[0]
    width = src.shape[1]
    assert n % GATHER_ROWS == 0
    mesh = plsc.VectorSubcoreMesh(core_axis_name="core", subcore_axis_name="subcore")

    @functools.partial(pl.kernel, out_type=jax.ShapeDtypeStruct((n, width), src.dtype),
                       mesh=mesh, scratch_types=[], name="sc_gather_rows")
    def gather(src_hbm, idx_hbm, out_hbm):
        def body(idx_vmem, out_vmem):
            off = pl.multiple_of(pl.program_id(1) * GATHER_WINDOW, GATHER_WINDOW)
            pltpu.sync_copy(src_hbm.at[idx_vmem.at[0, pl.ds(off, GATHER_WINDOW)]], out_vmem)

        pltpu.emit_pipeline(
            body,
            grid=(n // INDEX_BLOCK, INDEX_SPLIT),
            in_specs=[pl.BlockSpec((1, INDEX_BLOCK), lambda i, j: (0, i))],
            out_specs=[pl.BlockSpec((GATHER_WINDOW, width), lambda i, j: (INDEX_SPLIT * i + j, 0))],
            core_axis_name=("core", "subcore"),
            dimension_semantics=(pltpu.PARALLEL, pltpu.ARBITRARY),
        )(idx_hbm, out_hbm)

    return gather(src, idx.reshape(1, n))


def _scatter_rows(src, dest, n_out):
    n_src, width = src.shape
    n = dest.shape[0]
    assert n == TOP_K * n_src and n % GATHER_ROWS == 0 and n_src % GATHER_WINDOW == 0
    steps_per_copy = n_src // GATHER_WINDOW
    mesh = plsc.VectorSubcoreMesh(core_axis_name="core", subcore_axis_name="subcore")

    @functools.partial(pl.kernel, out_type=jax.ShapeDtypeStruct((n_out, width), src.dtype),
                       mesh=mesh, scratch_types=[], name="sc_scatter_rows")
    def scatter(src_hbm, idx_hbm, out_hbm):
        def body(src_vmem, idx_vmem):
            off = pl.multiple_of(pl.program_id(1) * GATHER_WINDOW, GATHER_WINDOW)
            pltpu.sync_copy(src_vmem, out_hbm.at[idx_vmem.at[0, pl.ds(off, GATHER_WINDOW)]])

        pltpu.emit_pipeline(
            body,
            grid=(n // INDEX_BLOCK, INDEX_SPLIT),
            in_specs=[pl.BlockSpec((GATHER_WINDOW, width),
                                   lambda i, j: ((INDEX_SPLIT * i + j) % steps_per_copy, 0)),
                      pl.BlockSpec((1, INDEX_BLOCK), lambda i, j: (0, i))],
            out_specs=[],
            core_axis_name=("core", "subcore"),
            dimension_semantics=(pltpu.PARALLEL, pltpu.ARBITRARY),
        )(src_hbm, idx_hbm)

    return scatter(src, dest.reshape(1, n))


GU_GROUP = 512


def _expert_kernel(be_ref, nv_ref, nr_ref, x_ref, wgu_ref, bgu_ref, wd_ref, bd_ref, perm_ref,
                   y_ref, wgu_s, wd_s):
    i = pl.program_id(0)
    prev = be_ref[jnp.maximum(i - 1, 0)]
    fresh = (i == 0) | (be_ref[i] != prev)

    @pl.when(fresh & (i < nv_ref[0]))
    def _():
        for c in range(2 * D_EXPERT // GU_GROUP):
            cols = slice(c * GU_GROUP, (c + 1) * GU_GROUP)
            w = wgu_ref[0, :, cols].astype(BF16)
            wgu_s[:, cols] = jnp.dot(w, perm_ref[...], preferred_element_type=F32).astype(BF16)
        wd_s[...] = wd_ref[0].astype(BF16)

    @pl.when(i < nv_ref[0])
    def _():
        half = D_MODEL // 2
        live = lax.broadcasted_iota(jnp.int32, (MOE_BLOCK, 1), 0) < nr_ref[i]
        xa, xb = _unpack_pairs(jnp.where(live, x_ref[...], jnp.uint32(0)))
        x = jnp.concatenate([xa.astype(BF16), xb.astype(BF16)], axis=1)
        hw = GU_GROUP // 2
        n_groups = 2 * D_EXPERT // GU_GROUP

        def gate_up(c):
            return jnp.dot(x, wgu_s[:, c * GU_GROUP:(c + 1) * GU_GROUP],
                           preferred_element_type=F32)

        acc = None
        hcols = []
        gu_next = gate_up(0)
        for c in range(n_groups):
            gu = gu_next + bgu_ref[0, :, c * GU_GROUP:(c + 1) * GU_GROUP]
            if c + 1 < n_groups:
                gu_next = gate_up(c + 1)
            gate = jnp.minimum(gu[:, :hw], SWIGLU_LIMIT)
            up = jnp.clip(gu[:, hw:], -SWIGLU_LIMIT, SWIGLU_LIMIT)
            hcols.append(((up + 1.0) * (gate * _sigmoid(gate * SWIGLU_ALPHA))).astype(BF16))
            if c % 2 == 1:
                part = jnp.dot(jnp.concatenate(hcols[c - 1:c + 1], axis=1),
                               wd_s[(c - 1) * hw:(c + 1) * hw, :], preferred_element_type=F32)
                acc = part if acc is None else acc + part
        y_ref[...] = _pack_pairs(acc + bd_ref[0])

    @pl.when(i >= nv_ref[0])
    def _():
        y_ref[...] = jnp.zeros_like(y_ref)


def _experts(block_e, n_valid, block_rows, x_rows, w_gate_up, b_gu_grouped, w_down, b_down, perm):
    n_rows = x_rows.shape[0]
    n_blocks = n_rows // MOE_BLOCK
    half = D_MODEL // 2
    by_expert = lambda i, be, nv, nr: (be[i], 0, 0)
    grid_spec = pltpu.PrefetchScalarGridSpec(
        num_scalar_prefetch=3,
        grid=(n_blocks,),
        in_specs=[
            pl.BlockSpec((MOE_BLOCK, half), lambda i, be, nv, nr: (i, 0)),
            pl.BlockSpec((1, D_MODEL, 2 * D_EXPERT), by_expert),
            pl.BlockSpec((1, 1, 2 * D_EXPERT), by_expert),
            pl.BlockSpec((1, D_EXPERT, D_MODEL), by_expert),
            pl.BlockSpec((1, 1, D_MODEL), by_expert),
            pl.BlockSpec(perm.shape, lambda i, be, nv, nr: (0, 0)),
        ],
        out_specs=pl.BlockSpec((MOE_BLOCK, half), lambda i, be, nv, nr: (i, 0)),
        scratch_shapes=[
            pltpu.VMEM((D_MODEL, 2 * D_EXPERT), BF16),
            pltpu.VMEM((D_EXPERT, D_MODEL), BF16),
        ],
    )
    return pl.pallas_call(
        _expert_kernel,
        grid_spec=grid_spec,
        out_shape=jax.ShapeDtypeStruct((n_rows, half), jnp.uint32),
        compiler_params=_cparams(("arbitrary",)),
        name="experts",
    )(block_e, n_valid, block_rows, x_rows, w_gate_up, b_gu_grouped, w_down, b_down, perm)


def _combine_kernel(h2_ref, y0_ref, y1_ref, y2_ref, y3_ref, wt_ref, gf_ref, out_ref):
    half = D_MODEL // 2
    wt = wt_ref[...]
    lo = jnp.zeros((h2_ref.shape[0], half), F32)
    hi = jnp.zeros((h2_ref.shape[0], half), F32)
    for k, yk_ref in enumerate((y0_ref, y1_ref, y2_ref, y3_ref)):
        a, b = _unpack_pairs(yk_ref[...])
        lo = lo + wt[:, k:k + 1] * a
        hi = hi + wt[:, k:k + 1] * b
    y = h2_ref[...] + jnp.concatenate([lo, hi], axis=1)
    ms = jnp.mean(y * y, axis=-1, keepdims=True)
    out_ref[...] = y * lax.rsqrt(ms + EPS) * gf_ref[...]


def _combine(h2, y_tok, wts, gain):
    n = h2.shape[0]
    rb = ROW_BLOCK
    rowblk = lambda width: pl.BlockSpec((rb, width), lambda i: (i, 0))
    steps = n // rb
    choice = lambda k: pl.BlockSpec((rb, D_MODEL // 2), lambda i: (k * steps + i, 0))
    return pl.pallas_call(
        _combine_kernel,
        grid=(steps,),
        in_specs=[rowblk(D_MODEL)] + [choice(k) for k in range(TOP_K)]
        + [rowblk(TOP_K), pl.BlockSpec((1, D_MODEL), lambda i: (0, 0))],
        out_specs=rowblk(D_MODEL),
        out_shape=jax.ShapeDtypeStruct((n, D_MODEL), F32),
        compiler_params=_cparams(("parallel",)),
        name="combine_norm",
    )(h2, y_tok, y_tok, y_tok, y_tok, wts, gain)


def _lower_tri(n, strict):
    r = lax.broadcasted_iota(jnp.int32, (n, n), 0)
    c = lax.broadcasted_iota(jnp.int32, (n, n), 1)
    return ((c < r) if strict else (c <= r)).astype(BF16)


def kernel(x, meta_tokens, norm1_gain, w_in, s5_lambda_re, s5_lambda_im, s5_log_dt, s5_b_re,
           s5_b_im, s5_c_re, s5_c_im, s5_d, s5_w_glu, hgrn_lb_logits, hgrn_norm_gain,
           w_branch_s5, w_branch_hgrn, w_out, norm2_gain, w_router, b_router, w_gate_up,
           b_gate_up, w_down, b_down, final_norm_gain):
    nb, seq, d = x.shape
    n_tok = nb * seq
    assert d == D_MODEL and seq % TIME_BLOCK == 0 and n_tok % ROW_BLOCK == 0
    x_rows = x.reshape(n_tok, d)
    w_in_b = w_in[0].astype(BF16)
    g1 = norm1_gain[0].reshape(1, d).astype(F32)

    u, qfvg, gates = _inproj(x_rows, g1, w_in_b, ROW_BLOCK)
    meta_rows = jnp.concatenate(
        [jnp.zeros((META_PAD - N_META, d), F32), meta_tokens.astype(F32)], axis=0)
    u_m, qfvg_m, _ = _inproj(meta_rows, g1, w_in_b, META_PAD)

    bm, cm, tab, ptab = _s5_params(s5_lambda_re[0], s5_lambda_im[0], s5_log_dt[0], s5_b_re[0],
                                   s5_b_im[0], s5_c_re[0], s5_c_im[0])
    y_s5 = _s5_mixer(u.reshape(nb, seq, S5_WIDTH), u_m, bm, cm, tab, ptab,
                     s5_d[0].reshape(1, S5_WIDTH).astype(F32), s5_w_glu[0].astype(BF16))

    lower_bounds = jnp.cumsum(jax.nn.softmax(hgrn_lb_logits.astype(F32), axis=0), axis=0)
    lb = lower_bounds[0].reshape(1, HG_WIDTH)
    y_hg = _hg_mixer(qfvg.reshape(nb, seq, 4 * HG_WIDTH), qfvg_m, lb,
                     hgrn_norm_gain[0].reshape(1, HG_WIDTH).astype(F32),
                     _lower_tri(TIME_BLOCK, strict=False))

    wr = jnp.zeros((d, LANES), F32).at[:, :N_EXPERTS].set(w_router[0].astype(F32))
    wr_hi = wr.astype(BF16)
    wr_lo = (wr - wr_hi.astype(F32)).astype(BF16)
    br = jnp.full((1, LANES), NEG_BIG, F32).at[0, :N_EXPERTS].set(b_router[0].astype(F32))
    h2, xn_packed, eid, wts, rank, counts = _merge_router(
        x_rows, y_s5.reshape(n_tok, S5_WIDTH), y_hg.reshape(n_tok, HG_WIDTH), gates,
        w_branch_s5[0].astype(BF16), w_branch_hgrn[0].astype(BF16), w_out[0].astype(BF16),
        norm2_gain[0].reshape(1, d).astype(F32), wr_hi, wr_lo, br,
        _lower_tri(ROW_BLOCK, strict=True))

    n_assign = n_tok * TOP_K
    n_blocks = n_assign // MOE_BLOCK + N_EXPERTS
    n_rows = n_blocks * MOE_BLOCK
    n_rows_pad = -(-n_rows // GATHER_ROWS) * GATHER_ROWS
    cnt = counts[0, :N_EXPERTS].astype(jnp.int32)
    padded = (cnt + MOE_BLOCK - 1) // MOE_BLOCK * MOE_BLOCK
    padded_end = jnp.cumsum(padded)
    padded_start = padded_end - padded
    onehot = (eid[:, :, None] == jnp.arange(N_EXPERTS, dtype=jnp.int32)).astype(jnp.int32)
    dest = (jnp.sum(onehot * padded_start, axis=-1) + rank).T.reshape(-1)
    block_start = jnp.arange(n_rows_pad // MOE_BLOCK, dtype=jnp.int32) * MOE_BLOCK
    block_e = jnp.minimum(jnp.sum((block_start[:, None] >= padded_end[None, :]).astype(jnp.int32),
                                  axis=1), N_EXPERTS - 1)
    block_rows = jnp.clip(cnt[block_e] - (block_start - padded_start[block_e]), 0, MOE_BLOCK)
    n_valid = (padded_end[-1] // MOE_BLOCK).astype(jnp.int32).reshape(1)

    x_sorted = _scatter_rows(xn_packed, dest, n_rows_pad)
    hw = GU_GROUP // 2
    pr = jnp.arange(GU_GROUP)
    src = jnp.where(pr < hw, 2 * pr, 2 * (pr - hw) + 1)
    perm = (jnp.arange(GU_GROUP)[:, None] == src[None, :]).astype(BF16)
    b_gu = b_gate_up[0].astype(F32).reshape(N_EXPERTS, 2 * D_EXPERT // GU_GROUP, hw, 2)
    b_gu = b_gu.transpose(0, 1, 3, 2).reshape(N_EXPERTS, 1, 2 * D_EXPERT)
    y_sorted = _experts(block_e, n_valid, block_rows, x_sorted, w_gate_up[0], b_gu, w_down[0],
                        b_down[0].astype(F32).reshape(N_EXPERTS, 1, d), perm)
    y_tok = _gather_rows(y_sorted, dest)

    out = _combine(h2, y_tok, wts, final_norm_gain.reshape(1, d).astype(F32))
    return out.reshape(nb, seq, d)
```

```python
import functools

import jax
import jax.numpy as jnp
from jax import lax
from jax.experimental import pallas as pl
from jax.experimental.pallas import tpu as pltpu
from jax.experimental.pallas import tpu_sc as plsc

F32 = jnp.float32
BF16 = jnp.bfloat16

D_MODEL = 1024
N_META = 16
S5_GROUP = 16
S5_GROUPS = 32
S5_WIDTH = 512
S5_STATE = 64
HG_HEADS = 4
HG_HEAD_DIM = 128
HG_WIDTH = 512
N_EXPERTS = 32
TOP_K = 4
D_EXPERT = 1024
SWIGLU_ALPHA = 1.702
SWIGLU_LIMIT = 7.0
EPS = 1e-6

LANES = 128
SUBLANES = 8
TIME_BLOCK = 256
META_PAD = TIME_BLOCK
ROW_BLOCK = 512
MOE_BLOCK = 512
GATHER_WINDOW = 64
INDEX_BLOCK = 128
INDEX_SPLIT = INDEX_BLOCK // GATHER_WINDOW
SC_SUBCORES = 32
GATHER_ROWS = INDEX_BLOCK * SC_SUBCORES
HG_DIAG = 32
NEG_BIG = -1e30
VMEM_LIMIT = 56 * 1024 * 1024


def _cparams(sem):
    return pltpu.CompilerParams(dimension_semantics=sem, vmem_limit_bytes=VMEM_LIMIT)


def _sigmoid(x):
    return 0.5 * jnp.tanh(0.5 * x) + 0.5


def _silu(x):
    h = 0.5 * x
    return h + h * jnp.tanh(h)


def _inproj_kernel(x_ref, g_ref, w_ref, u_ref, qfvg_ref, gates_ref):
    x = x_ref[...]
    ms = jnp.mean(x * x, axis=-1, keepdims=True)
    xn = (x * lax.rsqrt(ms + EPS) * g_ref[...]).astype(BF16)
    u_ref[...] = jnp.dot(xn, w_ref[:, 0:S5_WIDTH], preferred_element_type=F32).astype(BF16)
    a, b = S5_WIDTH, S5_WIDTH + 4 * HG_WIDTH
    qfvg_ref[...] = jnp.dot(xn, w_ref[:, a:b], preferred_element_type=F32).astype(BF16)
    gates_ref[...] = jnp.dot(xn, w_ref[:, b:], preferred_element_type=F32).astype(BF16)


def _inproj(rows, gain, w_in_bf16, row_block):
    n = rows.shape[0]
    n_gate = 2 * D_MODEL
    return pl.pallas_call(
        _inproj_kernel,
        grid=(n // row_block,),
        in_specs=[
            pl.BlockSpec((row_block, D_MODEL), lambda i: (i, 0)),
            pl.BlockSpec((1, D_MODEL), lambda i: (0, 0)),
            pl.BlockSpec(w_in_bf16.shape, lambda i: (0, 0)),
        ],
        out_specs=[
            pl.BlockSpec((row_block, S5_WIDTH), lambda i: (i, 0)),
            pl.BlockSpec((row_block, 4 * HG_WIDTH), lambda i: (i, 0)),
            pl.BlockSpec((row_block, n_gate), lambda i: (i, 0)),
        ],
        out_shape=[
            jax.ShapeDtypeStruct((n, S5_WIDTH), BF16),
            jax.ShapeDtypeStruct((n, 4 * HG_WIDTH), BF16),
            jax.ShapeDtypeStruct((n, n_gate), BF16),
        ],
        compiler_params=_cparams(("parallel",)),
        name="inproj",
    )(rows, gain, w_in_bf16)


N_STATE = S5_GROUPS * S5_STATE
HALF_STATE = N_STATE // 2
HALF_COLS = 2 * HALF_STATE


SEG_COUNT = SUBLANES
SEG_LEN = TIME_BLOCK // SEG_COUNT
SCAN_CHAINS = 8
SCAN_BATCH_GROUP = 2


def _state_cols(j):
    half, m0 = divmod(j * LANES, HALF_STATE)
    cre = half * HALF_COLS + m0
    return cre, cre + HALF_STATE, j * LANES


def _s5_local_scan(chains, tab_ref, car_ref, cs_ref):
    sub = lax.broadcasted_iota(jnp.int32, (SUBLANES, LANES), 0)
    zero = jnp.zeros((SUBLANES, LANES), F32)
    lane_groups = max(1, SCAN_CHAINS // len(chains))
    for j0 in range(0, N_STATE // LANES, lane_groups):
        groups = [_state_cols(j) for j in range(j0, j0 + lane_groups)]
        cols = [(ref, cre, cim) for cre, cim, _ in groups for ref, _ in chains]
        coef = [(tab_ref[8, :, n0:n0 + LANES], tab_ref[9, :, n0:n0 + LANES])
                for _, _, n0 in groups for _ in chains]
        state = [(zero, zero)] * len(cols)
        for tau in range(SEG_LEN):
            rows = slice(tau * SUBLANES, (tau + 1) * SUBLANES)
            vals = [(ref[rows, cre:cre + LANES], ref[rows, cim:cim + LANES])
                    for ref, cre, cim in cols]
            state = [(lr * xr - li * xi + br, lr * xi + li * xr + bi)
                     for (lr, li), (xr, xi), (br, bi) in zip(coef, state, vals)]
            for (ref, cre, cim), (xr, xi) in zip(cols, state):
                ref[rows, cre:cre + LANES] = xr
                ref[rows, cim:cim + LANES] = xi
        k = 0
        for cre, cim, n0 in groups:
            m = [tab_ref[i, :, n0:n0 + LANES] for i in range(8)]
            for _, c in chains:
                er, ei = state[k]
                k += 1
                gr = jnp.where(sub == 0, car_ref[c, :, cre:cre + LANES], pltpu.roll(er, 1, 0))
                gi = jnp.where(sub == 0, car_ref[c, :, cim:cim + LANES], pltpu.roll(ei, 1, 0))
                for q, shift in enumerate((1, 2, 4)):
                    ar, ai = m[2 * q], m[2 * q + 1]
                    rr = pltpu.roll(gr, shift, 0)
                    ri = pltpu.roll(gi, shift, 0)
                    gr, gi = gr + ar * rr - ai * ri, gi + ar * ri + ai * rr
                cs_ref[c, :, cre:cre + LANES] = gr
                cs_ref[c, :, cim:cim + LANES] = gi
                xr = m[6] * gr - m[7] * gi + er
                xi = m[6] * gi + m[7] * gr + ei
                car_ref[c, :, cre:cre + LANES] = jnp.broadcast_to(
                    xr[SUBLANES - 1:SUBLANES, :], (SUBLANES, LANES))
                car_ref[c, :, cim:cim + LANES] = jnp.broadcast_to(
                    xi[SUBLANES - 1:SUBLANES, :], (SUBLANES, LANES))


def _s5_add_start_states(chains, ptab_ref, cs_ref):
    for j in range(N_STATE // LANES):
        cre, cim, n0 = _state_cols(j)
        starts = [(cs_ref[c, :, cre:cre + LANES], cs_ref[c, :, cim:cim + LANES]) for _, c in chains]
        for tau in range(SEG_LEN):
            rows = slice(tau * SUBLANES, (tau + 1) * SUBLANES)
            pr = ptab_ref[0, rows, n0:n0 + LANES]
            pi = ptab_ref[1, rows, n0:n0 + LANES]
            vals = [(ref[rows, cre:cre + LANES], ref[rows, cim:cim + LANES]) for ref, _ in chains]
            for (ref, _), (xr, xi), (cr, ci) in zip(chains, vals, starts):
                ref[rows, cre:cre + LANES] = xr + pr * cr - pi * ci
                ref[rows, cim:cim + LANES] = xi + pr * ci + pi * cr


def _s5_kernel(u_ref, um_ref, perm_ref, unperm_ref, bm_ref, cm_ref, tab_ref, ptab_ref, d_ref,
               wglu_ref, y_ref, car_ref, cs_ref, up_ref, *xs_refs, nb):
    half_ch = S5_WIDTH // 2

    def project_in(u, c):
        up = jnp.dot(perm_ref[...], u, preferred_element_type=F32).astype(BF16)
        up_ref[c] = up
        for hf in range(2):
            xs_refs[c][:, hf * HALF_COLS:(hf + 1) * HALF_COLS] = jnp.dot(
                up[:, hf * half_ch:(hf + 1) * half_ch], bm_ref[hf], preferred_element_type=F32)

    @pl.when(pl.program_id(0) == 0)
    def _():
        car_ref[...] = jnp.zeros_like(car_ref)
        project_in(um_ref[...], 0)
        _s5_local_scan([(xs_refs[0], 0)], tab_ref, car_ref, cs_ref)
        for b in range(1, nb):
            car_ref[b] = car_ref[0]

    def project_out(b):
        ys = [jnp.dot(xs_refs[b][:, hf * HALF_COLS:(hf + 1) * HALF_COLS].astype(BF16), cm_ref[hf],
                      preferred_element_type=F32) for hf in range(2)]
        y = jnp.concatenate(ys, axis=1) + d_ref[...] * up_ref[b].astype(F32)
        y = 0.5 * y * (1.0 + jnp.tanh(0.7978845608028654 * (y + 0.044715 * (y * y * y))))
        z = jnp.dot(y.astype(BF16), wglu_ref[...], preferred_element_type=F32)
        out = (y * _sigmoid(z)).astype(BF16)
        y_ref[b] = jnp.dot(unperm_ref[...], out, preferred_element_type=F32).astype(BF16)

    for b in range(nb):
        project_in(u_ref[b], b)
    for b0 in range(0, nb, SCAN_BATCH_GROUP):
        group = range(b0, min(nb, b0 + SCAN_BATCH_GROUP))
        chains = [(xs_refs[b], b) for b in group]
        _s5_local_scan(chains, tab_ref, car_ref, cs_ref)
        _s5_add_start_states(chains, ptab_ref, cs_ref)
        for b in group:
            project_out(b)


def _s5_mixer(u, u_meta, bm, cm, tab, ptab, d_skip, w_glu_bf16):
    nb, seq, _ = u.shape
    tb = TIME_BLOCK
    r = jnp.arange(tb)
    perm = (((r % SEG_COUNT) * SEG_LEN + r // SEG_COUNT)[:, None] == r[None, :]).astype(BF16)
    kern = functools.partial(_s5_kernel, nb=nb)
    const2 = lambda t: (0, 0)
    const3 = lambda t: (0, 0, 0)
    return pl.pallas_call(
        kern,
        grid=(seq // tb,),
        in_specs=[
            pl.BlockSpec((nb, tb, S5_WIDTH), lambda t: (0, t, 0)),
            pl.BlockSpec(u_meta.shape, const2),
            pl.BlockSpec(perm.shape, const2),
            pl.BlockSpec(perm.shape, const2),
            pl.BlockSpec(bm.shape, const3),
            pl.BlockSpec(cm.shape, const3),
            pl.BlockSpec(tab.shape, const3),
            pl.BlockSpec(ptab.shape, const3),
            pl.BlockSpec(d_skip.shape, const2),
            pl.BlockSpec(w_glu_bf16.shape, const2),
        ],
        out_specs=pl.BlockSpec((nb, tb, S5_WIDTH), lambda t: (0, t, 0)),
        out_shape=jax.ShapeDtypeStruct((nb, seq, S5_WIDTH), BF16),
        scratch_shapes=[pltpu.VMEM((nb, SUBLANES, 2 * N_STATE), F32),
                        pltpu.VMEM((nb, SUBLANES, 2 * N_STATE), F32),
                        pltpu.VMEM((nb, tb, S5_WIDTH), BF16)]
        + [pltpu.VMEM((tb, 2 * N_STATE), F32) for _ in range(nb)],
        compiler_params=_cparams(("arbitrary",)),
        name="s5_mixer",
    )(u, u_meta, perm, perm.T, bm, cm, tab, ptab, d_skip, w_glu_bf16)


def _s5_params(lam_re, lam_im, log_dt, b_re, b_im, c_re, c_im):
    lam = lax.complex(lam_re.astype(F32), lam_im.astype(F32))
    dt = jnp.exp(log_dt.astype(F32))[:, None]
    lam_dt = lam * dt
    lam_bar = jnp.exp(lam_dt)
    b_bar = ((lam_bar - 1.0) / lam)[:, :, None] * lax.complex(b_re.astype(F32), b_im.astype(F32))
    gl = S5_GROUPS // 2
    eye = jnp.eye(gl, dtype=F32)

    def in_half(bh):
        def blk(part):
            t = jnp.einsum('gph,gk->ghkp', part, eye)
            return t.reshape(gl * S5_GROUP, gl * S5_STATE)
        return jnp.concatenate([blk(jnp.real(bh)), blk(jnp.imag(bh))], axis=1)

    def out_half(cr, ci):
        def blk(part):
            t = jnp.einsum('ghp,gk->gpkh', part, eye)
            return t.reshape(gl * S5_STATE, gl * S5_GROUP)
        return jnp.concatenate([blk(cr), blk(-ci)], axis=0)

    bm = jnp.stack([in_half(b_bar[:gl]), in_half(b_bar[gl:])]).astype(BF16)
    cm = jnp.stack([out_half(c_re[:gl].astype(F32), c_im[:gl].astype(F32)),
                    out_half(c_re[gl:].astype(F32), c_im[gl:].astype(F32))]).astype(BF16)
    lam_flat = lam_dt.reshape(1, N_STATE)
    rows = jnp.arange(SUBLANES, dtype=F32)[:, None]
    tabs = []
    for shift in (1, 2, 4):
        p = jnp.exp(lam_flat * float(SEG_LEN * shift)) * (rows >= shift).astype(F32)
        tabs += [jnp.real(p), jnp.imag(p)]
    for p in (jnp.exp(lam_flat * float(SEG_LEN)), jnp.exp(lam_flat)):
        tabs += [jnp.real(p), jnp.imag(p)]
    tab = jnp.stack([jnp.broadcast_to(t, (SUBLANES, N_STATE)) for t in tabs]).astype(F32)
    steps = (jnp.arange(SEG_LEN * SUBLANES) // SUBLANES + 1).astype(F32)[:, None]
    p = jnp.exp(lam_flat * steps)
    ptab = jnp.stack([jnp.real(p), jnp.imag(p)]).astype(F32)
    return bm, cm, tab, ptab


def _dot_nt(a, b):
    return lax.dot_general(a, b, (((1,), (1,)), ((), ())), preferred_element_type=F32)


def _dot_tn(a, b):
    return lax.dot_general(a, b, (((0,), (0,)), ((), ())), preferred_element_type=F32)


def _cumsum_rows(tri, x):
    hi = x.astype(BF16)
    r1 = x - hi.astype(F32)
    mid = r1.astype(BF16)
    lo = (r1 - mid.astype(F32)).astype(BF16)
    return (jnp.dot(tri, hi, preferred_element_type=F32)
            + jnp.dot(tri, mid, preferred_element_type=F32)
            + jnp.dot(tri, lo, preferred_element_type=F32))


def _segment_rows(x, seg, pos):
    n = x.shape[0]
    parts = [jnp.broadcast_to(x[s * seg + pos:s * seg + pos + 1, :], (seg, x.shape[1]))
             for s in range(n // seg)]
    return parts[0] if len(parts) == 1 else jnp.concatenate(parts, axis=0)


def _hg_gates(x, lb, row_valid=None):
    w = HG_WIDTH
    q = x[:, 0:w]
    f = lb + (1.0 - lb) * _sigmoid(x[:, w:2 * w])
    logf = jnp.log(f)
    k = 1.0 - f
    if row_valid is not None:
        logf = jnp.where(row_valid, logf, 0.0)
        k = jnp.where(row_valid, k, 0.0)
    return _silu(q), k, logf


def _hg_kernel(x_ref, xm_ref, lb_ref, gain_ref, tri_ref, y_ref, st_ref, *, nb, tb):
    w, dh = HG_WIDTH, HG_HEAD_DIM
    lb = lb_ref[...]
    tri = tri_ref[...]

    @pl.when(pl.program_id(0) == 0)
    def _():
        xm = xm_ref[...].astype(F32)
        valid = lax.broadcasted_iota(jnp.int32, (META_PAD, 1), 0) >= (META_PAD - N_META)
        _, k, logf = _hg_gates(xm, lb, valid)
        bc = _cumsum_rows(tri[0:META_PAD, 0:META_PAD], logf)
        ki = (k * jnp.exp(bc[META_PAD - 1:META_PAD, :] - bc)).astype(BF16)
        v = xm[:, 2 * w:3 * w].astype(BF16)
        for h in range(HG_HEADS):
            sl = slice(h * dh, (h + 1) * dh)
            s0 = _dot_tn(v[:, sl], ki[:, sl])
            for b in range(nb):
                st_ref[b, h] = s0

    ri = lax.broadcasted_iota(jnp.int32, (tb, tb), 0)
    ci = lax.broadcasted_iota(jnp.int32, (tb, tb), 1)
    diag_shift = HG_DIAG.bit_length() - 1
    diag_mask = ((ri >> diag_shift) == (ci >> diag_shift)) & (ci <= ri)
    levels = []
    seg = 2 * HG_DIAG
    while seg <= tb:
        levels.append(seg)
        seg *= 2
    seg_masks = [None if s == tb else
                 ((ri >> (s.bit_length() - 1)) == (ci >> (s.bit_length() - 1))).astype(F32)
                 for s in levels]
    row = lax.broadcasted_iota(jnp.int32, (tb, 1), 0)

    for b in range(nb):
        x = x_ref[b].astype(F32)
        qs, k, logf = _hg_gates(x, lb)
        v = x[:, 2 * w:3 * w].astype(BF16)
        g = x[:, 3 * w:4 * w]
        bc = _cumsum_rows(tri, logf)
        dlt = bc - _segment_rows(bc, HG_DIAG, HG_DIAG // 2 - 1)
        qk = [((qs * jnp.exp(dlt)).astype(BF16), (k * jnp.exp(-dlt)).astype(BF16))]
        for s in levels:
            e = jnp.exp(-jnp.abs(bc - _segment_rows(bc, s, s // 2 - 1)))
            upper = (row & (s - 1)) >= (s // 2)
            qk.append((jnp.where(upper, qs * e, 0.0).astype(BF16),
                       jnp.where(upper, 0.0, k * e).astype(BF16)))
        b_last = bc[tb - 1:tb, :]
        qi = (qs * jnp.exp(bc)).astype(BF16)
        ki = (k * jnp.exp(b_last - bc)).astype(BF16)
        dec = jnp.exp(b_last)
        outs = []
        for h in range(HG_HEADS):
            sl = slice(h * dh, (h + 1) * dh)
            sc = jnp.where(diag_mask, _dot_nt(qk[0][0][:, sl], qk[0][1][:, sl]), 0.0)
            for (ql, kl), m in zip(qk[1:], seg_masks):
                t = _dot_nt(ql[:, sl], kl[:, sl])
                sc = sc + (t if m is None else t * m)
            st = st_ref[b, h]
            o = (jnp.dot(sc.astype(BF16), v[:, sl], preferred_element_type=F32)
                 + _dot_nt(qi[:, sl], st.astype(BF16)))
            st_ref[b, h] = dec[:, sl] * st + _dot_tn(v[:, sl], ki[:, sl])
            ms = jnp.mean(o * o, axis=-1, keepdims=True)
            outs.append(o * lax.rsqrt(ms + EPS))
        o = jnp.concatenate(outs, axis=1) * gain_ref[...]
        y_ref[b] = (o * _silu(g)).astype(BF16)


def _hg_mixer(x, x_meta, lb, gain, tri):
    nb, seq, _ = x.shape
    tb = TIME_BLOCK
    kern = functools.partial(_hg_kernel, nb=nb, tb=tb)
    const2 = lambda t: (0, 0)
    return pl.pallas_call(
        kern,
        grid=(seq // tb,),
        in_specs=[
            pl.BlockSpec((nb, tb, 4 * HG_WIDTH), lambda t: (0, t, 0)),
            pl.BlockSpec(x_meta.shape, const2),
            pl.BlockSpec(lb.shape, const2),
            pl.BlockSpec(gain.shape, const2),
            pl.BlockSpec(tri.shape, const2),
        ],
        out_specs=pl.BlockSpec((nb, tb, HG_WIDTH), lambda t: (0, t, 0)),
        out_shape=jax.ShapeDtypeStruct((nb, seq, HG_WIDTH), BF16),
        scratch_shapes=[pltpu.VMEM((nb, HG_HEADS, HG_HEAD_DIM, HG_HEAD_DIM), F32)],
        compiler_params=_cparams(("arbitrary",)),
        name="hgrn2_mixer",
    )(x, x_meta, lb, gain, tri)


def _split3(x):
    hi = x.astype(BF16)
    lo = (x - hi.astype(F32)).astype(BF16)
    return hi, lo


def _pack_pairs(y):
    n = y.shape[1] // 2
    lo = pltpu.bitcast(y[:, :n].astype(BF16).astype(F32), jnp.uint32)
    hi = pltpu.bitcast(y[:, n:].astype(BF16).astype(F32), jnp.uint32)
    return (lo >> 16) | (hi & jnp.uint32(0xFFFF0000))


def _unpack_pairs(p):
    lo = pltpu.bitcast(p << 16, F32)
    hi = pltpu.bitcast(p & jnp.uint32(0xFFFF0000), F32)
    return lo, hi


def _merge_kernel(x_ref, ys_ref, yh_ref, gt_ref, wbs_ref, wbh_ref, wo_ref, g2_ref, wr_hi_ref,
                  wr_lo_ref, br_ref, tri_ref, h2_ref, xp_ref, eid_ref, wt_ref, rank_ref, cnt_ref,
                  run_ref):
    @pl.when(pl.program_id(0) == 0)
    def _():
        run_ref[...] = jnp.zeros_like(run_ref)

    gt = gt_ref[...].astype(F32)
    gs = _sigmoid(gt[:, :D_MODEL])
    gh = _sigmoid(gt[:, D_MODEL:])
    merged = (gs * jnp.dot(ys_ref[...], wbs_ref[...], preferred_element_type=F32)
              + gh * jnp.dot(yh_ref[...], wbh_ref[...], preferred_element_type=F32))
    h2 = x_ref[...] + jnp.dot(merged.astype(BF16), wo_ref[...], preferred_element_type=F32)
    h2_ref[...] = h2
    ms = jnp.mean(h2 * h2, axis=-1, keepdims=True)
    xn = h2 * lax.rsqrt(ms + EPS) * g2_ref[...]
    xp_ref[...] = _pack_pairs(xn)

    x_hi, x_lo = _split3(xn)
    logits = (jnp.dot(x_hi, wr_hi_ref[...], preferred_element_type=F32)
              + jnp.dot(x_hi, wr_lo_ref[...], preferred_element_type=F32)
              + jnp.dot(x_lo, wr_hi_ref[...], preferred_element_type=F32)) + br_ref[...]
    rows = logits.shape[0]
    lane = lax.broadcasted_iota(jnp.int32, (rows, LANES), 1)
    tops, hots = [], []
    sel = jnp.zeros((rows, LANES), F32)
    for k in range(TOP_K):
        m = jnp.max(logits, axis=-1, keepdims=True)
        idx = jnp.min(jnp.where(logits == m, lane, LANES), axis=-1, keepdims=True)
        hot = lane == idx
        logits = jnp.where(hot, NEG_BIG, logits)
        tops.append(m)
        hots.append(hot)
        sel = sel + hot.astype(F32)
        eid_ref[:, k:k + 1] = idx
    es = [jnp.exp(m - tops[0]) for m in tops]
    tot = es[0] + es[1] + es[2] + es[3]
    for k in range(TOP_K):
        wt_ref[:, k:k + 1] = es[k] / tot
    prefix = jnp.dot(tri_ref[...], sel.astype(BF16), preferred_element_type=F32) + run_ref[...]
    for k in range(TOP_K):
        r = jnp.sum(jnp.where(hots[k], prefix, 0.0), axis=-1, keepdims=True)
        rank_ref[:, k:k + 1] = r.astype(jnp.int32)
    run = run_ref[...] + jnp.sum(sel, axis=0, keepdims=True)
    run_ref[...] = run
    cnt_ref[...] = run


def _merge_router(x_rows, y_s5, y_hg, gates, wbs, wbh, wo, g2, wr_hi, wr_lo, br, tri_strict):
    n = x_rows.shape[0]
    rb = ROW_BLOCK
    const2 = lambda i: (0, 0)
    rowblk = lambda width: pl.BlockSpec((rb, width), lambda i: (i, 0))
    return pl.pallas_call(
        _merge_kernel,
        grid=(n // rb,),
        in_specs=[
            rowblk(D_MODEL), rowblk(S5_WIDTH), rowblk(HG_WIDTH), rowblk(2 * D_MODEL),
            pl.BlockSpec(wbs.shape, const2), pl.BlockSpec(wbh.shape, const2),
            pl.BlockSpec(wo.shape, const2), pl.BlockSpec(g2.shape, const2),
            pl.BlockSpec(wr_hi.shape, const2), pl.BlockSpec(wr_lo.shape, const2),
            pl.BlockSpec(br.shape, const2), pl.BlockSpec(tri_strict.shape, const2),
        ],
        out_specs=[
            rowblk(D_MODEL), rowblk(D_MODEL // 2), rowblk(TOP_K), rowblk(TOP_K), rowblk(TOP_K),
            pl.BlockSpec((1, LANES), const2),
        ],
        out_shape=[
            jax.ShapeDtypeStruct((n, D_MODEL), F32),
            jax.ShapeDtypeStruct((n, D_MODEL // 2), jnp.uint32),
            jax.ShapeDtypeStruct((n, TOP_K), jnp.int32),
            jax.ShapeDtypeStruct((n, TOP_K), F32),
            jax.ShapeDtypeStruct((n, TOP_K), jnp.int32),
            jax.ShapeDtypeStruct((1, LANES), F32),
        ],
        scratch_shapes=[pltpu.VMEM((1, LANES), F32)],
        compiler_params=_cparams(("arbitrary",)),
        name="merge_router",
    )(x_rows, y_s5, y_hg, gates, wbs, wbh, wo, g2, wr_hi, wr_lo, br, tri_strict)


def _gather_rows(src, idx):
    n = idx.shape[0]
    width = src.shape[1]
    assert n % GATHER_ROWS == 0
    mesh = plsc.VectorSubcoreMesh(core_axis_name="core", subcore_axis_name="subcore")

    @functools.partial(pl.kernel, out_type=jax.ShapeDtypeStruct((n, width), src.dtype),
                       mesh=mesh, scratch_types=[], name="sc_gather_rows")
    def gather(src_hbm, idx_hbm, out_hbm):
        def body(idx_vmem, out_vmem):
            off = pl.multiple_of(pl.program_id(1) * GATHER_WINDOW, GATHER_WINDOW)
            pltpu.sync_copy(src_hbm.at[idx_vmem.at[0, pl.ds(off, GATHER_WINDOW)]], out_vmem)

        pltpu.emit_pipeline(
            body,
            grid=(n // INDEX_BLOCK, INDEX_SPLIT),
            in_specs=[pl.BlockSpec((1, INDEX_BLOCK), lambda i, j: (0, i))],
            out_specs=[pl.BlockSpec((GATHER_WINDOW, width), lambda i, j: (INDEX_SPLIT * i + j, 0))],
            core_axis_name=("core", "subcore"),
            dimension_semantics=(pltpu.PARALLEL, pltpu.ARBITRARY),
        )(idx_hbm, out_hbm)

    return gather(src, idx.reshape(1, n))


def _scatter_rows(src, dest, n_out):
    n_src, width = src.shape
    n = dest.shape[0]
    assert n == TOP_K * n_src and n % GATHER_ROWS == 0 and n_src % GATHER_WINDOW == 0
    steps_per_copy = n_src // GATHER_WINDOW
    mesh = plsc.VectorSubcoreMesh(core_axis_name="core", subcore_axis_name="subcore")

    @functools.partial(pl.kernel, out_type=jax.ShapeDtypeStruct((n_out, width), src.dtype),
                       mesh=mesh, scratch_types=[], name="sc_scatter_rows")
    def scatter(src_hbm, idx_hbm, out_hbm):
        def body(src_vmem, idx_vmem):
            off = pl.multiple_of(pl.program_id(1) * GATHER_WINDOW, GATHER_WINDOW)
            pltpu.sync_copy(src_vmem, out_hbm.at[idx_vmem.at[0, pl.ds(off, GATHER_WINDOW)]])

        pltpu.emit_pipeline(
            body,
            grid=(n // INDEX_BLOCK, INDEX_SPLIT),
            in_specs=[pl.BlockSpec((GATHER_WINDOW, width),
                                   lambda i, j: ((INDEX_SPLIT * i + j) % steps_per_copy, 0)),
                      pl.BlockSpec((1, INDEX_BLOCK), lambda i, j: (0, i))],
            out_specs=[],
            core_axis_name=("core", "subcore"),
            dimension_semantics=(pltpu.PARALLEL, pltpu.ARBITRARY),
        )(src_hbm, idx_hbm)

    return scatter(src, dest.reshape(1, n))


GU_GROUP = 512


def _expert_kernel(be_ref, nv_ref, nr_ref, x_ref, wgu_ref, bgu_ref, wd_ref, bd_ref, perm_ref,
                   y_ref, wgu_s, wd_s):
    i = pl.program_id(0)
    prev = be_ref[jnp.maximum(i - 1, 0)]
    fresh = (i == 0) | (be_ref[i] != prev)

    @pl.when(fresh & (i < nv_ref[0]))
    def _():
        for c in range(2 * D_EXPERT // GU_GROUP):
            cols = slice(c * GU_GROUP, (c + 1) * GU_GROUP)
            w = wgu_ref[0, :, cols].astype(BF16)
            wgu_s[:, cols] = jnp.dot(w, perm_ref[...], preferred_element_type=F32).astype(BF16)
        wd_s[...] = wd_ref[0].astype(BF16)

    @pl.when(i < nv_ref[0])
    def _():
        half = D_MODEL // 2
        live = lax.broadcasted_iota(jnp.int32, (MOE_BLOCK, 1), 0) < nr_ref[i]
        xa, xb = _unpack_pairs(jnp.where(live, x_ref[...], jnp.uint32(0)))
        x = jnp.concatenate([xa.astype(BF16), xb.astype(BF16)], axis=1)
        hw = GU_GROUP // 2
        n_groups = 2 * D_EXPERT // GU_GROUP

        def gate_up(c):
            return jnp.dot(x, wgu_s[:, c * GU_GROUP:(c + 1) * GU_GROUP],
                           preferred_element_type=F32)

        acc = None
        hcols = []
        gu_next = gate_up(0)
        for c in range(n_groups):
            gu = gu_next + bgu_ref[0, :, c * GU_GROUP:(c + 1) * GU_GROUP]
            if c + 1 < n_groups:
                gu_next = gate_up(c + 1)
            gate = jnp.minimum(gu[:, :hw], SWIGLU_LIMIT)
            up = jnp.clip(gu[:, hw:], -SWIGLU_LIMIT, SWIGLU_LIMIT)
            hcols.append(((up + 1.0) * (gate * _sigmoid(gate * SWIGLU_ALPHA))).astype(BF16))
            if c % 2 == 1:
                part = jnp.dot(jnp.concatenate(hcols[c - 1:c + 1], axis=1),
                               wd_s[(c - 1) * hw:(c + 1) * hw, :], preferred_element_type=F32)
                acc = part if acc is None else acc + part
        y_ref[...] = _pack_pairs(acc + bd_ref[0])

    @pl.when(i >= nv_ref[0])
    def _():
        y_ref[...] = jnp.zeros_like(y_ref)


def _experts(block_e, n_valid, block_rows, x_rows, w_gate_up, b_gu_grouped, w_down, b_down, perm):
    n_rows = x_rows.shape[0]
    n_blocks = n_rows // MOE_BLOCK
    half = D_MODEL // 2
    by_expert = lambda i, be, nv, nr: (be[i], 0, 0)
    grid_spec = pltpu.PrefetchScalarGridSpec(
        num_scalar_prefetch=3,
        grid=(n_blocks,),
        in_specs=[
            pl.BlockSpec((MOE_BLOCK, half), lambda i, be, nv, nr: (i, 0)),
            pl.BlockSpec((1, D_MODEL, 2 * D_EXPERT), by_expert),
            pl.BlockSpec((1, 1, 2 * D_EXPERT), by_expert),
            pl.BlockSpec((1, D_EXPERT, D_MODEL), by_expert),
            pl.BlockSpec((1, 1, D_MODEL), by_expert),
            pl.BlockSpec(perm.shape, lambda i, be, nv, nr: (0, 0)),
        ],
        out_specs=pl.BlockSpec((MOE_BLOCK, half), lambda i, be, nv, nr: (i, 0)),
        scratch_shapes=[
            pltpu.VMEM((D_MODEL, 2 * D_EXPERT), BF16),
            pltpu.VMEM((D_EXPERT, D_MODEL), BF16),
        ],
    )
    return pl.pallas_call(
        _expert_kernel,
        grid_spec=grid_spec,
        out_shape=jax.ShapeDtypeStruct((n_rows, half), jnp.uint32),
        compiler_params=_cparams(("arbitrary",)),
        name="experts",
    )(block_e, n_valid, block_rows, x_rows, w_gate_up, b_gu_grouped, w_down, b_down, perm)


def _combine_kernel(h2_ref, y0_ref, y1_ref, y2_ref, y3_ref, wt_ref, gf_ref, out_ref):
    half = D_MODEL // 2
    wt = wt_ref[...]
    lo = jnp.zeros((h2_ref.shape[0], half), F32)
    hi = jnp.zeros((h2_ref.shape[0], half), F32)
    for k, yk_ref in enumerate((y0_ref, y1_ref, y2_ref, y3_ref)):
        a, b = _unpack_pairs(yk_ref[...])
        lo = lo + wt[:, k:k + 1] * a
        hi = hi + wt[:, k:k + 1] * b
    y = h2_ref[...] + jnp.concatenate([lo, hi], axis=1)
    ms = jnp.mean(y * y, axis=-1, keepdims=True)
    out_ref[...] = y * lax.rsqrt(ms + EPS) * gf_ref[...]


def _combine(h2, y_tok, wts, gain):
    n = h2.shape[0]
    rb = ROW_BLOCK
    rowblk = lambda width: pl.BlockSpec((rb, width), lambda i: (i, 0))
    steps = n // rb
    choice = lambda k: pl.BlockSpec((rb, D_MODEL // 2), lambda i: (k * steps + i, 0))
    return pl.pallas_call(
        _combine_kernel,
        grid=(steps,),
        in_specs=[rowblk(D_MODEL)] + [choice(k) for k in range(TOP_K)]
        + [rowblk(TOP_K), pl.BlockSpec((1, D_MODEL), lambda i: (0, 0))],
        out_specs=rowblk(D_MODEL),
        out_shape=jax.ShapeDtypeStruct((n, D_MODEL), F32),
        compiler_params=_cparams(("parallel",)),
        name="combine_norm",
    )(h2, y_tok, y_tok, y_tok, y_tok, wts, gain)


def _lower_tri(n, strict):
    r = lax.broadcasted_iota(jnp.int32, (n, n), 0)
    c = lax.broadcasted_iota(jnp.int32, (n, n), 1)
    return ((c < r) if strict else (c <= r)).astype(BF16)


def kernel(x, meta_tokens, norm1_gain, w_in, s5_lambda_re, s5_lambda_im, s5_log_dt, s5_b_re,
           s5_b_im, s5_c_re, s5_c_im, s5_d, s5_w_glu, hgrn_lb_logits, hgrn_norm_gain,
           w_branch_s5, w_branch_hgrn, w_out, norm2_gain, w_router, b_router, w_gate_up,
           b_gate_up, w_down, b_down, final_norm_gain):
    nb, seq, d = x.shape
    n_tok = nb * seq
    assert d == D_MODEL and seq % TIME_BLOCK == 0 and n_tok % ROW_BLOCK == 0
    x_rows = x.reshape(n_tok, d)
    w_in_b = w_in[0].astype(BF16)
    g1 = norm1_gain[0].reshape(1, d).astype(F32)

    u, qfvg, gates = _inproj(x_rows, g1, w_in_b, ROW_BLOCK)
    meta_rows = jnp.concatenate(
        [jnp.zeros((META_PAD - N_META, d), F32), meta_tokens.astype(F32)], axis=0)
    u_m, qfvg_m, _ = _inproj(meta_rows, g1, w_in_b, META_PAD)

    bm, cm, tab, ptab = _s5_params(s5_lambda_re[0], s5_lambda_im[0], s5_log_dt[0], s5_b_re[0],
                                   s5_b_im[0], s5_c_re[0], s5_c_im[0])
    y_s5 = _s5_mixer(u.reshape(nb, seq, S5_WIDTH), u_m, bm, cm, tab, ptab,
                     s5_d[0].reshape(1, S5_WIDTH).astype(F32), s5_w_glu[0].astype(BF16))

    lower_bounds = jnp.cumsum(jax.nn.softmax(hgrn_lb_logits.astype(F32), axis=0), axis=0)
    lb = lower_bounds[0].reshape(1, HG_WIDTH)
    y_hg = _hg_mixer(qfvg.reshape(nb, seq, 4 * HG_WIDTH), qfvg_m, lb,
                     hgrn_norm_gain[0].reshape(1, HG_WIDTH).astype(F32),
                     _lower_tri(TIME_BLOCK, strict=False))

    wr = jnp.zeros((d, LANES), F32).at[:, :N_EXPERTS].set(w_router[0].astype(F32))
    wr_hi = wr.astype(BF16)
    wr_lo = (wr - wr_hi.astype(F32)).astype(BF16)
    br = jnp.full((1, LANES), NEG_BIG, F32).at[0, :N_EXPERTS].set(b_router[0].astype(F32))
    h2, xn_packed, eid, wts, rank, counts = _merge_router(
        x_rows, y_s5.reshape(n_tok, S5_WIDTH), y_hg.reshape(n_tok, HG_WIDTH), gates,
        w_branch_s5[0].astype(BF16), w_branch_hgrn[0].astype(BF16), w_out[0].astype(BF16),
        norm2_gain[0].reshape(1, d).astype(F32), wr_hi, wr_lo, br,
        _lower_tri(ROW_BLOCK, strict=True))

    n_assign = n_tok * TOP_K
    n_blocks = n_assign // MOE_BLOCK + N_EXPERTS
    n_rows = n_blocks * MOE_BLOCK
    n_rows_pad = -(-n_rows // GATHER_ROWS) * GATHER_ROWS
    cnt = counts[0, :N_EXPERTS].astype(jnp.int32)
    padded = (cnt + MOE_BLOCK - 1) // MOE_BLOCK * MOE_BLOCK
    padded_end = jnp.cumsum(padded)
    padded_start = padded_end - padded
    onehot = (eid[:, :, None] == jnp.arange(N_EXPERTS, dtype=jnp.int32)).astype(jnp.int32)
    dest = (jnp.sum(onehot * padded_start, axis=-1) + rank).T.reshape(-1)
    block_start = jnp.arange(n_rows_pad // MOE_BLOCK, dtype=jnp.int32) * MOE_BLOCK
    block_e = jnp.minimum(jnp.sum((block_start[:, None] >= padded_end[None, :]).astype(jnp.int32),
                                  axis=1), N_EXPERTS - 1)
    block_rows = jnp.clip(cnt[block_e] - (block_start - padded_start[block_e]), 0, MOE_BLOCK)
    n_valid = (padded_end[-1] // MOE_BLOCK).astype(jnp.int32).reshape(1)

    x_sorted = _scatter_rows(xn_packed, dest, n_rows_pad)
    hw = GU_GROUP // 2
    pr = jnp.arange(GU_GROUP)
    src = jnp.where(pr < hw, 2 * pr, 2 * (pr - hw) + 1)
    perm = (jnp.arange(GU_GROUP)[:, None] == src[None, :]).astype(BF16)
    b_gu = b_gate_up[0].astype(F32).reshape(N_EXPERTS, 2 * D_EXPERT // GU_GROUP, hw, 2)
    b_gu = b_gu.transpose(0, 1, 3, 2).reshape(N_EXPERTS, 1, 2 * D_EXPERT)
    y_sorted = _experts(block_e, n_valid, block_rows, x_sorted, w_gate_up[0], b_gu, w_down[0],
                        b_down[0].astype(F32).reshape(N_EXPERTS, 1, d), perm)
    y_tok = _gather_rows(y_sorted, dest)

    out = _combine(h2, y_tok, wts, final_norm_gain.reshape(1, d).astype(F32))
    return out.reshape(nb, seq, d)
```

```python
import functools

import jax
import jax.numpy as jnp
from jax import lax
from jax.experimental import pallas as pl
from jax.experimental.pallas import tpu as pltpu
from jax.experimental.pallas import tpu_sc as plsc

F32 = jnp.float32
BF16 = jnp.bfloat16

D_MODEL = 1024
N_META = 16
S5_GROUP = 16
S5_GROUPS = 32
S5_WIDTH = 512
S5_STATE = 64
HG_HEADS = 4
HG_HEAD_DIM = 128
HG_WIDTH = 512
N_EXPERTS = 32
TOP_K = 4
D_EXPERT = 1024
SWIGLU_ALPHA = 1.702
SWIGLU_LIMIT = 7.0
EPS = 1e-6

LANES = 128
SUBLANES = 8
TIME_BLOCK = 256
META_PAD = TIME_BLOCK
ROW_BLOCK = 512
MOE_BLOCK = 512
GATHER_WINDOW = 64
INDEX_BLOCK = 128
INDEX_SPLIT = INDEX_BLOCK // GATHER_WINDOW
SC_SUBCORES = 32
GATHER_ROWS = INDEX_BLOCK * SC_SUBCORES
HG_DIAG = 32
NEG_BIG = -1e30
VMEM_LIMIT = 56 * 1024 * 1024


def _cparams(sem):
    return pltpu.CompilerParams(dimension_semantics=sem, vmem_limit_bytes=VMEM_LIMIT)


def _sigmoid(x):
    return 0.5 * jnp.tanh(0.5 * x) + 0.5


def _silu(x):
    h = 0.5 * x
    return h + h * jnp.tanh(h)


def _inproj_kernel(x_ref, g_ref, w_ref, u_ref, qfvg_ref, gates_ref):
    x = x_ref[...]
    ms = jnp.mean(x * x, axis=-1, keepdims=True)
    xn = (x * lax.rsqrt(ms + EPS) * g_ref[...]).astype(BF16)
    u_ref[...] = jnp.dot(xn, w_ref[:, 0:S5_WIDTH], preferred_element_type=F32).astype(BF16)
    a, b = S5_WIDTH, S5_WIDTH + 4 * HG_WIDTH
    qfvg_ref[...] = jnp.dot(xn, w_ref[:, a:b], preferred_element_type=F32).astype(BF16)
    gates_ref[...] = jnp.dot(xn, w_ref[:, b:], preferred_element_type=F32).astype(BF16)


def _inproj(rows, gain, w_in_bf16, row_block):
    n = rows.shape[0]
    n_gate = 2 * D_MODEL
    return pl.pallas_call(
        _inproj_kernel,
        grid=(n // row_block,),
        in_specs=[
            pl.BlockSpec((row_block, D_MODEL), lambda i: (i, 0)),
            pl.BlockSpec((1, D_MODEL), lambda i: (0, 0)),
            pl.BlockSpec(w_in_bf16.shape, lambda i: (0, 0)),
        ],
        out_specs=[
            pl.BlockSpec((row_block, S5_WIDTH), lambda i: (i, 0)),
            pl.BlockSpec((row_block, 4 * HG_WIDTH), lambda i: (i, 0)),
            pl.BlockSpec((row_block, n_gate), lambda i: (i, 0)),
        ],
        out_shape=[
            jax.ShapeDtypeStruct((n, S5_WIDTH), BF16),
            jax.ShapeDtypeStruct((n, 4 * HG_WIDTH), BF16),
            jax.ShapeDtypeStruct((n, n_gate), BF16),
        ],
        compiler_params=_cparams(("parallel",)),
        name="inproj",
    )(rows, gain, w_in_bf16)


N_STATE = S5_GROUPS * S5_STATE
HALF_STATE = N_STATE // 2
HALF_COLS = 2 * HALF_STATE


SEG_COUNT = SUBLANES
SEG_LEN = TIME_BLOCK // SEG_COUNT
SCAN_CHAINS = 4
SCAN_BATCH_GROUP = 2


def _state_cols(j):
    half, m0 = divmod(j * LANES, HALF_STATE)
    cre = half * HALF_COLS + m0
    return cre, cre + HALF_STATE, j * LANES


def _s5_local_scan(chains, tab_ref, car_ref, cs_ref):
    sub = lax.broadcasted_iota(jnp.int32, (SUBLANES, LANES), 0)
    zero = jnp.zeros((SUBLANES, LANES), F32)
    lane_groups = max(1, SCAN_CHAINS // len(chains))
    for j0 in range(0, N_STATE // LANES, lane_groups):
        groups = [_state_cols(j) for j in range(j0, j0 + lane_groups)]
        cols = [(ref, cre, cim) for cre, cim, _ in groups for ref, _ in chains]
        coef = [(tab_ref[8, :, n0:n0 + LANES], tab_ref[9, :, n0:n0 + LANES])
                for _, _, n0 in groups for _ in chains]
        state = [(zero, zero)] * len(cols)
        for tau in range(SEG_LEN):
            rows = slice(tau * SUBLANES, (tau + 1) * SUBLANES)
            vals = [(ref[rows, cre:cre + LANES], ref[rows, cim:cim + LANES])
                    for ref, cre, cim in cols]
            state = [(lr * xr - li * xi + br, lr * xi + li * xr + bi)
                     for (lr, li), (xr, xi), (br, bi) in zip(coef, state, vals)]
            for (ref, cre, cim), (xr, xi) in zip(cols, state):
                ref[rows, cre:cre + LANES] = xr
                ref[rows, cim:cim + LANES] = xi
        k = 0
        for cre, cim, n0 in groups:
            m = [tab_ref[i, :, n0:n0 + LANES] for i in range(8)]
            for _, c in chains:
                er, ei = state[k]
                k += 1
                gr = jnp.where(sub == 0, car_ref[c, :, cre:cre + LANES], pltpu.roll(er, 1, 0))
                gi = jnp.where(sub == 0, car_ref[c, :, cim:cim + LANES], pltpu.roll(ei, 1, 0))
                for q, shift in enumerate((1, 2, 4)):
                    ar, ai = m[2 * q], m[2 * q + 1]
                    rr = pltpu.roll(gr, shift, 0)
                    ri = pltpu.roll(gi, shift, 0)
                    gr, gi = gr + ar * rr - ai * ri, gi + ar * ri + ai * rr
                cs_ref[c, :, cre:cre + LANES] = gr
                cs_ref[c, :, cim:cim + LANES] = gi
                xr = m[6] * gr - m[7] * gi + er
                xi = m[6] * gi + m[7] * gr + ei
                car_ref[c, :, cre:cre + LANES] = jnp.broadcast_to(
                    xr[SUBLANES - 1:SUBLANES, :], (SUBLANES, LANES))
                car_ref[c, :, cim:cim + LANES] = jnp.broadcast_to(
                    xi[SUBLANES - 1:SUBLANES, :], (SUBLANES, LANES))


def _s5_add_start_states(chains, ptab_ref, cs_ref):
    for j in range(N_STATE // LANES):
        cre, cim, n0 = _state_cols(j)
        starts = [(cs_ref[c, :, cre:cre + LANES], cs_ref[c, :, cim:cim + LANES]) for _, c in chains]
        for tau in range(SEG_LEN):
            rows = slice(tau * SUBLANES, (tau + 1) * SUBLANES)
            pr = ptab_ref[0, rows, n0:n0 + LANES]
            pi = ptab_ref[1, rows, n0:n0 + LANES]
            vals = [(ref[rows, cre:cre + LANES], ref[rows, cim:cim + LANES]) for ref, _ in chains]
            for (ref, _), (xr, xi), (cr, ci) in zip(chains, vals, starts):
                ref[rows, cre:cre + LANES] = xr + pr * cr - pi * ci
                ref[rows, cim:cim + LANES] = xi + pr * ci + pi * cr


def _s5_kernel(u_ref, um_ref, perm_ref, unperm_ref, bm_ref, cm_ref, tab_ref, ptab_ref, d_ref,
               wglu_ref, y_ref, car_ref, cs_ref, up_ref, *xs_refs, nb):
    half_ch = S5_WIDTH // 2

    def project_in(u, c):
        up = jnp.dot(perm_ref[...], u, preferred_element_type=F32).astype(BF16)
        up_ref[c] = up
        for hf in range(2):
            xs_refs[c][:, hf * HALF_COLS:(hf + 1) * HALF_COLS] = jnp.dot(
                up[:, hf * half_ch:(hf + 1) * half_ch], bm_ref[hf], preferred_element_type=F32)

    @pl.when(pl.program_id(0) == 0)
    def _():
        car_ref[...] = jnp.zeros_like(car_ref)
        project_in(um_ref[...], 0)
        _s5_local_scan([(xs_refs[0], 0)], tab_ref, car_ref, cs_ref)
        for b in range(1, nb):
            car_ref[b] = car_ref[0]

    def project_out(b):
        ys = [jnp.dot(xs_refs[b][:, hf * HALF_COLS:(hf + 1) * HALF_COLS].astype(BF16), cm_ref[hf],
                      preferred_element_type=F32) for hf in range(2)]
        y = jnp.concatenate(ys, axis=1) + d_ref[...] * up_ref[b].astype(F32)
        y = 0.5 * y * (1.0 + jnp.tanh(0.7978845608028654 * (y + 0.044715 * (y * y * y))))
        z = jnp.dot(y.astype(BF16), wglu_ref[...], preferred_element_type=F32)
        out = (y * _sigmoid(z)).astype(BF16)
        y_ref[b] = jnp.dot(unperm_ref[...], out, preferred_element_type=F32).astype(BF16)

    for b in range(nb):
        project_in(u_ref[b], b)
    for b0 in range(0, nb, SCAN_BATCH_GROUP):
        group = range(b0, min(nb, b0 + SCAN_BATCH_GROUP))
        chains = [(xs_refs[b], b) for b in group]
        _s5_local_scan(chains, tab_ref, car_ref, cs_ref)
        _s5_add_start_states(chains, ptab_ref, cs_ref)
        for b in group:
            project_out(b)


def _s5_mixer(u, u_meta, bm, cm, tab, ptab, d_skip, w_glu_bf16):
    nb, seq, _ = u.shape
    tb = TIME_BLOCK
    r = jnp.arange(tb)
    perm = (((r % SEG_COUNT) * SEG_LEN + r // SEG_COUNT)[:, None] == r[None, :]).astype(BF16)
    kern = functools.partial(_s5_kernel, nb=nb)
    const2 = lambda t: (0, 0)
    const3 = lambda t: (0, 0, 0)
    return pl.pallas_call(
        kern,
        grid=(seq // tb,),
        in_specs=[
            pl.BlockSpec((nb, tb, S5_WIDTH), lambda t: (0, t, 0)),
            pl.BlockSpec(u_meta.shape, const2),
            pl.BlockSpec(perm.shape, const2),
            pl.BlockSpec(perm.shape, const2),
            pl.BlockSpec(bm.shape, const3),
            pl.BlockSpec(cm.shape, const3),
            pl.BlockSpec(tab.shape, const3),
            pl.BlockSpec(ptab.shape, const3),
            pl.BlockSpec(d_skip.shape, const2),
            pl.BlockSpec(w_glu_bf16.shape, const2),
        ],
        out_specs=pl.BlockSpec((nb, tb, S5_WIDTH), lambda t: (0, t, 0)),
        out_shape=jax.ShapeDtypeStruct((nb, seq, S5_WIDTH), BF16),
        scratch_shapes=[pltpu.VMEM((nb, SUBLANES, 2 * N_STATE), F32),
                        pltpu.VMEM((nb, SUBLANES, 2 * N_STATE), F32),
                        pltpu.VMEM((nb, tb, S5_WIDTH), BF16)]
        + [pltpu.VMEM((tb, 2 * N_STATE), F32) for _ in range(nb)],
        compiler_params=_cparams(("arbitrary",)),
        name="s5_mixer",
    )(u, u_meta, perm, perm.T, bm, cm, tab, ptab, d_skip, w_glu_bf16)


def _s5_params(lam_re, lam_im, log_dt, b_re, b_im, c_re, c_im):
    lam = lax.complex(lam_re.astype(F32), lam_im.astype(F32))
    dt = jnp.exp(log_dt.astype(F32))[:, None]
    lam_dt = lam * dt
    lam_bar = jnp.exp(lam_dt)
    b_bar = ((lam_bar - 1.0) / lam)[:, :, None] * lax.complex(b_re.astype(F32), b_im.astype(F32))
    gl = S5_GROUPS // 2
    eye = jnp.eye(gl, dtype=F32)

    def in_half(bh):
        def blk(part):
            t = jnp.einsum('gph,gk->ghkp', part, eye)
            return t.reshape(gl * S5_GROUP, gl * S5_STATE)
        return jnp.concatenate([blk(jnp.real(bh)), blk(jnp.imag(bh))], axis=1)

    def out_half(cr, ci):
        def blk(part):
            t = jnp.einsum('ghp,gk->gpkh', part, eye)
            return t.reshape(gl * S5_STATE, gl * S5_GROUP)
        return jnp.concatenate([blk(cr), blk(-ci)], axis=0)

    bm = jnp.stack([in_half(b_bar[:gl]), in_half(b_bar[gl:])]).astype(BF16)
    cm = jnp.stack([out_half(c_re[:gl].astype(F32), c_im[:gl].astype(F32)),
                    out_half(c_re[gl:].astype(F32), c_im[gl:].astype(F32))]).astype(BF16)
    lam_flat = lam_dt.reshape(1, N_STATE)
    rows = jnp.arange(SUBLANES, dtype=F32)[:, None]
    tabs = []
    for shift in (1, 2, 4):
        p = jnp.exp(lam_flat * float(SEG_LEN * shift)) * (rows >= shift).astype(F32)
        tabs += [jnp.real(p), jnp.imag(p)]
    for p in (jnp.exp(lam_flat * float(SEG_LEN)), jnp.exp(lam_flat)):
        tabs += [jnp.real(p), jnp.imag(p)]
    tab = jnp.stack([jnp.broadcast_to(t, (SUBLANES, N_STATE)) for t in tabs]).astype(F32)
    steps = (jnp.arange(SEG_LEN * SUBLANES) // SUBLANES + 1).astype(F32)[:, None]
    p = jnp.exp(lam_flat * steps)
    ptab = jnp.stack([jnp.real(p), jnp.imag(p)]).astype(F32)
    return bm, cm, tab, ptab


def _dot_nt(a, b):
    return lax.dot_general(a, b, (((1,), (1,)), ((), ())), preferred_element_type=F32)


def _dot_tn(a, b):
    return lax.dot_general(a, b, (((0,), (0,)), ((), ())), preferred_element_type=F32)


def _cumsum_rows(tri, x):
    hi = x.astype(BF16)
    r1 = x - hi.astype(F32)
    mid = r1.astype(BF16)
    lo = (r1 - mid.astype(F32)).astype(BF16)
    return (jnp.dot(tri, hi, preferred_element_type=F32)
            + jnp.dot(tri, mid, preferred_element_type=F32)
            + jnp.dot(tri, lo, preferred_element_type=F32))


def _segment_rows(x, seg, pos):
    n = x.shape[0]
    parts = [jnp.broadcast_to(x[s * seg + pos:s * seg + pos + 1, :], (seg, x.shape[1]))
             for s in range(n // seg)]
    return parts[0] if len(parts) == 1 else jnp.concatenate(parts, axis=0)


def _hg_gates(x, lb, row_valid=None):
    w = HG_WIDTH
    q = x[:, 0:w]
    f = lb + (1.0 - lb) * _sigmoid(x[:, w:2 * w])
    logf = jnp.log(f)
    k = 1.0 - f
    if row_valid is not None:
        logf = jnp.where(row_valid, logf, 0.0)
        k = jnp.where(row_valid, k, 0.0)
    return _silu(q), k, logf


def _hg_kernel(x_ref, xm_ref, lb_ref, gain_ref, tri_ref, y_ref, st_ref, *, nb, tb):
    w, dh = HG_WIDTH, HG_HEAD_DIM
    lb = lb_ref[...]
    tri = tri_ref[...]

    @pl.when(pl.program_id(0) == 0)
    def _():
        xm = xm_ref[...].astype(F32)
        valid = lax.broadcasted_iota(jnp.int32, (META_PAD, 1), 0) >= (META_PAD - N_META)
        _, k, logf = _hg_gates(xm, lb, valid)
        bc = _cumsum_rows(tri[0:META_PAD, 0:META_PAD], logf)
        ki = (k * jnp.exp(bc[META_PAD - 1:META_PAD, :] - bc)).astype(BF16)
        v = xm[:, 2 * w:3 * w].astype(BF16)
        for h in range(HG_HEADS):
            sl = slice(h * dh, (h + 1) * dh)
            s0 = _dot_tn(v[:, sl], ki[:, sl])
            for b in range(nb):
                st_ref[b, h] = s0

    ri = lax.broadcasted_iota(jnp.int32, (tb, tb), 0)
    ci = lax.broadcasted_iota(jnp.int32, (tb, tb), 1)
    diag_shift = HG_DIAG.bit_length() - 1
    diag_mask = ((ri >> diag_shift) == (ci >> diag_shift)) & (ci <= ri)
    levels = []
    seg = 2 * HG_DIAG
    while seg <= tb:
        levels.append(seg)
        seg *= 2
    seg_masks = [None if s == tb else
                 ((ri >> (s.bit_length() - 1)) == (ci >> (s.bit_length() - 1))).astype(F32)
                 for s in levels]
    row = lax.broadcasted_iota(jnp.int32, (tb, 1), 0)

    for b in range(nb):
        x = x_ref[b].astype(F32)
        qs, k, logf = _hg_gates(x, lb)
        v = x[:, 2 * w:3 * w].astype(BF16)
        g = x[:, 3 * w:4 * w]
        bc = _cumsum_rows(tri, logf)
        dlt = bc - _segment_rows(bc, HG_DIAG, HG_DIAG // 2 - 1)
        qk = [((qs * jnp.exp(dlt)).astype(BF16), (k * jnp.exp(-dlt)).astype(BF16))]
        for s in levels:
            e = jnp.exp(-jnp.abs(bc - _segment_rows(bc, s, s // 2 - 1)))
            upper = (row & (s - 1)) >= (s // 2)
            qk.append((jnp.where(upper, qs * e, 0.0).astype(BF16),
                       jnp.where(upper, 0.0, k * e).astype(BF16)))
        b_last = bc[tb - 1:tb, :]
        qi = (qs * jnp.exp(bc)).astype(BF16)
        ki = (k * jnp.exp(b_last - bc)).astype(BF16)
        dec = jnp.exp(b_last)
        outs = []
        for h in range(HG_HEADS):
            sl = slice(h * dh, (h + 1) * dh)
            sc = jnp.where(diag_mask, _dot_nt(qk[0][0][:, sl], qk[0][1][:, sl]), 0.0)
            for (ql, kl), m in zip(qk[1:], seg_masks):
                t = _dot_nt(ql[:, sl], kl[:, sl])
                sc = sc + (t if m is None else t * m)
            st = st_ref[b, h]
            o = (jnp.dot(sc.astype(BF16), v[:, sl], preferred_element_type=F32)
                 + _dot_nt(qi[:, sl], st.astype(BF16)))
            st_ref[b, h] = dec[:, sl] * st + _dot_tn(v[:, sl], ki[:, sl])
            ms = jnp.mean(o * o, axis=-1, keepdims=True)
            outs.append(o * lax.rsqrt(ms + EPS))
        o = jnp.concatenate(outs, axis=1) * gain_ref[...]
        y_ref[b] = (o * _silu(g)).astype(BF16)


def _hg_mixer(x, x_meta, lb, gain, tri):
    nb, seq, _ = x.shape
    tb = TIME_BLOCK
    kern = functools.partial(_hg_kernel, nb=nb, tb=tb)
    const2 = lambda t: (0, 0)
    return pl.pallas_call(
        kern,
        grid=(seq // tb,),
        in_specs=[
            pl.BlockSpec((nb, tb, 4 * HG_WIDTH), lambda t: (0, t, 0)),
            pl.BlockSpec(x_meta.shape, const2),
            pl.BlockSpec(lb.shape, const2),
            pl.BlockSpec(gain.shape, const2),
            pl.BlockSpec(tri.shape, const2),
        ],
        out_specs=pl.BlockSpec((nb, tb, HG_WIDTH), lambda t: (0, t, 0)),
        out_shape=jax.ShapeDtypeStruct((nb, seq, HG_WIDTH), BF16),
        scratch_shapes=[pltpu.VMEM((nb, HG_HEADS, HG_HEAD_DIM, HG_HEAD_DIM), F32)],
        compiler_params=_cparams(("arbitrary",)),
        name="hgrn2_mixer",
    )(x, x_meta, lb, gain, tri)


def _split3(x):
    hi = x.astype(BF16)
    lo = (x - hi.astype(F32)).astype(BF16)
    return hi, lo


def _pack_pairs(y):
    n = y.shape[1] // 2
    lo = pltpu.bitcast(y[:, :n].astype(BF16).astype(F32), jnp.uint32)
    hi = pltpu.bitcast(y[:, n:].astype(BF16).astype(F32), jnp.uint32)
    return (lo >> 16) | (hi & jnp.uint32(0xFFFF0000))


def _unpack_pairs(p):
    lo = pltpu.bitcast(p << 16, F32)
    hi = pltpu.bitcast(p & jnp.uint32(0xFFFF0000), F32)
    return lo, hi


def _merge_kernel(x_ref, ys_ref, yh_ref, gt_ref, wbs_ref, wbh_ref, wo_ref, g2_ref, wr_hi_ref,
                  wr_lo_ref, br_ref, tri_ref, h2_ref, xp_ref, eid_ref, wt_ref, rank_ref, cnt_ref,
                  run_ref):
    @pl.when(pl.program_id(0) == 0)
    def _():
        run_ref[...] = jnp.zeros_like(run_ref)

    gt = gt_ref[...].astype(F32)
    gs = _sigmoid(gt[:, :D_MODEL])
    gh = _sigmoid(gt[:, D_MODEL:])
    merged = (gs * jnp.dot(ys_ref[...], wbs_ref[...], preferred_element_type=F32)
              + gh * jnp.dot(yh_ref[...], wbh_ref[...], preferred_element_type=F32))
    h2 = x_ref[...] + jnp.dot(merged.astype(BF16), wo_ref[...], preferred_element_type=F32)
    h2_ref[...] = h2
    ms = jnp.mean(h2 * h2, axis=-1, keepdims=True)
    xn = h2 * lax.rsqrt(ms + EPS) * g2_ref[...]
    xp_ref[...] = _pack_pairs(xn)

    x_hi, x_lo = _split3(xn)
    logits = (_dot_nt(wr_hi_ref[...], x_hi) + _dot_nt(wr_lo_ref[...], x_hi)
              + _dot_nt(wr_hi_ref[...], x_lo)) + br_ref[:, 0:1]
    rows = logits.shape[1]
    expert = lax.broadcasted_iota(jnp.int32, (N_EXPERTS, rows), 0)
    tops, hots, ids = [], [], []
    sel = jnp.zeros((N_EXPERTS, rows), F32)
    for k in range(TOP_K):
        m = jnp.max(logits, axis=0, keepdims=True)
        idx = jnp.min(jnp.where(logits == m, expert, N_EXPERTS), axis=0, keepdims=True)
        hot = expert == idx
        logits = jnp.where(hot, NEG_BIG, logits)
        tops.append(m)
        hots.append(hot)
        ids.append(idx)
        sel = sel + hot.astype(F32)
    es = [jnp.exp(m - tops[0]) for m in tops]
    tot = es[0] + es[1] + es[2] + es[3]
    run = run_ref[...]
    prefix = jnp.dot(sel.astype(BF16), tri_ref[...], preferred_element_type=F32) + run[:, 0:1]
    ranks = [jnp.sum(jnp.where(hot, prefix, 0.0), axis=0, keepdims=True) for hot in hots]
    eid_ref[...] = jnp.concatenate(ids, axis=0)
    wt_ref[...] = jnp.concatenate([e / tot for e in es], axis=0)
    rank_ref[...] = jnp.concatenate(ranks, axis=0).astype(jnp.int32)
    run = run + jnp.sum(sel, axis=1, keepdims=True)
    run_ref[...] = run
    cnt_ref[...] = run


def _merge_router(x_rows, y_s5, y_hg, gates, wbs, wbh, wo, g2, wr_hi, wr_lo, br, tri_strict):
    n = x_rows.shape[0]
    rb = ROW_BLOCK
    const2 = lambda i: (0, 0)
    rowblk = lambda width: pl.BlockSpec((rb, width), lambda i: (i, 0))
    choice = pl.BlockSpec((TOP_K, rb), lambda i: (0, i))
    return pl.pallas_call(
        _merge_kernel,
        grid=(n // rb,),
        in_specs=[
            rowblk(D_MODEL), rowblk(S5_WIDTH), rowblk(HG_WIDTH), rowblk(2 * D_MODEL),
            pl.BlockSpec(wbs.shape, const2), pl.BlockSpec(wbh.shape, const2),
            pl.BlockSpec(wo.shape, const2), pl.BlockSpec(g2.shape, const2),
            pl.BlockSpec(wr_hi.shape, const2), pl.BlockSpec(wr_lo.shape, const2),
            pl.BlockSpec(br.shape, const2), pl.BlockSpec(tri_strict.shape, const2),
        ],
        out_specs=[
            rowblk(D_MODEL), rowblk(D_MODEL // 2), choice, choice, choice,
            pl.BlockSpec((N_EXPERTS, LANES), const2),
        ],
        out_shape=[
            jax.ShapeDtypeStruct((n, D_MODEL), F32),
            jax.ShapeDtypeStruct((n, D_MODEL // 2), jnp.uint32),
            jax.ShapeDtypeStruct((TOP_K, n), jnp.int32),
            jax.ShapeDtypeStruct((TOP_K, n), F32),
            jax.ShapeDtypeStruct((TOP_K, n), jnp.int32),
            jax.ShapeDtypeStruct((N_EXPERTS, LANES), F32),
        ],
        scratch_shapes=[pltpu.VMEM((N_EXPERTS, LANES), F32)],
        compiler_params=_cparams(("arbitrary",)),
        name="merge_router",
    )(x_rows, y_s5, y_hg, gates, wbs, wbh, wo, g2, wr_hi, wr_lo, br, tri_strict)


def _gather_rows(src, idx):
    n = idx.shape[0]
    width = src.shape[1]
    assert n % GATHER_ROWS == 0
    mesh = plsc.VectorSubcoreMesh(core_axis_name="core", subcore_axis_name="subcore")

    @functools.partial(pl.kernel, out_type=jax.ShapeDtypeStruct((n, width), src.dtype),
                       mesh=mesh, scratch_types=[], name="sc_gather_rows")
    def gather(src_hbm, idx_hbm, out_hbm):
        def body(idx_vmem, out_vmem):
            off = pl.multiple_of(pl.program_id(1) * GATHER_WINDOW, GATHER_WINDOW)
            pltpu.sync_copy(src_hbm.at[idx_vmem.at[0, pl.ds(off, GATHER_WINDOW)]], out_vmem)

        pltpu.emit_pipeline(
            body,
            grid=(n // INDEX_BLOCK, INDEX_SPLIT),
            in_specs=[pl.BlockSpec((1, INDEX_BLOCK), lambda i, j: (0, i))],
            out_specs=[pl.BlockSpec((GATHER_WINDOW, width), lambda i, j: (INDEX_SPLIT * i + j, 0))],
            core_axis_name=("core", "subcore"),
            dimension_semantics=(pltpu.PARALLEL, pltpu.ARBITRARY),
        )(idx_hbm, out_hbm)

    return gather(src, idx.reshape(1, n))


def _scatter_rows(src, dest, n_out):
    n_src, width = src.shape
    assert dest.shape == (TOP_K, n_src) and n_src % GATHER_ROWS == 0
    mesh = plsc.VectorSubcoreMesh(core_axis_name="core", subcore_axis_name="subcore")

    @functools.partial(pl.kernel, out_type=jax.ShapeDtypeStruct((n_out, width), src.dtype),
                       mesh=mesh, scratch_types=[], name="sc_scatter_rows")
    def scatter(src_hbm, idx_hbm, out_hbm):
        def body(src_vmem, idx_vmem):
            off = pl.multiple_of(pl.program_id(1) * GATHER_WINDOW, GATHER_WINDOW)
            for k in range(TOP_K):
                pltpu.sync_copy(src_vmem, out_hbm.at[idx_vmem.at[k, pl.ds(off, GATHER_WINDOW)]])

        pltpu.emit_pipeline(
            body,
            grid=(n_src // INDEX_BLOCK, INDEX_SPLIT),
            in_specs=[pl.BlockSpec((GATHER_WINDOW, width), lambda i, j: (INDEX_SPLIT * i + j, 0)),
                      pl.BlockSpec((TOP_K, INDEX_BLOCK), lambda i, j: (0, i))],
            out_specs=[],
            core_axis_name=("core", "subcore"),
            dimension_semantics=(pltpu.PARALLEL, pltpu.ARBITRARY),
        )(src_hbm, idx_hbm)

    return scatter(src, dest)


GU_GROUP = 256
DOWN_GROUPS = 4


def _expert_kernel(be_ref, nv_ref, nr_ref, x_ref, wgu_ref, bgu_ref, wd_ref, bd_ref, perm_ref,
                   y_ref, wgu_s, wd_s):
    i = pl.program_id(0)
    prev = be_ref[jnp.maximum(i - 1, 0)]
    fresh = (i == 0) | (be_ref[i] != prev)

    @pl.when(fresh & (i < nv_ref[0]))
    def _():
        for c in range(2 * D_EXPERT // GU_GROUP):
            cols = slice(c * GU_GROUP, (c + 1) * GU_GROUP)
            w = wgu_ref[0, :, cols].astype(BF16)
            wgu_s[:, cols] = jnp.dot(w, perm_ref[...], preferred_element_type=F32).astype(BF16)
        wd_s[...] = wd_ref[0].astype(BF16)

    @pl.when(i < nv_ref[0])
    def _():
        half = D_MODEL // 2
        live = lax.broadcasted_iota(jnp.int32, (MOE_BLOCK, 1), 0) < nr_ref[i]
        xa, xb = _unpack_pairs(jnp.where(live, x_ref[...], jnp.uint32(0)))
        x = jnp.concatenate([xa.astype(BF16), xb.astype(BF16)], axis=1)
        hw = GU_GROUP // 2
        n_groups = 2 * D_EXPERT // GU_GROUP

        def gate_up(c):
            return jnp.dot(x, wgu_s[:, c * GU_GROUP:(c + 1) * GU_GROUP],
                           preferred_element_type=F32)

        acc = None
        hcols = []
        gu_next = gate_up(0)
        for c in range(n_groups):
            gu = gu_next + bgu_ref[0, :, c * GU_GROUP:(c + 1) * GU_GROUP]
            if c + 1 < n_groups:
                gu_next = gate_up(c + 1)
            gate = jnp.minimum(gu[:, :hw], SWIGLU_LIMIT)
            up = jnp.clip(gu[:, hw:], -SWIGLU_LIMIT, SWIGLU_LIMIT)
            hcols.append(((up + 1.0) * (gate * _sigmoid(gate * SWIGLU_ALPHA))).astype(BF16))
            if (c + 1) % DOWN_GROUPS == 0:
                lo = c + 1 - DOWN_GROUPS
                part = jnp.dot(jnp.concatenate(hcols[lo:c + 1], axis=1),
                               wd_s[lo * hw:(c + 1) * hw, :], preferred_element_type=F32)
                acc = part if acc is None else acc + part
        y_ref[...] = _pack_pairs(acc + bd_ref[0])

    @pl.when(i >= nv_ref[0])
    def _():
        y_ref[...] = jnp.zeros_like(y_ref)


def _experts(block_e, n_valid, block_rows, x_rows, w_gate_up, b_gu_grouped, w_down, b_down, perm):
    n_rows = x_rows.shape[0]
    n_blocks = n_rows // MOE_BLOCK
    half = D_MODEL // 2
    by_expert = lambda i, be, nv, nr: (be[i], 0, 0)
    grid_spec = pltpu.PrefetchScalarGridSpec(
        num_scalar_prefetch=3,
        grid=(n_blocks,),
        in_specs=[
            pl.BlockSpec((MOE_BLOCK, half), lambda i, be, nv, nr: (i, 0)),
            pl.BlockSpec((1, D_MODEL, 2 * D_EXPERT), by_expert),
            pl.BlockSpec((1, 1, 2 * D_EXPERT), by_expert),
            pl.BlockSpec((1, D_EXPERT, D_MODEL), by_expert),
            pl.BlockSpec((1, 1, D_MODEL), by_expert),
            pl.BlockSpec(perm.shape, lambda i, be, nv, nr: (0, 0)),
        ],
        out_specs=pl.BlockSpec((MOE_BLOCK, half), lambda i, be, nv, nr: (i, 0)),
        scratch_shapes=[
            pltpu.VMEM((D_MODEL, 2 * D_EXPERT), BF16),
            pltpu.VMEM((D_EXPERT, D_MODEL), BF16),
        ],
    )
    return pl.pallas_call(
        _expert_kernel,
        grid_spec=grid_spec,
        out_shape=jax.ShapeDtypeStruct((n_rows, half), jnp.uint32),
        compiler_params=_cparams(("arbitrary",)),
        name="experts",
    )(block_e, n_valid, block_rows, x_rows, w_gate_up, b_gu_grouped, w_down, b_down, perm)


def _combine_kernel(h2_ref, y0_ref, y1_ref, y2_ref, y3_ref, wt_ref, gf_ref, out_ref):
    half = D_MODEL // 2
    wt = wt_ref[...]
    lo = jnp.zeros((h2_ref.shape[0], half), F32)
    hi = jnp.zeros((h2_ref.shape[0], half), F32)
    for k, yk_ref in enumerate((y0_ref, y1_ref, y2_ref, y3_ref)):
        a, b = _unpack_pairs(yk_ref[...])
        lo = lo + wt[:, k:k + 1] * a
        hi = hi + wt[:, k:k + 1] * b
    y = h2_ref[...] + jnp.concatenate([lo, hi], axis=1)
    ms = jnp.mean(y * y, axis=-1, keepdims=True)
    out_ref[...] = y * lax.rsqrt(ms + EPS) * gf_ref[...]


def _combine(h2, y_tok, wts, gain):
    n = h2.shape[0]
    rb = ROW_BLOCK
    rowblk = lambda width: pl.BlockSpec((rb, width), lambda i: (i, 0))
    steps = n // rb
    choice = lambda k: pl.BlockSpec((rb, D_MODEL // 2), lambda i: (k * steps + i, 0))
    return pl.pallas_call(
        _combine_kernel,
        grid=(steps,),
        in_specs=[rowblk(D_MODEL)] + [choice(k) for k in range(TOP_K)]
        + [rowblk(TOP_K), pl.BlockSpec((1, D_MODEL), lambda i: (0, 0))],
        out_specs=rowblk(D_MODEL),
        out_shape=jax.ShapeDtypeStruct((n, D_MODEL), F32),
        compiler_params=_cparams(("parallel",)),
        name="combine_norm",
    )(h2, y_tok, y_tok, y_tok, y_tok, wts, gain)


def _lower_tri(n, strict):
    r = lax.broadcasted_iota(jnp.int32, (n, n), 0)
    c = lax.broadcasted_iota(jnp.int32, (n, n), 1)
    return ((c < r) if strict else (c <= r)).astype(BF16)


def kernel(x, meta_tokens, norm1_gain, w_in, s5_lambda_re, s5_lambda_im, s5_log_dt, s5_b_re,
           s5_b_im, s5_c_re, s5_c_im, s5_d, s5_w_glu, hgrn_lb_logits, hgrn_norm_gain,
           w_branch_s5, w_branch_hgrn, w_out, norm2_gain, w_router, b_router, w_gate_up,
           b_gate_up, w_down, b_down, final_norm_gain):
    nb, seq, d = x.shape
    n_tok = nb * seq
    assert d == D_MODEL and seq % TIME_BLOCK == 0 and n_tok % ROW_BLOCK == 0
    x_rows = x.reshape(n_tok, d)
    w_in_b = w_in[0].astype(BF16)
    g1 = norm1_gain[0].reshape(1, d).astype(F32)

    u, qfvg, gates = _inproj(x_rows, g1, w_in_b, ROW_BLOCK)
    meta_rows = jnp.concatenate(
        [jnp.zeros((META_PAD - N_META, d), F32), meta_tokens.astype(F32)], axis=0)
    u_m, qfvg_m, _ = _inproj(meta_rows, g1, w_in_b, META_PAD)

    bm, cm, tab, ptab = _s5_params(s5_lambda_re[0], s5_lambda_im[0], s5_log_dt[0], s5_b_re[0],
                                   s5_b_im[0], s5_c_re[0], s5_c_im[0])
    y_s5 = _s5_mixer(u.reshape(nb, seq, S5_WIDTH), u_m, bm, cm, tab, ptab,
                     s5_d[0].reshape(1, S5_WIDTH).astype(F32), s5_w_glu[0].astype(BF16))

    lower_bounds = jnp.cumsum(jax.nn.softmax(hgrn_lb_logits.astype(F32), axis=0), axis=0)
    lb = lower_bounds[0].reshape(1, HG_WIDTH)
    y_hg = _hg_mixer(qfvg.reshape(nb, seq, 4 * HG_WIDTH), qfvg_m, lb,
                     hgrn_norm_gain[0].reshape(1, HG_WIDTH).astype(F32),
                     _lower_tri(TIME_BLOCK, strict=False))

    wr = w_router[0].astype(F32).T
    wr_hi = wr.astype(BF16)
    wr_lo = (wr - wr_hi.astype(F32)).astype(BF16)
    br = jnp.broadcast_to(b_router[0].astype(F32)[:, None], (N_EXPERTS, LANES))
    h2, xn_packed, eid, wts, rank, counts = _merge_router(
        x_rows, y_s5.reshape(n_tok, S5_WIDTH), y_hg.reshape(n_tok, HG_WIDTH), gates,
        w_branch_s5[0].astype(BF16), w_branch_hgrn[0].astype(BF16), w_out[0].astype(BF16),
        norm2_gain[0].reshape(1, d).astype(F32), wr_hi, wr_lo, br,
        _lower_tri(ROW_BLOCK, strict=True).T)

    n_assign = n_tok * TOP_K
    n_blocks = n_assign // MOE_BLOCK + N_EXPERTS
    n_rows = n_blocks * MOE_BLOCK
    n_rows_pad = -(-n_rows // GATHER_ROWS) * GATHER_ROWS
    cnt = counts[:, 0].astype(jnp.int32)
    padded = (cnt + MOE_BLOCK - 1) // MOE_BLOCK * MOE_BLOCK
    padded_end = jnp.cumsum(padded)
    padded_start = padded_end - padded
    experts = jnp.arange(N_EXPERTS, dtype=jnp.int32)[None, :, None]
    start_of = jnp.sum(jnp.where(eid[:, None, :] == experts, padded_start[None, :, None], 0), axis=1)
    dest = start_of + rank
    block_start = jnp.arange(n_rows_pad // MOE_BLOCK, dtype=jnp.int32) * MOE_BLOCK
    block_e = jnp.minimum(jnp.sum((block_start[:, None] >= padded_end[None, :]).astype(jnp.int32),
                                  axis=1), N_EXPERTS - 1)
    of_block = block_e[:, None] == jnp.arange(N_EXPERTS, dtype=jnp.int32)[None, :]
    end_of_block = jnp.sum(jnp.where(of_block, (padded_start + cnt)[None, :], 0), axis=1)
    block_rows = jnp.clip(end_of_block - block_start, 0, MOE_BLOCK)
    n_valid = (padded_end[-1] // MOE_BLOCK).astype(jnp.int32).reshape(1)

    x_sorted = _scatter_rows(xn_packed, dest, n_rows_pad)
    hw = GU_GROUP // 2
    pr = jnp.arange(GU_GROUP)
    src = jnp.where(pr < hw, 2 * pr, 2 * (pr - hw) + 1)
    perm = (jnp.arange(GU_GROUP)[:, None] == src[None, :]).astype(BF16)
    b_gu = b_gate_up[0].astype(F32).reshape(N_EXPERTS, 2 * D_EXPERT // GU_GROUP, hw, 2)
    b_gu = b_gu.transpose(0, 1, 3, 2).reshape(N_EXPERTS, 1, 2 * D_EXPERT)
    y_sorted = _experts(block_e, n_valid, block_rows, x_sorted, w_gate_up[0], b_gu, w_down[0],
                        b_down[0].astype(F32).reshape(N_EXPERTS, 1, d), perm)
    y_tok = _gather_rows(y_sorted, dest.reshape(-1))

    out = _combine(h2, y_tok, wts.T, final_norm_gain.reshape(1, d).astype(F32))
    return out.reshape(nb, seq, d)
```

```python
import functools

import jax
import jax.numpy as jnp
from jax import lax
from jax.experimental import pallas as pl
from jax.experimental.pallas import tpu as pltpu
from jax.experimental.pallas import tpu_sc as plsc

F32 = jnp.float32
BF16 = jnp.bfloat16

D_MODEL = 1024
N_META = 16
S5_GROUP = 16
S5_GROUPS = 32
S5_WIDTH = 512
S5_STATE = 64
HG_HEADS = 4
HG_HEAD_DIM = 128
HG_WIDTH = 512
N_EXPERTS = 32
TOP_K = 4
D_EXPERT = 1024
SWIGLU_ALPHA = 1.702
SWIGLU_LIMIT = 7.0
EPS = 1e-6

LANES = 128
SUBLANES = 8
TIME_BLOCK = 256
META_PAD = TIME_BLOCK
ROW_BLOCK = 512
MOE_BLOCK = 512
GATHER_WINDOW = 64
INDEX_BLOCK = 128
INDEX_SPLIT = INDEX_BLOCK // GATHER_WINDOW
SC_SUBCORES = 32
GATHER_ROWS = INDEX_BLOCK * SC_SUBCORES
HG_DIAG = 32
NEG_BIG = -1e30
VMEM_LIMIT = 56 * 1024 * 1024


def _cparams(sem):
    return pltpu.CompilerParams(dimension_semantics=sem, vmem_limit_bytes=VMEM_LIMIT)


def _sigmoid(x):
    return 0.5 * jnp.tanh(0.5 * x) + 0.5


def _silu(x):
    h = 0.5 * x
    return h + h * jnp.tanh(h)


def _inproj_kernel(x_ref, g_ref, w_ref, u_ref, qfvg_ref, gates_ref):
    x = x_ref[...]
    ms = jnp.mean(x * x, axis=-1, keepdims=True)
    xn = (x * lax.rsqrt(ms + EPS) * g_ref[...]).astype(BF16)
    u_ref[...] = jnp.dot(xn, w_ref[:, 0:S5_WIDTH], preferred_element_type=F32).astype(BF16)
    a, b = S5_WIDTH, S5_WIDTH + 4 * HG_WIDTH
    qfvg_ref[...] = jnp.dot(xn, w_ref[:, a:b], preferred_element_type=F32).astype(BF16)
    gates_ref[...] = jnp.dot(xn, w_ref[:, b:], preferred_element_type=F32).astype(BF16)


def _inproj(rows, gain, w_in_bf16, row_block):
    n = rows.shape[0]
    n_gate = 2 * D_MODEL
    return pl.pallas_call(
        _inproj_kernel,
        grid=(n // row_block,),
        in_specs=[
            pl.BlockSpec((row_block, D_MODEL), lambda i: (i, 0)),
            pl.BlockSpec((1, D_MODEL), lambda i: (0, 0)),
            pl.BlockSpec(w_in_bf16.shape, lambda i: (0, 0)),
        ],
        out_specs=[
            pl.BlockSpec((row_block, S5_WIDTH), lambda i: (i, 0)),
            pl.BlockSpec((row_block, 4 * HG_WIDTH), lambda i: (i, 0)),
            pl.BlockSpec((row_block, n_gate), lambda i: (i, 0)),
        ],
        out_shape=[
            jax.ShapeDtypeStruct((n, S5_WIDTH), BF16),
            jax.ShapeDtypeStruct((n, 4 * HG_WIDTH), BF16),
            jax.ShapeDtypeStruct((n, n_gate), BF16),
        ],
        compiler_params=_cparams(("parallel",)),
        name="inproj",
    )(rows, gain, w_in_bf16)


N_STATE = S5_GROUPS * S5_STATE
HALF_STATE = N_STATE // 2
SCAN_GROUPS_PER_HALF = HALF_STATE // LANES
SEG_COUNT = SUBLANES
SEG_LEN = TIME_BLOCK // SEG_COUNT


def _s5_scan_group(xs, xb, tab, ptab, car_ref, g, chains, tb):
    sub = lax.broadcasted_iota(jnp.int32, (SUBLANES, LANES), 0)
    zero = jnp.zeros((SUBLANES, LANES), F32)
    re, im = slice(0, LANES), slice(LANES, 2 * LANES)

    def rows_of(b, tau):
        return slice(b * tb + tau * SUBLANES, b * tb + (tau + 1) * SUBLANES)

    ends = [(xs[rows_of(b, SEG_LEN - 1), re], xs[rows_of(b, SEG_LEN - 1), im]) for b in chains]
    for tau in range(SEG_LEN - 1):
        p_rows = slice((SEG_LEN - 2 - tau) * SUBLANES, (SEG_LEN - 1 - tau) * SUBLANES)
        pr, pi = ptab[0, p_rows, :], ptab[1, p_rows, :]
        vals = [(xs[rows_of(b, tau), re], xs[rows_of(b, tau), im]) for b in chains]
        ends = [(pr * br - pi * bi + er, pr * bi + pi * br + ei)
                for (er, ei), (br, bi) in zip(ends, vals)]

    m = [tab[i] for i in range(8)]
    starts = []
    for b, (er, ei) in zip(chains, ends):
        gr = jnp.where(sub == 0, car_ref[b, g, :, re], pltpu.roll(er, 1, 0))
        gi = jnp.where(sub == 0, car_ref[b, g, :, im], pltpu.roll(ei, 1, 0))
        for q, shift in enumerate((1, 2, 4)):
            ar, ai = m[2 * q], m[2 * q + 1]
            rr = pltpu.roll(gr, shift, 0)
            ri = pltpu.roll(gi, shift, 0)
            gr, gi = gr + ar * rr - ai * ri, gi + ar * ri + ai * rr
        starts.append((gr, gi))
        xr = m[6] * gr - m[7] * gi + er
        xi = m[6] * gi + m[7] * gr + ei
        car_ref[b, g, :, re] = jnp.broadcast_to(xr[SUBLANES - 1:SUBLANES, :], (SUBLANES, LANES))
        car_ref[b, g, :, im] = jnp.broadcast_to(xi[SUBLANES - 1:SUBLANES, :], (SUBLANES, LANES))

    if xb is not None:
        lr, li = tab[8], tab[9]
        state = starts
        prev = None
        for tau in range(SEG_LEN):
            vals = [(xs[rows_of(b, tau), re], xs[rows_of(b, tau), im]) for b in chains]
            state = [(lr * xr - li * xi + br, lr * xi + li * xr + bi)
                     for (xr, xi), (br, bi) in zip(state, vals)]
            if tau % 2 == 0:
                prev = state
                continue
            for b, (xr0, xi0), (xr1, xi1) in zip(chains, prev, state):
                rows = slice(b * tb + (tau - 1) * SUBLANES, b * tb + (tau + 1) * SUBLANES)
                xb[rows, re] = jnp.concatenate([xr0, xr1], axis=0).astype(BF16)
                xb[rows, im] = jnp.concatenate([xi0, xi1], axis=0).astype(BF16)


def _s5_kernel(u_ref, um_ref, perm_ref, unperm_ref, bm_ref, cm_ref, tab_ref, ptab_ref, d_ref,
               wglu_ref, y_ref, car_ref, up_ref, xs_ref, xb_ref, yacc_ref, *, nb, tb):
    half_ch = S5_WIDTH // 2

    def permute_in(u, b):
        up = jnp.dot(perm_ref[...], u, preferred_element_type=F32).astype(BF16)
        for hf in range(2):
            up_ref[hf, b * tb:(b + 1) * tb, :] = up[:, hf * half_ch:(hf + 1) * half_ch]

    def project_in(g, slot, n_rows):
        xs_ref[slot, 0:n_rows, :] = jnp.dot(up_ref[g // SCAN_GROUPS_PER_HALF, 0:n_rows, :],
                                            bm_ref[g], preferred_element_type=F32)

    def project_out(g0):
        c_pair = cm_ref[pl.ds(g0, 2)].reshape(4 * LANES, half_ch)
        yacc_ref[g0 // SCAN_GROUPS_PER_HALF] += jnp.dot(xb_ref[...], c_pair,
                                                        preferred_element_type=F32)

    def scan(g, slot, chains, finish):
        xb = xb_ref.at[:, slot * 2 * LANES:(slot + 1) * 2 * LANES] if finish else None
        _s5_scan_group(xs_ref.at[slot], xb, tab_ref.at[g], ptab_ref.at[g], car_ref, g, chains, tb)

    @pl.when(pl.program_id(0) == 0)
    def _():
        car_ref[...] = jnp.zeros_like(car_ref)
        permute_in(um_ref[...], 0)

        def meta_trip(g, carry):
            project_in(g, 0, tb)
            scan(g, 0, [0], False)
            return carry

        lax.fori_loop(0, N_STATE // LANES, meta_trip, 0)
        for b in range(1, nb):
            car_ref[b] = car_ref[0]

    for b in range(nb):
        permute_in(u_ref[b], b)
    yacc_ref[...] = jnp.zeros_like(yacc_ref)

    def trip(t, carry):
        g0 = 2 * t
        project_in(g0, 0, nb * tb)
        project_in(g0 + 1, 1, nb * tb)
        scan(g0, 0, list(range(nb)), True)
        scan(g0 + 1, 1, list(range(nb)), True)
        project_out(g0)
        return carry

    lax.fori_loop(0, N_STATE // LANES // 2, trip, 0)

    for b in range(nb):
        rows = slice(b * tb, (b + 1) * tb)
        y = jnp.concatenate([yacc_ref[0, rows, :], yacc_ref[1, rows, :]], axis=1)
        up = jnp.concatenate([up_ref[0, rows, :], up_ref[1, rows, :]], axis=1)
        y = y + d_ref[...] * up.astype(F32)
        y = 0.5 * y * (1.0 + jnp.tanh(0.7978845608028654 * (y + 0.044715 * (y * y * y))))
        z = jnp.dot(y.astype(BF16), wglu_ref[...], preferred_element_type=F32)
        out = (y * _sigmoid(z)).astype(BF16)
        y_ref[b] = jnp.dot(unperm_ref[...], out, preferred_element_type=F32).astype(BF16)


def _s5_mixer(u, u_meta, bm, cm, tab, ptab, d_skip, w_glu_bf16):
    nb, seq, _ = u.shape
    tb = TIME_BLOCK
    r = jnp.arange(tb)
    perm = (((r % SEG_COUNT) * SEG_LEN + r // SEG_COUNT)[:, None] == r[None, :]).astype(BF16)
    kern = functools.partial(_s5_kernel, nb=nb, tb=tb)
    const2 = lambda t: (0, 0)
    const3 = lambda t: (0, 0, 0)
    const4 = lambda t: (0, 0, 0, 0)
    n_groups = N_STATE // LANES
    half_ch = S5_WIDTH // 2
    return pl.pallas_call(
        kern,
        grid=(seq // tb,),
        in_specs=[
            pl.BlockSpec((nb, tb, S5_WIDTH), lambda t: (0, t, 0)),
            pl.BlockSpec(u_meta.shape, const2),
            pl.BlockSpec(perm.shape, const2),
            pl.BlockSpec(perm.shape, const2),
            pl.BlockSpec(bm.shape, const3),
            pl.BlockSpec(cm.shape, const3),
            pl.BlockSpec(tab.shape, const4),
            pl.BlockSpec(ptab.shape, const4),
            pl.BlockSpec(d_skip.shape, const2),
            pl.BlockSpec(w_glu_bf16.shape, const2),
        ],
        out_specs=pl.BlockSpec((nb, tb, S5_WIDTH), lambda t: (0, t, 0)),
        out_shape=jax.ShapeDtypeStruct((nb, seq, S5_WIDTH), BF16),
        scratch_shapes=[pltpu.VMEM((nb, n_groups, SUBLANES, 2 * LANES), F32),
                        pltpu.VMEM((2, nb * tb, half_ch), BF16),
                        pltpu.VMEM((2, nb * tb, 2 * LANES), F32),
                        pltpu.VMEM((nb * tb, 4 * LANES), BF16),
                        pltpu.VMEM((2, nb * tb, half_ch), F32)],
        compiler_params=_cparams(("arbitrary",)),
        name="s5_mixer",
    )(u, u_meta, perm, perm.T, bm, cm, tab, ptab, d_skip, w_glu_bf16)


def _s5_params(lam_re, lam_im, log_dt, b_re, b_im, c_re, c_im):
    lam = lax.complex(lam_re.astype(F32), lam_im.astype(F32))
    dt = jnp.exp(log_dt.astype(F32))[:, None]
    lam_dt = lam * dt
    lam_bar = jnp.exp(lam_dt)
    b_bar = ((lam_bar - 1.0) / lam)[:, :, None] * lax.complex(b_re.astype(F32), b_im.astype(F32))
    gl = S5_GROUPS // 2
    eye = jnp.eye(gl, dtype=F32)

    def in_half(bh):
        def blk(part):
            t = jnp.einsum('gph,gk->ghkp', part, eye)
            return t.reshape(gl * S5_GROUP, gl * S5_STATE)
        return jnp.concatenate([blk(jnp.real(bh)), blk(jnp.imag(bh))], axis=1)

    def out_half(cr, ci):
        def blk(part):
            t = jnp.einsum('ghp,gk->gpkh', part, eye)
            return t.reshape(gl * S5_STATE, gl * S5_GROUP)
        return jnp.concatenate([blk(cr), blk(-ci)], axis=0)

    bm = jnp.stack([in_half(b_bar[:gl]), in_half(b_bar[gl:])]).astype(BF16)
    cm = jnp.stack([out_half(c_re[:gl].astype(F32), c_im[:gl].astype(F32)),
                    out_half(c_re[gl:].astype(F32), c_im[gl:].astype(F32))]).astype(BF16)
    lam_flat = lam_dt.reshape(1, N_STATE)
    rows = jnp.arange(SUBLANES, dtype=F32)[:, None]
    tabs = []
    for shift in (1, 2, 4):
        p = jnp.exp(lam_flat * float(SEG_LEN * shift)) * (rows >= shift).astype(F32)
        tabs += [jnp.real(p), jnp.imag(p)]
    for p in (jnp.exp(lam_flat * float(SEG_LEN)), jnp.exp(lam_flat)):
        tabs += [jnp.real(p), jnp.imag(p)]
    tab = jnp.stack([jnp.broadcast_to(t, (SUBLANES, N_STATE)) for t in tabs]).astype(F32)
    steps = (jnp.arange(SEG_LEN * SUBLANES) // SUBLANES + 1).astype(F32)[:, None]
    p = jnp.exp(lam_flat * steps)
    ptab = jnp.stack([jnp.real(p), jnp.imag(p)]).astype(F32)
    ng = SCAN_GROUPS_PER_HALF
    bm = bm.reshape(2, gl * S5_GROUP, 2, ng, LANES).transpose(0, 3, 1, 2, 4)
    bm = bm.reshape(2 * ng, gl * S5_GROUP, 2 * LANES)
    cm = cm.reshape(2, 2, ng, LANES, gl * S5_GROUP).transpose(0, 2, 1, 3, 4)
    cm = cm.reshape(2 * ng, 2 * LANES, gl * S5_GROUP)
    tab = tab.reshape(tab.shape[0], SUBLANES, 2 * ng, LANES).transpose(2, 0, 1, 3)
    ptab = ptab.reshape(2, SEG_LEN * SUBLANES, 2 * ng, LANES).transpose(2, 0, 1, 3)
    return bm, cm, tab, ptab


def _dot_nt(a, b):
    return lax.dot_general(a, b, (((1,), (1,)), ((), ())), preferred_element_type=F32)


def _dot_tn(a, b):
    return lax.dot_general(a, b, (((0,), (0,)), ((), ())), preferred_element_type=F32)


def _cumsum_rows(tri, x):
    hi = x.astype(BF16)
    r1 = x - hi.astype(F32)
    mid = r1.astype(BF16)
    lo = (r1 - mid.astype(F32)).astype(BF16)
    return (jnp.dot(tri, hi, preferred_element_type=F32)
            + jnp.dot(tri, mid, preferred_element_type=F32)
            + jnp.dot(tri, lo, preferred_element_type=F32))


def _segment_rows(x, seg, pos):
    n = x.shape[0]
    parts = [jnp.broadcast_to(x[s * seg + pos:s * seg + pos + 1, :], (seg, x.shape[1]))
             for s in range(n // seg)]
    return parts[0] if len(parts) == 1 else jnp.concatenate(parts, axis=0)


def _hg_gates(x, lb, row_valid=None):
    w = HG_WIDTH
    q = x[:, 0:w]
    f = lb + (1.0 - lb) * _sigmoid(x[:, w:2 * w])
    logf = jnp.log(f)
    k = 1.0 - f
    if row_valid is not None:
        logf = jnp.where(row_valid, logf, 0.0)
        k = jnp.where(row_valid, k, 0.0)
    return _silu(q), k, logf


def _hg_kernel(x_ref, xm_ref, lb_ref, gain_ref, tri_ref, y_ref, st_ref, *, nb, tb):
    w, dh = HG_WIDTH, HG_HEAD_DIM
    lb = lb_ref[...]
    tri = tri_ref[...]

    @pl.when(pl.program_id(0) == 0)
    def _():
        xm = xm_ref[...].astype(F32)
        valid = lax.broadcasted_iota(jnp.int32, (META_PAD, 1), 0) >= (META_PAD - N_META)
        _, k, logf = _hg_gates(xm, lb, valid)
        bc = _cumsum_rows(tri[0:META_PAD, 0:META_PAD], logf)
        ki = (k * jnp.exp(bc[META_PAD - 1:META_PAD, :] - bc)).astype(BF16)
        v = xm[:, 2 * w:3 * w].astype(BF16)
        for h in range(HG_HEADS):
            sl = slice(h * dh, (h + 1) * dh)
            s0 = _dot_tn(v[:, sl], ki[:, sl])
            for b in range(nb):
                st_ref[b, h] = s0

    ri = lax.broadcasted_iota(jnp.int32, (tb, tb), 0)
    ci = lax.broadcasted_iota(jnp.int32, (tb, tb), 1)
    diag_shift = HG_DIAG.bit_length() - 1
    diag_mask = ((ri >> diag_shift) == (ci >> diag_shift)) & (ci <= ri)
    levels = []
    seg = 2 * HG_DIAG
    while seg <= tb:
        levels.append(seg)
        seg *= 2
    seg_masks = [None if s == tb else
                 ((ri >> (s.bit_length() - 1)) == (ci >> (s.bit_length() - 1))).astype(F32)
                 for s in levels]
    row = lax.broadcasted_iota(jnp.int32, (tb, 1), 0)

    for b in range(nb):
        x = x_ref[b].astype(F32)
        qs, k, logf = _hg_gates(x, lb)
        v = x[:, 2 * w:3 * w].astype(BF16)
        g = x[:, 3 * w:4 * w]
        bc = _cumsum_rows(tri, logf)
        dlt = bc - _segment_rows(bc, HG_DIAG, HG_DIAG // 2 - 1)
        qk = [((qs * jnp.exp(dlt)).astype(BF16), (k * jnp.exp(-dlt)).astype(BF16))]
        for s in levels:
            e = jnp.exp(-jnp.abs(bc - _segment_rows(bc, s, s // 2 - 1)))
            upper = (row & (s - 1)) >= (s // 2)
            qk.append((jnp.where(upper, qs * e, 0.0).astype(BF16),
                       jnp.where(upper, 0.0, k * e).astype(BF16)))
        b_last = bc[tb - 1:tb, :]
        qi = (qs * jnp.exp(bc)).astype(BF16)
        ki = (k * jnp.exp(b_last - bc)).astype(BF16)
        dec = jnp.exp(b_last)
        outs = []
        for h in range(HG_HEADS):
            sl = slice(h * dh, (h + 1) * dh)
            sc = jnp.where(diag_mask, _dot_nt(qk[0][0][:, sl], qk[0][1][:, sl]), 0.0)
            for (ql, kl), m in zip(qk[1:], seg_masks):
                t = _dot_nt(ql[:, sl], kl[:, sl])
                sc = sc + (t if m is None else t * m)
            st = st_ref[b, h]
            o = (jnp.dot(sc.astype(BF16), v[:, sl], preferred_element_type=F32)
                 + _dot_nt(qi[:, sl], st.astype(BF16)))
            st_ref[b, h] = dec[:, sl] * st + _dot_tn(v[:, sl], ki[:, sl])
            ms = jnp.mean(o * o, axis=-1, keepdims=True)
            outs.append(o * lax.rsqrt(ms + EPS))
        o = jnp.concatenate(outs, axis=1) * gain_ref[...]
        y_ref[b] = (o * _silu(g)).astype(BF16)


def _hg_mixer(x, x_meta, lb, gain, tri):
    nb, seq, _ = x.shape
    tb = TIME_BLOCK
    kern = functools.partial(_hg_kernel, nb=nb, tb=tb)
    const2 = lambda t: (0, 0)
    return pl.pallas_call(
        kern,
        grid=(seq // tb,),
        in_specs=[
            pl.BlockSpec((nb, tb, 4 * HG_WIDTH), lambda t: (0, t, 0)),
            pl.BlockSpec(x_meta.shape, const2),
            pl.BlockSpec(lb.shape, const2),
            pl.BlockSpec(gain.shape, const2),
            pl.BlockSpec(tri.shape, const2),
        ],
        out_specs=pl.BlockSpec((nb, tb, HG_WIDTH), lambda t: (0, t, 0)),
        out_shape=jax.ShapeDtypeStruct((nb, seq, HG_WIDTH), BF16),
        scratch_shapes=[pltpu.VMEM((nb, HG_HEADS, HG_HEAD_DIM, HG_HEAD_DIM), F32)],
        compiler_params=_cparams(("arbitrary",)),
        name="hgrn2_mixer",
    )(x, x_meta, lb, gain, tri)


def _split3(x):
    hi = x.astype(BF16)
    lo = (x - hi.astype(F32)).astype(BF16)
    return hi, lo


def _pack_pairs(y):
    n = y.shape[1] // 2
    lo = pltpu.bitcast(y[:, :n].astype(BF16).astype(F32), jnp.uint32)
    hi = pltpu.bitcast(y[:, n:].astype(BF16).astype(F32), jnp.uint32)
    return (lo >> 16) | (hi & jnp.uint32(0xFFFF0000))


def _unpack_pairs(p):
    lo = pltpu.bitcast(p << 16, F32)
    hi = pltpu.bitcast(p & jnp.uint32(0xFFFF0000), F32)
    return lo, hi


def _merge_kernel(x_ref, ys_ref, yh_ref, gt_ref, wbs_ref, wbh_ref, wo_ref, g2_ref, wr_hi_ref,
                  wr_lo_ref, br_ref, tri_ref, h2_ref, xp_ref, eid_ref, wt_ref, rank_ref, cnt_ref,
                  run_ref):
    @pl.when(pl.program_id(0) == 0)
    def _():
        run_ref[...] = jnp.zeros_like(run_ref)

    gt = gt_ref[...].astype(F32)
    gs = _sigmoid(gt[:, :D_MODEL])
    gh = _sigmoid(gt[:, D_MODEL:])
    merged = (gs * jnp.dot(ys_ref[...], wbs_ref[...], preferred_element_type=F32)
              + gh * jnp.dot(yh_ref[...], wbh_ref[...], preferred_element_type=F32))
    h2 = x_ref[...] + jnp.dot(merged.astype(BF16), wo_ref[...], preferred_element_type=F32)
    h2_ref[...] = h2
    ms = jnp.mean(h2 * h2, axis=-1, keepdims=True)
    xn = h2 * lax.rsqrt(ms + EPS) * g2_ref[...]
    xp_ref[...] = _pack_pairs(xn)

    x_hi, x_lo = _split3(xn)
    logits = (_dot_nt(wr_hi_ref[...], x_hi) + _dot_nt(wr_lo_ref[...], x_hi)
              + _dot_nt(wr_hi_ref[...], x_lo)) + br_ref[:, 0:1]
    rows = logits.shape[1]
    expert = lax.broadcasted_iota(jnp.int32, (N_EXPERTS, rows), 0)
    tops, hots, ids = [], [], []
    sel = jnp.zeros((N_EXPERTS, rows), F32)
    for k in range(TOP_K):
        m = jnp.max(logits, axis=0, keepdims=True)
        idx = jnp.min(jnp.where(logits == m, expert, N_EXPERTS), axis=0, keepdims=True)
        hot = expert == idx
        logits = jnp.where(hot, NEG_BIG, logits)
        tops.append(m)
        hots.append(hot)
        ids.append(idx)
        sel = sel + hot.astype(F32)
    es = [jnp.exp(m - tops[0]) for m in tops]
    tot = es[0] + es[1] + es[2] + es[3]
    run = run_ref[...]
    prefix = jnp.dot(sel.astype(BF16), tri_ref[...], preferred_element_type=F32) + run[:, 0:1]
    ranks = [jnp.sum(jnp.where(hot, prefix, 0.0), axis=0, keepdims=True) for hot in hots]
    eid_ref[...] = jnp.concatenate(ids, axis=0)
    wt_ref[...] = jnp.concatenate([e / tot for e in es], axis=0)
    rank_ref[...] = jnp.concatenate(ranks, axis=0).astype(jnp.int32)
    run = run + jnp.sum(sel, axis=1, keepdims=True)
    run_ref[...] = run
    cnt_ref[...] = run


def _merge_router(x_rows, y_s5, y_hg, gates, wbs, wbh, wo, g2, wr_hi, wr_lo, br, tri_strict):
    n = x_rows.shape[0]
    rb = ROW_BLOCK
    const2 = lambda i: (0, 0)
    rowblk = lambda width: pl.BlockSpec((rb, width), lambda i: (i, 0))
    choice = pl.BlockSpec((TOP_K, rb), lambda i: (0, i))
    return pl.pallas_call(
        _merge_kernel,
        grid=(n // rb,),
        in_specs=[
            rowblk(D_MODEL), rowblk(S5_WIDTH), rowblk(HG_WIDTH), rowblk(2 * D_MODEL),
            pl.BlockSpec(wbs.shape, const2), pl.BlockSpec(wbh.shape, const2),
            pl.BlockSpec(wo.shape, const2), pl.BlockSpec(g2.shape, const2),
            pl.BlockSpec(wr_hi.shape, const2), pl.BlockSpec(wr_lo.shape, const2),
            pl.BlockSpec(br.shape, const2), pl.BlockSpec(tri_strict.shape, const2),
        ],
        out_specs=[
            rowblk(D_MODEL), rowblk(D_MODEL // 2), choice, choice, choice,
            pl.BlockSpec((N_EXPERTS, LANES), const2),
        ],
        out_shape=[
            jax.ShapeDtypeStruct((n, D_MODEL), F32),
            jax.ShapeDtypeStruct((n, D_MODEL // 2), jnp.uint32),
            jax.ShapeDtypeStruct((TOP_K, n), jnp.int32),
            jax.ShapeDtypeStruct((TOP_K, n), F32),
            jax.ShapeDtypeStruct((TOP_K, n), jnp.int32),
            jax.ShapeDtypeStruct((N_EXPERTS, LANES), F32),
        ],
        scratch_shapes=[pltpu.VMEM((N_EXPERTS, LANES), F32)],
        compiler_params=_cparams(("arbitrary",)),
        name="merge_router",
    )(x_rows, y_s5, y_hg, gates, wbs, wbh, wo, g2, wr_hi, wr_lo, br, tri_strict)


def _gather_rows(src, idx):
    n = idx.shape[0]
    width = src.shape[1]
    assert n % GATHER_ROWS == 0
    mesh = plsc.VectorSubcoreMesh(core_axis_name="core", subcore_axis_name="subcore")

    @functools.partial(pl.kernel, out_type=jax.ShapeDtypeStruct((n, width), src.dtype),
                       mesh=mesh, scratch_types=[], name="sc_gather_rows")
    def gather(src_hbm, idx_hbm, out_hbm):
        def body(idx_vmem, out_vmem):
            off = pl.multiple_of(pl.program_id(1) * GATHER_WINDOW, GATHER_WINDOW)
            pltpu.sync_copy(src_hbm.at[idx_vmem.at[0, pl.ds(off, GATHER_WINDOW)]], out_vmem)

        pltpu.emit_pipeline(
            body,
            grid=(n // INDEX_BLOCK, INDEX_SPLIT),
            in_specs=[pl.BlockSpec((1, INDEX_BLOCK), lambda i, j: (0, i))],
            out_specs=[pl.BlockSpec((GATHER_WINDOW, width), lambda i, j: (INDEX_SPLIT * i + j, 0))],
            core_axis_name=("core", "subcore"),
            dimension_semantics=(pltpu.PARALLEL, pltpu.ARBITRARY),
        )(idx_hbm, out_hbm)

    return gather(src, idx.reshape(1, n))


def _scatter_rows(src, dest, n_out):
    n_src, width = src.shape
    assert dest.shape == (TOP_K, n_src) and n_src % GATHER_ROWS == 0
    mesh = plsc.VectorSubcoreMesh(core_axis_name="core", subcore_axis_name="subcore")

    @functools.partial(pl.kernel, out_type=jax.ShapeDtypeStruct((n_out, width), src.dtype),
                       mesh=mesh, scratch_types=[], name="sc_scatter_rows")
    def scatter(src_hbm, idx_hbm, out_hbm):
        def body(src_vmem, idx_vmem):
            off = pl.multiple_of(pl.program_id(1) * GATHER_WINDOW, GATHER_WINDOW)
            for k in range(TOP_K):
                pltpu.sync_copy(src_vmem, out_hbm.at[idx_vmem.at[k, pl.ds(off, GATHER_WINDOW)]])

        pltpu.emit_pipeline(
            body,
            grid=(n_src // INDEX_BLOCK, INDEX_SPLIT),
            in_specs=[pl.BlockSpec((GATHER_WINDOW, width), lambda i, j: (INDEX_SPLIT * i + j, 0)),
                      pl.BlockSpec((TOP_K, INDEX_BLOCK), lambda i, j: (0, i))],
            out_specs=[],
            core_axis_name=("core", "subcore"),
            dimension_semantics=(pltpu.PARALLEL, pltpu.ARBITRARY),
        )(src_hbm, idx_hbm)

    return scatter(src, dest)


GU_GROUP = 256
DOWN_GROUPS = 4


def _expert_kernel(be_ref, nv_ref, nr_ref, x_ref, wgu_ref, bgu_ref, wd_ref, bd_ref, perm_ref,
                   y_ref, wgu_s, wd_s):
    i = pl.program_id(0)
    prev = be_ref[jnp.maximum(i - 1, 0)]
    fresh = (i == 0) | (be_ref[i] != prev)

    @pl.when(fresh & (i < nv_ref[0]))
    def _():
        for c in range(2 * D_EXPERT // GU_GROUP):
            cols = slice(c * GU_GROUP, (c + 1) * GU_GROUP)
            w = wgu_ref[0, :, cols].astype(BF16)
            wgu_s[:, cols] = jnp.dot(w, perm_ref[...], preferred_element_type=F32).astype(BF16)
        wd_s[...] = wd_ref[0].astype(BF16)

    @pl.when(i < nv_ref[0])
    def _():
        half = D_MODEL // 2
        live = lax.broadcasted_iota(jnp.int32, (MOE_BLOCK, 1), 0) < nr_ref[i]
        xa, xb = _unpack_pairs(jnp.where(live, x_ref[...], jnp.uint32(0)))
        x = jnp.concatenate([xa.astype(BF16), xb.astype(BF16)], axis=1)
        hw = GU_GROUP // 2
        n_groups = 2 * D_EXPERT // GU_GROUP

        def gate_up(c):
            return jnp.dot(x, wgu_s[:, c * GU_GROUP:(c + 1) * GU_GROUP],
                           preferred_element_type=F32)

        acc = None
        hcols = []
        gu_next = gate_up(0)
        for c in range(n_groups):
            gu = gu_next + bgu_ref[0, :, c * GU_GROUP:(c + 1) * GU_GROUP]
            if c + 1 < n_groups:
                gu_next = gate_up(c + 1)
            gate = jnp.minimum(gu[:, :hw], SWIGLU_LIMIT)
            up = jnp.clip(gu[:, hw:], -SWIGLU_LIMIT, SWIGLU_LIMIT)
            hcols.append(((up + 1.0) * (gate * _sigmoid(gate * SWIGLU_ALPHA))).astype(BF16))
            if (c + 1) % DOWN_GROUPS == 0:
                lo = c + 1 - DOWN_GROUPS
                part = jnp.dot(jnp.concatenate(hcols[lo:c + 1], axis=1),
                               wd_s[lo * hw:(c + 1) * hw, :], preferred_element_type=F32)
                acc = part if acc is None else acc + part
        y_ref[...] = _pack_pairs(acc + bd_ref[0])

    @pl.when(i >= nv_ref[0])
    def _():
        y_ref[...] = jnp.zeros_like(y_ref)


def _experts(block_e, n_valid, block_rows, x_rows, w_gate_up, b_gu_grouped, w_down, b_down, perm):
    n_rows = x_rows.shape[0]
    n_blocks = n_rows // MOE_BLOCK
    half = D_MODEL // 2
    by_expert = lambda i, be, nv, nr: (be[i], 0, 0)
    grid_spec = pltpu.PrefetchScalarGridSpec(
        num_scalar_prefetch=3,
        grid=(n_blocks,),
        in_specs=[
            pl.BlockSpec((MOE_BLOCK, half), lambda i, be, nv, nr: (i, 0)),
            pl.BlockSpec((1, D_MODEL, 2 * D_EXPERT), by_expert),
            pl.BlockSpec((1, 1, 2 * D_EXPERT), by_expert),
            pl.BlockSpec((1, D_EXPERT, D_MODEL), by_expert),
            pl.BlockSpec((1, 1, D_MODEL), by_expert),
            pl.BlockSpec(perm.shape, lambda i, be, nv, nr: (0, 0)),
        ],
        out_specs=pl.BlockSpec((MOE_BLOCK, half), lambda i, be, nv, nr: (i, 0)),
        scratch_shapes=[
            pltpu.VMEM((D_MODEL, 2 * D_EXPERT), BF16),
            pltpu.VMEM((D_EXPERT, D_MODEL), BF16),
        ],
    )
    return pl.pallas_call(
        _expert_kernel,
        grid_spec=grid_spec,
        out_shape=jax.ShapeDtypeStruct((n_rows, half), jnp.uint32),
        compiler_params=_cparams(("arbitrary",)),
        name="experts",
    )(block_e, n_valid, block_rows, x_rows, w_gate_up, b_gu_grouped, w_down, b_down, perm)


def _combine_kernel(h2_ref, y0_ref, y1_ref, y2_ref, y3_ref, wt_ref, gf_ref, out_ref):
    half = D_MODEL // 2
    wt = wt_ref[...]
    lo = jnp.zeros((h2_ref.shape[0], half), F32)
    hi = jnp.zeros((h2_ref.shape[0], half), F32)
    for k, yk_ref in enumerate((y0_ref, y1_ref, y2_ref, y3_ref)):
        a, b = _unpack_pairs(yk_ref[...])
        lo = lo + wt[:, k:k + 1] * a
        hi = hi + wt[:, k:k + 1] * b
    y = h2_ref[...] + jnp.concatenate([lo, hi], axis=1)
    ms = jnp.mean(y * y, axis=-1, keepdims=True)
    out_ref[...] = y * lax.rsqrt(ms + EPS) * gf_ref[...]


def _combine(h2, y_tok, wts, gain):
    n = h2.shape[0]
    rb = ROW_BLOCK
    rowblk = lambda width: pl.BlockSpec((rb, width), lambda i: (i, 0))
    steps = n // rb
    choice = lambda k: pl.BlockSpec((rb, D_MODEL // 2), lambda i: (k * steps + i, 0))
    return pl.pallas_call(
        _combine_kernel,
        grid=(steps,),
        in_specs=[rowblk(D_MODEL)] + [choice(k) for k in range(TOP_K)]
        + [rowblk(TOP_K), pl.BlockSpec((1, D_MODEL), lambda i: (0, 0))],
        out_specs=rowblk(D_MODEL),
        out_shape=jax.ShapeDtypeStruct((n, D_MODEL), F32),
        compiler_params=_cparams(("parallel",)),
        name="combine_norm",
    )(h2, y_tok, y_tok, y_tok, y_tok, wts, gain)


def _lower_tri(n, strict):
    r = lax.broadcasted_iota(jnp.int32, (n, n), 0)
    c = lax.broadcasted_iota(jnp.int32, (n, n), 1)
    return ((c < r) if strict else (c <= r)).astype(BF16)


def kernel(x, meta_tokens, norm1_gain, w_in, s5_lambda_re, s5_lambda_im, s5_log_dt, s5_b_re,
           s5_b_im, s5_c_re, s5_c_im, s5_d, s5_w_glu, hgrn_lb_logits, hgrn_norm_gain,
           w_branch_s5, w_branch_hgrn, w_out, norm2_gain, w_router, b_router, w_gate_up,
           b_gate_up, w_down, b_down, final_norm_gain):
    nb, seq, d = x.shape
    n_tok = nb * seq
    assert d == D_MODEL and seq % TIME_BLOCK == 0 and n_tok % ROW_BLOCK == 0
    x_rows = x.reshape(n_tok, d)
    w_in_b = w_in[0].astype(BF16)
    g1 = norm1_gain[0].reshape(1, d).astype(F32)

    u, qfvg, gates = _inproj(x_rows, g1, w_in_b, ROW_BLOCK)
    meta_rows = jnp.concatenate(
        [jnp.zeros((META_PAD - N_META, d), F32), meta_tokens.astype(F32)], axis=0)
    u_m, qfvg_m, _ = _inproj(meta_rows, g1, w_in_b, META_PAD)

    bm, cm, tab, ptab = _s5_params(s5_lambda_re[0], s5_lambda_im[0], s5_log_dt[0], s5_b_re[0],
                                   s5_b_im[0], s5_c_re[0], s5_c_im[0])
    y_s5 = _s5_mixer(u.reshape(nb, seq, S5_WIDTH), u_m, bm, cm, tab, ptab,
                     s5_d[0].reshape(1, S5_WIDTH).astype(F32), s5_w_glu[0].astype(BF16))

    lower_bounds = jnp.cumsum(jax.nn.softmax(hgrn_lb_logits.astype(F32), axis=0), axis=0)
    lb = lower_bounds[0].reshape(1, HG_WIDTH)
    y_hg = _hg_mixer(qfvg.reshape(nb, seq, 4 * HG_WIDTH), qfvg_m, lb,
                     hgrn_norm_gain[0].reshape(1, HG_WIDTH).astype(F32),
                     _lower_tri(TIME_BLOCK, strict=False))

    wr = w_router[0].astype(F32).T
    wr_hi = wr.astype(BF16)
    wr_lo = (wr - wr_hi.astype(F32)).astype(BF16)
    br = jnp.broadcast_to(b_router[0].astype(F32)[:, None], (N_EXPERTS, LANES))
    h2, xn_packed, eid, wts, rank, counts = _merge_router(
        x_rows, y_s5.reshape(n_tok, S5_WIDTH), y_hg.reshape(n_tok, HG_WIDTH), gates,
        w_branch_s5[0].astype(BF16), w_branch_hgrn[0].astype(BF16), w_out[0].astype(BF16),
        norm2_gain[0].reshape(1, d).astype(F32), wr_hi, wr_lo, br,
        _lower_tri(ROW_BLOCK, strict=True).T)

    n_assign = n_tok * TOP_K
    n_blocks = n_assign // MOE_BLOCK + N_EXPERTS
    n_rows = n_blocks * MOE_BLOCK
    n_rows_pad = -(-n_rows // GATHER_ROWS) * GATHER_ROWS
    cnt = counts[:, 0].astype(jnp.int32)
    padded = (cnt + MOE_BLOCK - 1) // MOE_BLOCK * MOE_BLOCK
    padded_end = jnp.cumsum(padded)
    padded_start = padded_end - padded
    experts = jnp.arange(N_EXPERTS, dtype=jnp.int32)[None, :, None]
    start_of = jnp.sum(jnp.where(eid[:, None, :] == experts, padded_start[None, :, None], 0), axis=1)
    dest = start_of + rank
    block_start = jnp.arange(n_rows_pad // MOE_BLOCK, dtype=jnp.int32) * MOE_BLOCK
    block_e = jnp.minimum(jnp.sum((block_start[:, None] >= padded_end[None, :]).astype(jnp.int32),
                                  axis=1), N_EXPERTS - 1)
    of_block = block_e[:, None] == jnp.arange(N_EXPERTS, dtype=jnp.int32)[None, :]
    end_of_block = jnp.sum(jnp.where(of_block, (padded_start + cnt)[None, :], 0), axis=1)
    block_rows = jnp.clip(end_of_block - block_start, 0, MOE_BLOCK)
    n_valid = (padded_end[-1] // MOE_BLOCK).astype(jnp.int32).reshape(1)

    x_sorted = _scatter_rows(xn_packed, dest, n_rows_pad)
    hw = GU_GROUP // 2
    pr = jnp.arange(GU_GROUP)
    src = jnp.where(pr < hw, 2 * pr, 2 * (pr - hw) + 1)
    perm = (jnp.arange(GU_GROUP)[:, None] == src[None, :]).astype(BF16)
    b_gu = b_gate_up[0].astype(F32).reshape(N_EXPERTS, 2 * D_EXPERT // GU_GROUP, hw, 2)
    b_gu = b_gu.transpose(0, 1, 3, 2).reshape(N_EXPERTS, 1, 2 * D_EXPERT)
    y_sorted = _experts(block_e, n_valid, block_rows, x_sorted, w_gate_up[0], b_gu, w_down[0],
                        b_down[0].astype(F32).reshape(N_EXPERTS, 1, d), perm)
    y_tok = _gather_rows(y_sorted, dest.reshape(-1))

    out = _combine(h2, y_tok, wts.T, final_norm_gain.reshape(1, d).astype(F32))
    return out.reshape(nb, seq, d)
```

```python
import functools

import jax
import jax.numpy as jnp
from jax import lax
from jax.experimental import pallas as pl
from jax.experimental.pallas import tpu as pltpu
from jax.experimental.pallas import tpu_sc as plsc

F32 = jnp.float32
BF16 = jnp.bfloat16

D_MODEL = 1024
N_META = 16
S5_GROUP = 16
S5_GROUPS = 32
S5_WIDTH = 512
S5_STATE = 64
HG_HEADS = 4
HG_HEAD_DIM = 128
HG_WIDTH = 512
N_EXPERTS = 32
TOP_K = 4
D_EXPERT = 1024
SWIGLU_ALPHA = 1.702
SWIGLU_LIMIT = 7.0
EPS = 1e-6

LANES = 128
SUBLANES = 8
TIME_BLOCK = 256
META_PAD = TIME_BLOCK
ROW_BLOCK = 512
MOE_BLOCK = 512
GATHER_WINDOW = 64
INDEX_BLOCK = 128
INDEX_SPLIT = INDEX_BLOCK // GATHER_WINDOW
SC_SUBCORES = 32
GATHER_ROWS = INDEX_BLOCK * SC_SUBCORES
HG_DIAG = 32
NEG_BIG = -1e30
VMEM_LIMIT = 56 * 1024 * 1024


def _cparams(sem):
    return pltpu.CompilerParams(dimension_semantics=sem, vmem_limit_bytes=VMEM_LIMIT)


def _sigmoid(x):
    return 0.5 * jnp.tanh(0.5 * x) + 0.5


def _silu(x):
    h = 0.5 * x
    return h + h * jnp.tanh(h)


def _inproj_kernel(x_ref, g_ref, w_ref, u_ref, qfvg_ref, gates_ref):
    x = x_ref[...]
    ms = jnp.mean(x * x, axis=-1, keepdims=True)
    xn = (x * lax.rsqrt(ms + EPS) * g_ref[...]).astype(BF16)
    u_ref[...] = jnp.dot(xn, w_ref[:, 0:S5_WIDTH], preferred_element_type=F32).astype(BF16)
    a, b = S5_WIDTH, S5_WIDTH + 4 * HG_WIDTH
    qfvg_ref[...] = jnp.dot(xn, w_ref[:, a:b], preferred_element_type=F32).astype(BF16)
    gates_ref[...] = jnp.dot(xn, w_ref[:, b:], preferred_element_type=F32).astype(BF16)


def _inproj(rows, gain, w_in_bf16, row_block):
    n = rows.shape[0]
    n_gate = 2 * D_MODEL
    return pl.pallas_call(
        _inproj_kernel,
        grid=(n // row_block,),
        in_specs=[
            pl.BlockSpec((row_block, D_MODEL), lambda i: (i, 0)),
            pl.BlockSpec((1, D_MODEL), lambda i: (0, 0)),
            pl.BlockSpec(w_in_bf16.shape, lambda i: (0, 0)),
        ],
        out_specs=[
            pl.BlockSpec((row_block, S5_WIDTH), lambda i: (i, 0)),
            pl.BlockSpec((row_block, 4 * HG_WIDTH), lambda i: (i, 0)),
            pl.BlockSpec((row_block, n_gate), lambda i: (i, 0)),
        ],
        out_shape=[
            jax.ShapeDtypeStruct((n, S5_WIDTH), BF16),
            jax.ShapeDtypeStruct((n, 4 * HG_WIDTH), BF16),
            jax.ShapeDtypeStruct((n, n_gate), BF16),
        ],
        compiler_params=_cparams(("parallel",)),
        name="inproj",
    )(rows, gain, w_in_bf16)


N_STATE = S5_GROUPS * S5_STATE
HALF_STATE = N_STATE // 2
SCAN_GROUPS_PER_HALF = HALF_STATE // LANES
SEG_COUNT = SUBLANES
SEG_LEN = TIME_BLOCK // SEG_COUNT


def _s5_scan_group(xs, xb, tab, ptab, car_ref, g, chains, tb):
    sub = lax.broadcasted_iota(jnp.int32, (SUBLANES, LANES), 0)
    zero = jnp.zeros((SUBLANES, LANES), F32)
    re, im = slice(0, LANES), slice(LANES, 2 * LANES)

    def rows_of(b, tau):
        return slice(b * tb + tau * SUBLANES, b * tb + (tau + 1) * SUBLANES)

    ends = [(xs[rows_of(b, SEG_LEN - 1), re], xs[rows_of(b, SEG_LEN - 1), im]) for b in chains]
    for tau in range(SEG_LEN - 1):
        p_rows = slice((SEG_LEN - 2 - tau) * SUBLANES, (SEG_LEN - 1 - tau) * SUBLANES)
        pr, pi = ptab[0, p_rows, :], ptab[1, p_rows, :]
        vals = [(xs[rows_of(b, tau), re], xs[rows_of(b, tau), im]) for b in chains]
        ends = [(pr * br - pi * bi + er, pr * bi + pi * br + ei)
                for (er, ei), (br, bi) in zip(ends, vals)]

    m = [tab[i] for i in range(8)]
    starts = []
    for b, (er, ei) in zip(chains, ends):
        gr = jnp.where(sub == 0, car_ref[b, g, :, re], pltpu.roll(er, 1, 0))
        gi = jnp.where(sub == 0, car_ref[b, g, :, im], pltpu.roll(ei, 1, 0))
        for q, shift in enumerate((1, 2, 4)):
            ar, ai = m[2 * q], m[2 * q + 1]
            rr = pltpu.roll(gr, shift, 0)
            ri = pltpu.roll(gi, shift, 0)
            gr, gi = gr + ar * rr - ai * ri, gi + ar * ri + ai * rr
        starts.append((gr, gi))
        xr = m[6] * gr - m[7] * gi + er
        xi = m[6] * gi + m[7] * gr + ei
        car_ref[b, g, :, re] = jnp.broadcast_to(xr[SUBLANES - 1:SUBLANES, :], (SUBLANES, LANES))
        car_ref[b, g, :, im] = jnp.broadcast_to(xi[SUBLANES - 1:SUBLANES, :], (SUBLANES, LANES))

    if xb is not None:
        lr, li = tab[8], tab[9]
        state = starts
        prev = None
        for tau in range(SEG_LEN):
            vals = [(xs[rows_of(b, tau), re], xs[rows_of(b, tau), im]) for b in chains]
            state = [(lr * xr - li * xi + br, lr * xi + li * xr + bi)
                     for (xr, xi), (br, bi) in zip(state, vals)]
            if tau % 2 == 0:
                prev = state
                continue
            for b, (xr0, xi0), (xr1, xi1) in zip(chains, prev, state):
                rows = slice(b * tb + (tau - 1) * SUBLANES, b * tb + (tau + 1) * SUBLANES)
                xb[rows, re] = jnp.concatenate([xr0, xr1], axis=0).astype(BF16)
                xb[rows, im] = jnp.concatenate([xi0, xi1], axis=0).astype(BF16)


def _s5_kernel(u_ref, um_ref, perm_ref, unperm_ref, bm_ref, cm_ref, tab_ref, ptab_ref, d_ref,
               wglu_ref, y_ref, car_ref, up_ref, xs_ref, xb_ref, yacc_ref, *, nb, tb):
    half_ch = S5_WIDTH // 2

    def permute_in(u, b):
        up = jnp.dot(perm_ref[...], u, preferred_element_type=F32).astype(BF16)
        for hf in range(2):
            up_ref[hf, b * tb:(b + 1) * tb, :] = up[:, hf * half_ch:(hf + 1) * half_ch]

    def project_in(g, slot, n_rows):
        xs_ref[slot, 0:n_rows, :] = jnp.dot(up_ref[g // SCAN_GROUPS_PER_HALF, 0:n_rows, :],
                                            bm_ref[g], preferred_element_type=F32)

    def project_out(g0):
        c_pair = cm_ref[pl.ds(g0, 2)].reshape(4 * LANES, half_ch)
        yacc_ref[g0 // SCAN_GROUPS_PER_HALF] += jnp.dot(xb_ref[...], c_pair,
                                                        preferred_element_type=F32)

    def scan(g, slot, chains, finish):
        xb = xb_ref.at[:, slot * 2 * LANES:(slot + 1) * 2 * LANES] if finish else None
        _s5_scan_group(xs_ref.at[slot], xb, tab_ref.at[g], ptab_ref.at[g], car_ref, g, chains, tb)

    @pl.when(pl.program_id(0) == 0)
    def _():
        car_ref[...] = jnp.zeros_like(car_ref)
        permute_in(um_ref[...], 0)

        def meta_trip(g, carry):
            project_in(g, 0, tb)
            scan(g, 0, [0], False)
            return carry

        lax.fori_loop(0, N_STATE // LANES, meta_trip, 0)
        for b in range(1, nb):
            car_ref[b] = car_ref[0]

    for b in range(nb):
        permute_in(u_ref[b], b)
    yacc_ref[...] = jnp.zeros_like(yacc_ref)

    def trip(t, carry):
        g0 = 2 * t
        project_in(g0, 0, nb * tb)
        project_in(g0 + 1, 1, nb * tb)
        scan(g0, 0, list(range(nb)), True)
        scan(g0 + 1, 1, list(range(nb)), True)
        project_out(g0)
        return carry

    lax.fori_loop(0, N_STATE // LANES // 2, trip, 0)

    for b in range(nb):
        rows = slice(b * tb, (b + 1) * tb)
        y = jnp.concatenate([yacc_ref[0, rows, :], yacc_ref[1, rows, :]], axis=1)
        up = jnp.concatenate([up_ref[0, rows, :], up_ref[1, rows, :]], axis=1)
        y = y + d_ref[...] * up.astype(F32)
        y = 0.5 * y * (1.0 + jnp.tanh(0.7978845608028654 * (y + 0.044715 * (y * y * y))))
        z = jnp.dot(y.astype(BF16), wglu_ref[...], preferred_element_type=F32)
        out = (y * _sigmoid(z)).astype(BF16)
        y_ref[b] = jnp.dot(unperm_ref[...], out, preferred_element_type=F32).astype(BF16)


def _s5_mixer(u, u_meta, bm, cm, tab, ptab, d_skip, w_glu_bf16):
    nb, seq, _ = u.shape
    tb = TIME_BLOCK
    r = jnp.arange(tb)
    perm = (((r % SEG_COUNT) * SEG_LEN + r // SEG_COUNT)[:, None] == r[None, :]).astype(BF16)
    kern = functools.partial(_s5_kernel, nb=nb, tb=tb)
    const2 = lambda t: (0, 0)
    const3 = lambda t: (0, 0, 0)
    const4 = lambda t: (0, 0, 0, 0)
    n_groups = N_STATE // LANES
    half_ch = S5_WIDTH // 2
    return pl.pallas_call(
        kern,
        grid=(seq // tb,),
        in_specs=[
            pl.BlockSpec((nb, tb, S5_WIDTH), lambda t: (0, t, 0)),
            pl.BlockSpec(u_meta.shape, const2),
            pl.BlockSpec(perm.shape, const2),
            pl.BlockSpec(perm.shape, const2),
            pl.BlockSpec(bm.shape, const3),
            pl.BlockSpec(cm.shape, const3),
            pl.BlockSpec(tab.shape, const4),
            pl.BlockSpec(ptab.shape, const4),
            pl.BlockSpec(d_skip.shape, const2),
            pl.BlockSpec(w_glu_bf16.shape, const2),
        ],
        out_specs=pl.BlockSpec((nb, tb, S5_WIDTH), lambda t: (0, t, 0)),
        out_shape=jax.ShapeDtypeStruct((nb, seq, S5_WIDTH), BF16),
        scratch_shapes=[pltpu.VMEM((nb, n_groups, SUBLANES, 2 * LANES), F32),
                        pltpu.VMEM((2, nb * tb, half_ch), BF16),
                        pltpu.VMEM((2, nb * tb, 2 * LANES), F32),
                        pltpu.VMEM((nb * tb, 4 * LANES), BF16),
                        pltpu.VMEM((2, nb * tb, half_ch), F32)],
        compiler_params=_cparams(("arbitrary",)),
        name="s5_mixer",
    )(u, u_meta, perm, perm.T, bm, cm, tab, ptab, d_skip, w_glu_bf16)


def _s5_params(lam_re, lam_im, log_dt, b_re, b_im, c_re, c_im):
    lam = lax.complex(lam_re.astype(F32), lam_im.astype(F32))
    dt = jnp.exp(log_dt.astype(F32))[:, None]
    lam_dt = lam * dt
    lam_bar = jnp.exp(lam_dt)
    b_bar = ((lam_bar - 1.0) / lam)[:, :, None] * lax.complex(b_re.astype(F32), b_im.astype(F32))
    gl = S5_GROUPS // 2
    eye = jnp.eye(gl, dtype=F32)

    def in_half(bh):
        def blk(part):
            t = jnp.einsum('gph,gk->ghkp', part, eye)
            return t.reshape(gl * S5_GROUP, gl * S5_STATE)
        return jnp.concatenate([blk(jnp.real(bh)), blk(jnp.imag(bh))], axis=1)

    def out_half(cr, ci):
        def blk(part):
            t = jnp.einsum('ghp,gk->gpkh', part, eye)
            return t.reshape(gl * S5_STATE, gl * S5_GROUP)
        return jnp.concatenate([blk(cr), blk(-ci)], axis=0)

    bm = jnp.stack([in_half(b_bar[:gl]), in_half(b_bar[gl:])]).astype(BF16)
    cm = jnp.stack([out_half(c_re[:gl].astype(F32), c_im[:gl].astype(F32)),
                    out_half(c_re[gl:].astype(F32), c_im[gl:].astype(F32))]).astype(BF16)
    lam_flat = lam_dt.reshape(1, N_STATE)
    rows = jnp.arange(SUBLANES, dtype=F32)[:, None]
    tabs = []
    for shift in (1, 2, 4):
        p = jnp.exp(lam_flat * float(SEG_LEN * shift)) * (rows >= shift).astype(F32)
        tabs += [jnp.real(p), jnp.imag(p)]
    for p in (jnp.exp(lam_flat * float(SEG_LEN)), jnp.exp(lam_flat)):
        tabs += [jnp.real(p), jnp.imag(p)]
    tab = jnp.stack([jnp.broadcast_to(t, (SUBLANES, N_STATE)) for t in tabs]).astype(F32)
    steps = (jnp.arange(SEG_LEN * SUBLANES) // SUBLANES + 1).astype(F32)[:, None]
    p = jnp.exp(lam_flat * steps)
    ptab = jnp.stack([jnp.real(p), jnp.imag(p)]).astype(F32)
    ng = SCAN_GROUPS_PER_HALF
    bm = bm.reshape(2, gl * S5_GROUP, 2, ng, LANES).transpose(0, 3, 1, 2, 4)
    bm = bm.reshape(2 * ng, gl * S5_GROUP, 2 * LANES)
    cm = cm.reshape(2, 2, ng, LANES, gl * S5_GROUP).transpose(0, 2, 1, 3, 4)
    cm = cm.reshape(2 * ng, 2 * LANES, gl * S5_GROUP)
    tab = tab.reshape(tab.shape[0], SUBLANES, 2 * ng, LANES).transpose(2, 0, 1, 3)
    ptab = ptab.reshape(2, SEG_LEN * SUBLANES, 2 * ng, LANES).transpose(2, 0, 1, 3)
    return bm, cm, tab, ptab


def _dot_nt(a, b):
    return lax.dot_general(a, b, (((1,), (1,)), ((), ())), preferred_element_type=F32)


def _dot_tn(a, b):
    return lax.dot_general(a, b, (((0,), (0,)), ((), ())), preferred_element_type=F32)


def _cumsum_rows(tri, x):
    hi = x.astype(BF16)
    r1 = x - hi.astype(F32)
    mid = r1.astype(BF16)
    lo = (r1 - mid.astype(F32)).astype(BF16)
    return (jnp.dot(tri, hi, preferred_element_type=F32)
            + jnp.dot(tri, mid, preferred_element_type=F32)
            + jnp.dot(tri, lo, preferred_element_type=F32))


def _segment_rows(x, seg, pos):
    n = x.shape[0]
    parts = [jnp.broadcast_to(x[s * seg + pos:s * seg + pos + 1, :], (seg, x.shape[1]))
             for s in range(n // seg)]
    return parts[0] if len(parts) == 1 else jnp.concatenate(parts, axis=0)


def _hg_gates(x, lb, row_valid=None):
    w = HG_WIDTH
    q = x[:, 0:w]
    f = lb + (1.0 - lb) * _sigmoid(x[:, w:2 * w])
    logf = jnp.log(f)
    k = 1.0 - f
    if row_valid is not None:
        logf = jnp.where(row_valid, logf, 0.0)
        k = jnp.where(row_valid, k, 0.0)
    return _silu(q), k, logf


def _hg_kernel(x_ref, xm_ref, lb_ref, gain_ref, tri_ref, y_ref, st_ref, *, nb, tb):
    w, dh = HG_WIDTH, HG_HEAD_DIM
    lb = lb_ref[...]
    tri = tri_ref[...]

    @pl.when(pl.program_id(0) == 0)
    def _():
        xm = xm_ref[...].astype(F32)
        valid = lax.broadcasted_iota(jnp.int32, (META_PAD, 1), 0) >= (META_PAD - N_META)
        _, k, logf = _hg_gates(xm, lb, valid)
        bc = _cumsum_rows(tri[0:META_PAD, 0:META_PAD], logf)
        ki = (k * jnp.exp(bc[META_PAD - 1:META_PAD, :] - bc)).astype(BF16)
        v = xm[:, 2 * w:3 * w].astype(BF16)
        for h in range(HG_HEADS):
            sl = slice(h * dh, (h + 1) * dh)
            s0 = _dot_tn(v[:, sl], ki[:, sl])
            for b in range(nb):
                st_ref[b, h] = s0

    ri = lax.broadcasted_iota(jnp.int32, (tb, tb), 0)
    ci = lax.broadcasted_iota(jnp.int32, (tb, tb), 1)
    diag_shift = HG_DIAG.bit_length() - 1
    diag_mask = ((ri >> diag_shift) == (ci >> diag_shift)) & (ci <= ri)
    levels = []
    seg = 2 * HG_DIAG
    while seg <= tb:
        levels.append(seg)
        seg *= 2
    seg_masks = [None if s == tb else
                 ((ri >> (s.bit_length() - 1)) == (ci >> (s.bit_length() - 1))).astype(F32)
                 for s in levels]
    row = lax.broadcasted_iota(jnp.int32, (tb, 1), 0)

    for b in range(nb):
        x = x_ref[b].astype(F32)
        qs, k, logf = _hg_gates(x, lb)
        v = x[:, 2 * w:3 * w].astype(BF16)
        g = x[:, 3 * w:4 * w]
        bc = _cumsum_rows(tri, logf)
        dlt = bc - _segment_rows(bc, HG_DIAG, HG_DIAG // 2 - 1)
        qk = [((qs * jnp.exp(dlt)).astype(BF16), (k * jnp.exp(-dlt)).astype(BF16))]
        for s in levels:
            e = jnp.exp(-jnp.abs(bc - _segment_rows(bc, s, s // 2 - 1)))
            upper = (row & (s - 1)) >= (s // 2)
            qk.append((jnp.where(upper, qs * e, 0.0).astype(BF16),
                       jnp.where(upper, 0.0, k * e).astype(BF16)))
        b_last = bc[tb - 1:tb, :]
        qi = (qs * jnp.exp(bc)).astype(BF16)
        ki = (k * jnp.exp(b_last - bc)).astype(BF16)
        dec = jnp.exp(b_last)
        outs = []
        for h in range(HG_HEADS):
            sl = slice(h * dh, (h + 1) * dh)
            sc = jnp.where(diag_mask, _dot_nt(qk[0][0][:, sl], qk[0][1][:, sl]), 0.0)
            for (ql, kl), m in zip(qk[1:], seg_masks):
                t = _dot_nt(ql[:, sl], kl[:, sl])
                sc = sc + (t if m is None else t * m)
            st = st_ref[b, h]
            o = (jnp.dot(sc.astype(BF16), v[:, sl], preferred_element_type=F32)
                 + _dot_nt(qi[:, sl], st.astype(BF16)))
            st_ref[b, h] = dec[:, sl] * st + _dot_tn(v[:, sl], ki[:, sl])
            ms = jnp.mean(o * o, axis=-1, keepdims=True)
            outs.append(o * lax.rsqrt(ms + EPS))
        o = jnp.concatenate(outs, axis=1) * gain_ref[...]
        y_ref[b] = (o * _silu(g)).astype(BF16)


def _hg_mixer(x, x_meta, lb, gain, tri):
    nb, seq, _ = x.shape
    tb = TIME_BLOCK
    kern = functools.partial(_hg_kernel, nb=nb, tb=tb)
    const2 = lambda t: (0, 0)
    return pl.pallas_call(
        kern,
        grid=(seq // tb,),
        in_specs=[
            pl.BlockSpec((nb, tb, 4 * HG_WIDTH), lambda t: (0, t, 0)),
            pl.BlockSpec(x_meta.shape, const2),
            pl.BlockSpec(lb.shape, const2),
            pl.BlockSpec(gain.shape, const2),
            pl.BlockSpec(tri.shape, const2),
        ],
        out_specs=pl.BlockSpec((nb, tb, HG_WIDTH), lambda t: (0, t, 0)),
        out_shape=jax.ShapeDtypeStruct((nb, seq, HG_WIDTH), BF16),
        scratch_shapes=[pltpu.VMEM((nb, HG_HEADS, HG_HEAD_DIM, HG_HEAD_DIM), F32)],
        compiler_params=_cparams(("arbitrary",)),
        name="hgrn2_mixer",
    )(x, x_meta, lb, gain, tri)


def _split3(x):
    hi = x.astype(BF16)
    lo = (x - hi.astype(F32)).astype(BF16)
    return hi, lo


def _pack_pairs(y):
    n = y.shape[1] // 2
    lo = pltpu.bitcast(y[:, :n].astype(BF16).astype(F32), jnp.uint32)
    hi = pltpu.bitcast(y[:, n:].astype(BF16).astype(F32), jnp.uint32)
    return (lo >> 16) | (hi & jnp.uint32(0xFFFF0000))


def _unpack_pairs(p):
    lo = pltpu.bitcast(p << 16, F32)
    hi = pltpu.bitcast(p & jnp.uint32(0xFFFF0000), F32)
    return lo, hi


def _merge_kernel(x_ref, ys_ref, yh_ref, gt_ref, wbs_ref, wbh_ref, wo_ref, g2_ref, wr_hi_ref,
                  wr_lo_ref, br_ref, tri_ref, h2_ref, xp_ref, eid_ref, wt_ref, rank_ref, cnt_ref,
                  run_ref):
    @pl.when(pl.program_id(0) == 0)
    def _():
        run_ref[...] = jnp.zeros_like(run_ref)

    gt = gt_ref[...].astype(F32)
    gs = _sigmoid(gt[:, :D_MODEL])
    gh = _sigmoid(gt[:, D_MODEL:])
    merged = (gs * jnp.dot(ys_ref[...], wbs_ref[...], preferred_element_type=F32)
              + gh * jnp.dot(yh_ref[...], wbh_ref[...], preferred_element_type=F32))
    h2 = x_ref[...] + jnp.dot(merged.astype(BF16), wo_ref[...], preferred_element_type=F32)
    h2_ref[...] = h2
    ms = jnp.mean(h2 * h2, axis=-1, keepdims=True)
    xn = h2 * lax.rsqrt(ms + EPS) * g2_ref[...]
    xp_ref[...] = _pack_pairs(xn)

    x_hi, x_lo = _split3(xn)
    logits = (_dot_nt(wr_hi_ref[...], x_hi) + _dot_nt(wr_lo_ref[...], x_hi)
              + _dot_nt(wr_hi_ref[...], x_lo)) + br_ref[:, 0:1]
    rows = logits.shape[1]
    expert = lax.broadcasted_iota(jnp.int32, (N_EXPERTS, rows), 0)
    tops, hots, ids = [], [], []
    sel = jnp.zeros((N_EXPERTS, rows), F32)
    for k in range(TOP_K):
        m = jnp.max(logits, axis=0, keepdims=True)
        idx = jnp.min(jnp.where(logits == m, expert, N_EXPERTS), axis=0, keepdims=True)
        hot = expert == idx
        logits = jnp.where(hot, NEG_BIG, logits)
        tops.append(m)
        hots.append(hot)
        ids.append(idx)
        sel = sel + hot.astype(F32)
    es = [jnp.exp(m - tops[0]) for m in tops]
    tot = es[0] + es[1] + es[2] + es[3]
    run = run_ref[...]
    prefix = jnp.dot(sel.astype(BF16), tri_ref[...], preferred_element_type=F32) + run[:, 0:1]
    ranks = [jnp.sum(jnp.where(hot, prefix, 0.0), axis=0, keepdims=True) for hot in hots]
    eid_ref[...] = jnp.concatenate(ids, axis=0)
    wt_ref[...] = jnp.concatenate([e / tot for e in es], axis=0)
    rank_ref[...] = jnp.concatenate(ranks, axis=0).astype(jnp.int32)
    run = run + jnp.sum(sel, axis=1, keepdims=True)
    run_ref[...] = run
    cnt_ref[...] = run


def _merge_router(x_rows, y_s5, y_hg, gates, wbs, wbh, wo, g2, wr_hi, wr_lo, br, tri_strict):
    n = x_rows.shape[0]
    rb = ROW_BLOCK
    const2 = lambda i: (0, 0)
    rowblk = lambda width: pl.BlockSpec((rb, width), lambda i: (i, 0))
    choice = pl.BlockSpec((TOP_K, rb), lambda i: (0, i))
    return pl.pallas_call(
        _merge_kernel,
        grid=(n // rb,),
        in_specs=[
            rowblk(D_MODEL), rowblk(S5_WIDTH), rowblk(HG_WIDTH), rowblk(2 * D_MODEL),
            pl.BlockSpec(wbs.shape, const2), pl.BlockSpec(wbh.shape, const2),
            pl.BlockSpec(wo.shape, const2), pl.BlockSpec(g2.shape, const2),
            pl.BlockSpec(wr_hi.shape, const2), pl.BlockSpec(wr_lo.shape, const2),
            pl.BlockSpec(br.shape, const2), pl.BlockSpec(tri_strict.shape, const2),
        ],
        out_specs=[
            rowblk(D_MODEL), rowblk(D_MODEL // 2), choice, choice, choice,
            pl.BlockSpec((N_EXPERTS, LANES), const2),
        ],
        out_shape=[
            jax.ShapeDtypeStruct((n, D_MODEL), F32),
            jax.ShapeDtypeStruct((n, D_MODEL // 2), jnp.uint32),
            jax.ShapeDtypeStruct((TOP_K, n), jnp.int32),
            jax.ShapeDtypeStruct((TOP_K, n), F32),
            jax.ShapeDtypeStruct((TOP_K, n), jnp.int32),
            jax.ShapeDtypeStruct((N_EXPERTS, LANES), F32),
        ],
        scratch_shapes=[pltpu.VMEM((N_EXPERTS, LANES), F32)],
        compiler_params=_cparams(("arbitrary",)),
        name="merge_router",
    )(x_rows, y_s5, y_hg, gates, wbs, wbh, wo, g2, wr_hi, wr_lo, br, tri_strict)


def _gather_rows(src, idx):
    n = idx.shape[0]
    width = src.shape[1]
    assert n % GATHER_ROWS == 0
    mesh = plsc.VectorSubcoreMesh(core_axis_name="core", subcore_axis_name="subcore")

    @functools.partial(pl.kernel, out_type=jax.ShapeDtypeStruct((n, width), src.dtype),
                       mesh=mesh, scratch_types=[], name="sc_gather_rows")
    def gather(src_hbm, idx_hbm, out_hbm):
        def body(idx_vmem, out_vmem):
            off = pl.multiple_of(pl.program_id(1) * GATHER_WINDOW, GATHER_WINDOW)
            pltpu.sync_copy(src_hbm.at[idx_vmem.at[0, pl.ds(off, GATHER_WINDOW)]], out_vmem)

        pltpu.emit_pipeline(
            body,
            grid=(n // INDEX_BLOCK, INDEX_SPLIT),
            in_specs=[pl.BlockSpec((1, INDEX_BLOCK), lambda i, j: (0, i))],
            out_specs=[pl.BlockSpec((GATHER_WINDOW, width), lambda i, j: (INDEX_SPLIT * i + j, 0))],
            core_axis_name=("core", "subcore"),
            dimension_semantics=(pltpu.PARALLEL, pltpu.ARBITRARY),
        )(idx_hbm, out_hbm)

    return gather(src, idx.reshape(1, n))


def _scatter_rows(src, dest, n_out):
    n_src, width = src.shape
    assert dest.shape == (TOP_K, n_src) and n_src % GATHER_ROWS == 0
    mesh = plsc.VectorSubcoreMesh(core_axis_name="core", subcore_axis_name="subcore")

    @functools.partial(pl.kernel, out_type=jax.ShapeDtypeStruct((n_out, width), src.dtype),
                       mesh=mesh, scratch_types=[], name="sc_scatter_rows")
    def scatter(src_hbm, idx_hbm, out_hbm):
        def body(src_vmem, idx_vmem):
            off = pl.multiple_of(pl.program_id(1) * GATHER_WINDOW, GATHER_WINDOW)
            for k in range(TOP_K):
                pltpu.sync_copy(src_vmem, out_hbm.at[idx_vmem.at[k, pl.ds(off, GATHER_WINDOW)]])

        pltpu.emit_pipeline(
            body,
            grid=(n_src // INDEX_BLOCK, INDEX_SPLIT),
            in_specs=[pl.BlockSpec((GATHER_WINDOW, width), lambda i, j: (INDEX_SPLIT * i + j, 0)),
                      pl.BlockSpec((TOP_K, INDEX_BLOCK), lambda i, j: (0, i))],
            out_specs=[],
            core_axis_name=("core", "subcore"),
            dimension_semantics=(pltpu.PARALLEL, pltpu.ARBITRARY),
        )(src_hbm, idx_hbm)

    return scatter(src, dest)


GU_GROUP = 256
DOWN_GROUPS = 4


def _expert_kernel(be_ref, nv_ref, nr_ref, x_ref, wgu_ref, bgu_ref, wd_ref, bd_ref, perm_ref,
                   y_ref, wgu_s, wd_s):
    i = pl.program_id(0)
    prev = be_ref[jnp.maximum(i - 1, 0)]
    fresh = (i == 0) | (be_ref[i] != prev)

    @pl.when(fresh & (i < nv_ref[0]))
    def _():
        for c in range(2 * D_EXPERT // GU_GROUP):
            cols = slice(c * GU_GROUP, (c + 1) * GU_GROUP)
            w = wgu_ref[0, :, cols].astype(BF16)
            wgu_s[:, cols] = jnp.dot(w, perm_ref[...], preferred_element_type=F32).astype(BF16)
        wd_s[...] = wd_ref[0].astype(BF16)

    def block(rows):
        live = lax.broadcasted_iota(jnp.int32, (rows, 1), 0) < nr_ref[i]
        xa, xb = _unpack_pairs(jnp.where(live, x_ref[0:rows, :], jnp.uint32(0)))
        x = jnp.concatenate([xa.astype(BF16), xb.astype(BF16)], axis=1)
        hw = GU_GROUP // 2
        n_groups = 2 * D_EXPERT // GU_GROUP

        def gate_up(c):
            return jnp.dot(x, wgu_s[:, c * GU_GROUP:(c + 1) * GU_GROUP],
                           preferred_element_type=F32)

        acc = None
        hcols = []
        gu_next = gate_up(0)
        for c in range(n_groups):
            gu = gu_next + bgu_ref[0, :, c * GU_GROUP:(c + 1) * GU_GROUP]
            if c + 1 < n_groups:
                gu_next = gate_up(c + 1)
            gate = jnp.minimum(gu[:, :hw], SWIGLU_LIMIT)
            up = jnp.clip(gu[:, hw:], -SWIGLU_LIMIT, SWIGLU_LIMIT)
            hcols.append(((up + 1.0) * (gate * _sigmoid(gate * SWIGLU_ALPHA))).astype(BF16))
            if (c + 1) % DOWN_GROUPS == 0:
                lo = c + 1 - DOWN_GROUPS
                part = jnp.dot(jnp.concatenate(hcols[lo:c + 1], axis=1),
                               wd_s[lo * hw:(c + 1) * hw, :], preferred_element_type=F32)
                acc = part if acc is None else acc + part
        y_ref[0:rows, :] = _pack_pairs(acc + bd_ref[0])

    short = nr_ref[i] <= MOE_BLOCK // 2

    @pl.when((i < nv_ref[0]) & jnp.logical_not(short))
    def _():
        block(MOE_BLOCK)

    @pl.when((i < nv_ref[0]) & short)
    def _():
        block(MOE_BLOCK // 2)
        y_ref[MOE_BLOCK // 2:, :] = jnp.zeros((MOE_BLOCK // 2, D_MODEL // 2), jnp.uint32)

    @pl.when(i >= nv_ref[0])
    def _():
        y_ref[...] = jnp.zeros_like(y_ref)


def _experts(block_e, n_valid, block_rows, x_rows, w_gate_up, b_gu_grouped, w_down, b_down, perm):
    n_rows = x_rows.shape[0]
    n_blocks = n_rows // MOE_BLOCK
    half = D_MODEL // 2
    by_expert = lambda i, be, nv, nr: (be[i], 0, 0)
    grid_spec = pltpu.PrefetchScalarGridSpec(
        num_scalar_prefetch=3,
        grid=(n_blocks,),
        in_specs=[
            pl.BlockSpec((MOE_BLOCK, half), lambda i, be, nv, nr: (i, 0)),
            pl.BlockSpec((1, D_MODEL, 2 * D_EXPERT), by_expert),
            pl.BlockSpec((1, 1, 2 * D_EXPERT), by_expert),
            pl.BlockSpec((1, D_EXPERT, D_MODEL), by_expert),
            pl.BlockSpec((1, 1, D_MODEL), by_expert),
            pl.BlockSpec(perm.shape, lambda i, be, nv, nr: (0, 0)),
        ],
        out_specs=pl.BlockSpec((MOE_BLOCK, half), lambda i, be, nv, nr: (i, 0)),
        scratch_shapes=[
            pltpu.VMEM((D_MODEL, 2 * D_EXPERT), BF16),
            pltpu.VMEM((D_EXPERT, D_MODEL), BF16),
        ],
    )
    return pl.pallas_call(
        _expert_kernel,
        grid_spec=grid_spec,
        out_shape=jax.ShapeDtypeStruct((n_rows, half), jnp.uint32),
        compiler_params=_cparams(("arbitrary",)),
        name="experts",
    )(block_e, n_valid, block_rows, x_rows, w_gate_up, b_gu_grouped, w_down, b_down, perm)


def _combine_kernel(h2_ref, y0_ref, y1_ref, y2_ref, y3_ref, wt_ref, gf_ref, out_ref):
    half = D_MODEL // 2
    wt = wt_ref[...]
    lo = jnp.zeros((h2_ref.shape[0], half), F32)
    hi = jnp.zeros((h2_ref.shape[0], half), F32)
    for k, yk_ref in enumerate((y0_ref, y1_ref, y2_ref, y3_ref)):
        a, b = _unpack_pairs(yk_ref[...])
        lo = lo + wt[:, k:k + 1] * a
        hi = hi + wt[:, k:k + 1] * b
    y = h2_ref[...] + jnp.concatenate([lo, hi], axis=1)
    ms = jnp.mean(y * y, axis=-1, keepdims=True)
    out_ref[...] = y * lax.rsqrt(ms + EPS) * gf_ref[...]


def _combine(h2, y_tok, wts, gain):
    n = h2.shape[0]
    rb = ROW_BLOCK
    rowblk = lambda width: pl.BlockSpec((rb, width), lambda i: (i, 0))
    steps = n // rb
    choice = lambda k: pl.BlockSpec((rb, D_MODEL // 2), lambda i: (k * steps + i, 0))
    return pl.pallas_call(
        _combine_kernel,
        grid=(steps,),
        in_specs=[rowblk(D_MODEL)] + [choice(k) for k in range(TOP_K)]
        + [rowblk(TOP_K), pl.BlockSpec((1, D_MODEL), lambda i: (0, 0))],
        out_specs=rowblk(D_MODEL),
        out_shape=jax.ShapeDtypeStruct((n, D_MODEL), F32),
        compiler_params=_cparams(("parallel",)),
        name="combine_norm",
    )(h2, y_tok, y_tok, y_tok, y_tok, wts, gain)


def _lower_tri(n, strict):
    r = lax.broadcasted_iota(jnp.int32, (n, n), 0)
    c = lax.broadcasted_iota(jnp.int32, (n, n), 1)
    return ((c < r) if strict else (c <= r)).astype(BF16)


def kernel(x, meta_tokens, norm1_gain, w_in, s5_lambda_re, s5_lambda_im, s5_log_dt, s5_b_re,
           s5_b_im, s5_c_re, s5_c_im, s5_d, s5_w_glu, hgrn_lb_logits, hgrn_norm_gain,
           w_branch_s5, w_branch_hgrn, w_out, norm2_gain, w_router, b_router, w_gate_up,
           b_gate_up, w_down, b_down, final_norm_gain):
    nb, seq, d = x.shape
    n_tok = nb * seq
    assert d == D_MODEL and seq % TIME_BLOCK == 0 and n_tok % ROW_BLOCK == 0
    x_rows = x.reshape(n_tok, d)
    w_in_b = w_in[0].astype(BF16)
    g1 = norm1_gain[0].reshape(1, d).astype(F32)

    u, qfvg, gates = _inproj(x_rows, g1, w_in_b, ROW_BLOCK)
    meta_rows = jnp.concatenate(
        [jnp.zeros((META_PAD - N_META, d), F32), meta_tokens.astype(F32)], axis=0)
    u_m, qfvg_m, _ = _inproj(meta_rows, g1, w_in_b, META_PAD)

    bm, cm, tab, ptab = _s5_params(s5_lambda_re[0], s5_lambda_im[0], s5_log_dt[0], s5_b_re[0],
                                   s5_b_im[0], s5_c_re[0], s5_c_im[0])
    y_s5 = _s5_mixer(u.reshape(nb, seq, S5_WIDTH), u_m, bm, cm, tab, ptab,
                     s5_d[0].reshape(1, S5_WIDTH).astype(F32), s5_w_glu[0].astype(BF16))

    lower_bounds = jnp.cumsum(jax.nn.softmax(hgrn_lb_logits.astype(F32), axis=0), axis=0)
    lb = lower_bounds[0].reshape(1, HG_WIDTH)
    y_hg = _hg_mixer(qfvg.reshape(nb, seq, 4 * HG_WIDTH), qfvg_m, lb,
                     hgrn_norm_gain[0].reshape(1, HG_WIDTH).astype(F32),
                     _lower_tri(TIME_BLOCK, strict=False))

    wr = w_router[0].astype(F32).T
    wr_hi = wr.astype(BF16)
    wr_lo = (wr - wr_hi.astype(F32)).astype(BF16)
    br = jnp.broadcast_to(b_router[0].astype(F32)[:, None], (N_EXPERTS, LANES))
    h2, xn_packed, eid, wts, rank, counts = _merge_router(
        x_rows, y_s5.reshape(n_tok, S5_WIDTH), y_hg.reshape(n_tok, HG_WIDTH), gates,
        w_branch_s5[0].astype(BF16), w_branch_hgrn[0].astype(BF16), w_out[0].astype(BF16),
        norm2_gain[0].reshape(1, d).astype(F32), wr_hi, wr_lo, br,
        _lower_tri(ROW_BLOCK, strict=True).T)

    n_assign = n_tok * TOP_K
    n_blocks = n_assign // MOE_BLOCK + N_EXPERTS
    n_rows = n_blocks * MOE_BLOCK
    n_rows_pad = -(-n_rows // GATHER_ROWS) * GATHER_ROWS
    cnt = counts[:, 0].astype(jnp.int32)
    padded = (cnt + MOE_BLOCK - 1) // MOE_BLOCK * MOE_BLOCK
    padded_end = jnp.cumsum(padded)
    padded_start = padded_end - padded
    experts = jnp.arange(N_EXPERTS, dtype=jnp.int32)[None, :, None]
    start_of = jnp.sum(jnp.where(eid[:, None, :] == experts, padded_start[None, :, None], 0), axis=1)
    dest = start_of + rank
    block_start = jnp.arange(n_rows_pad // MOE_BLOCK, dtype=jnp.int32) * MOE_BLOCK
    block_e = jnp.minimum(jnp.sum((block_start[:, None] >= padded_end[None, :]).astype(jnp.int32),
                                  axis=1), N_EXPERTS - 1)
    of_block = block_e[:, None] == jnp.arange(N_EXPERTS, dtype=jnp.int32)[None, :]
    end_of_block = jnp.sum(jnp.where(of_block, (padded_start + cnt)[None, :], 0), axis=1)
    block_rows = jnp.clip(end_of_block - block_start, 0, MOE_BLOCK)
    n_valid = (padded_end[-1] // MOE_BLOCK).astype(jnp.int32).reshape(1)

    x_sorted = _scatter_rows(xn_packed, dest, n_rows_pad)
    hw = GU_GROUP // 2
    pr = jnp.arange(GU_GROUP)
    src = jnp.where(pr < hw, 2 * pr, 2 * (pr - hw) + 1)
    perm = (jnp.arange(GU_GROUP)[:, None] == src[None, :]).astype(BF16)
    b_gu = b_gate_up[0].astype(F32).reshape(N_EXPERTS, 2 * D_EXPERT // GU_GROUP, hw, 2)
    b_gu = b_gu.transpose(0, 1, 3, 2).reshape(N_EXPERTS, 1, 2 * D_EXPERT)
    y_sorted = _experts(block_e, n_valid, block_rows, x_sorted, w_gate_up[0], b_gu, w_down[0],
                        b_down[0].astype(F32).reshape(N_EXPERTS, 1, d), perm)
    y_tok = _gather_rows(y_sorted, dest.reshape(-1))

    out = _combine(h2, y_tok, wts.T, final_norm_gain.reshape(1, d).astype(F32))
    return out.reshape(nb, seq, d)
```

```python
import functools

import jax
import jax.numpy as jnp
from jax import lax
from jax.experimental import pallas as pl
from jax.experimental.pallas import tpu as pltpu
from jax.experimental.pallas import tpu_sc as plsc

F32 = jnp.float32
BF16 = jnp.bfloat16

D_MODEL = 1024
N_META = 16
S5_GROUP = 16
S5_GROUPS = 32
S5_WIDTH = 512
S5_STATE = 64
HG_HEADS = 4
HG_HEAD_DIM = 128
HG_WIDTH = 512
N_EXPERTS = 32
TOP_K = 4
D_EXPERT = 1024
SWIGLU_ALPHA = 1.702
SWIGLU_LIMIT = 7.0
EPS = 1e-6

LANES = 128
SUBLANES = 8
TIME_BLOCK = 256
META_PAD = TIME_BLOCK
ROW_BLOCK = 512
MOE_BLOCK = 512
GATHER_WINDOW = 64
INDEX_BLOCK = 128
INDEX_SPLIT = INDEX_BLOCK // GATHER_WINDOW
SC_SUBCORES = 32
GATHER_ROWS = INDEX_BLOCK * SC_SUBCORES
HG_DIAG = 32
NEG_BIG = -1e30
VMEM_LIMIT = 56 * 1024 * 1024


def _cparams(sem):
    return pltpu.CompilerParams(dimension_semantics=sem, vmem_limit_bytes=VMEM_LIMIT)


def _sigmoid(x):
    return 0.5 * jnp.tanh(0.5 * x) + 0.5


def _silu(x):
    h = 0.5 * x
    return h + h * jnp.tanh(h)


def _inproj_kernel(x_ref, g_ref, w_ref, u_ref, qfvg_ref, gates_ref):
    x = x_ref[...]
    ms = jnp.mean(x * x, axis=-1, keepdims=True)
    xn = (x * lax.rsqrt(ms + EPS) * g_ref[...]).astype(BF16)
    u_ref[...] = jnp.dot(xn, w_ref[:, 0:S5_WIDTH], preferred_element_type=F32).astype(BF16)
    a, b = S5_WIDTH, S5_WIDTH + 4 * HG_WIDTH
    qfvg_ref[...] = jnp.dot(xn, w_ref[:, a:b], preferred_element_type=F32).astype(BF16)
    gates_ref[...] = jnp.dot(xn, w_ref[:, b:], preferred_element_type=F32).astype(BF16)


def _inproj(rows, gain, w_in_bf16, row_block):
    n = rows.shape[0]
    n_gate = 2 * D_MODEL
    return pl.pallas_call(
        _inproj_kernel,
        grid=(n // row_block,),
        in_specs=[
            pl.BlockSpec((row_block, D_MODEL), lambda i: (i, 0)),
            pl.BlockSpec((1, D_MODEL), lambda i: (0, 0)),
            pl.BlockSpec(w_in_bf16.shape, lambda i: (0, 0)),
        ],
        out_specs=[
            pl.BlockSpec((row_block, S5_WIDTH), lambda i: (i, 0)),
            pl.BlockSpec((row_block, 4 * HG_WIDTH), lambda i: (i, 0)),
            pl.BlockSpec((row_block, n_gate), lambda i: (i, 0)),
        ],
        out_shape=[
            jax.ShapeDtypeStruct((n, S5_WIDTH), BF16),
            jax.ShapeDtypeStruct((n, 4 * HG_WIDTH), BF16),
            jax.ShapeDtypeStruct((n, n_gate), BF16),
        ],
        compiler_params=_cparams(("parallel",)),
        name="inproj",
    )(rows, gain, w_in_bf16)


N_STATE = S5_GROUPS * S5_STATE
HALF_STATE = N_STATE // 2
SCAN_GROUPS_PER_HALF = HALF_STATE // LANES
SEG_COUNT = SUBLANES
SEG_LEN = TIME_BLOCK // SEG_COUNT


def _s5_scan_group(xs, xb, tab, ptab, car_ref, g, chains, tb):
    sub = lax.broadcasted_iota(jnp.int32, (SUBLANES, LANES), 0)
    zero = jnp.zeros((SUBLANES, LANES), F32)
    re, im = slice(0, LANES), slice(LANES, 2 * LANES)

    def rows_of(b, tau):
        return slice(b * tb + tau * SUBLANES, b * tb + (tau + 1) * SUBLANES)

    ends = [(xs[rows_of(b, SEG_LEN - 1), re], xs[rows_of(b, SEG_LEN - 1), im]) for b in chains]
    for tau in range(SEG_LEN - 1):
        p_rows = slice((SEG_LEN - 2 - tau) * SUBLANES, (SEG_LEN - 1 - tau) * SUBLANES)
        pr, pi = ptab[0, p_rows, :], ptab[1, p_rows, :]
        vals = [(xs[rows_of(b, tau), re], xs[rows_of(b, tau), im]) for b in chains]
        ends = [(pr * br - pi * bi + er, pr * bi + pi * br + ei)
                for (er, ei), (br, bi) in zip(ends, vals)]

    m = [tab[i] for i in range(8)]
    starts = []
    for b, (er, ei) in zip(chains, ends):
        gr = jnp.where(sub == 0, car_ref[b, g, :, re], pltpu.roll(er, 1, 0))
        gi = jnp.where(sub == 0, car_ref[b, g, :, im], pltpu.roll(ei, 1, 0))
        for q, shift in enumerate((1, 2, 4)):
            ar, ai = m[2 * q], m[2 * q + 1]
            rr = pltpu.roll(gr, shift, 0)
            ri = pltpu.roll(gi, shift, 0)
            gr, gi = gr + ar * rr - ai * ri, gi + ar * ri + ai * rr
        starts.append((gr, gi))
        xr = m[6] * gr - m[7] * gi + er
        xi = m[6] * gi + m[7] * gr + ei
        car_ref[b, g, :, re] = jnp.broadcast_to(xr[SUBLANES - 1:SUBLANES, :], (SUBLANES, LANES))
        car_ref[b, g, :, im] = jnp.broadcast_to(xi[SUBLANES - 1:SUBLANES, :], (SUBLANES, LANES))

    if xb is not None:
        lr, li = tab[8], tab[9]
        state = starts
        prev = None
        for tau in range(SEG_LEN):
            vals = [(xs[rows_of(b, tau), re], xs[rows_of(b, tau), im]) for b in chains]
            state = [(lr * xr - li * xi + br, lr * xi + li * xr + bi)
                     for (xr, xi), (br, bi) in zip(state, vals)]
            if tau % 2 == 0:
                prev = state
                continue
            for b, (xr0, xi0), (xr1, xi1) in zip(chains, prev, state):
                rows = slice(b * tb + (tau - 1) * SUBLANES, b * tb + (tau + 1) * SUBLANES)
                xb[rows, re] = jnp.concatenate([xr0, xr1], axis=0).astype(BF16)
                xb[rows, im] = jnp.concatenate([xi0, xi1], axis=0).astype(BF16)


def _s5_kernel(u_ref, um_ref, perm_ref, unperm_ref, bm_ref, cm_ref, tab_ref, ptab_ref, d_ref,
               wglu_ref, y_ref, car_ref, up_ref, xs_ref, xb_ref, yacc_ref, *, nb, tb):
    half_ch = S5_WIDTH // 2

    def permute_in(u, b):
        up = jnp.dot(perm_ref[...], u, preferred_element_type=F32).astype(BF16)
        for hf in range(2):
            up_ref[hf, b * tb:(b + 1) * tb, :] = up[:, hf * half_ch:(hf + 1) * half_ch]

    def project_in(g, slot, n_rows):
        xs_ref[slot, 0:n_rows, :] = jnp.dot(up_ref[g // SCAN_GROUPS_PER_HALF, 0:n_rows, :],
                                            bm_ref[g], preferred_element_type=F32)

    def project_out(g0):
        c_pair = cm_ref[pl.ds(g0, 2)].reshape(4 * LANES, half_ch)
        yacc_ref[g0 // SCAN_GROUPS_PER_HALF] += jnp.dot(xb_ref[...], c_pair,
                                                        preferred_element_type=F32)

    def scan(g, slot, chains, finish):
        xb = xb_ref.at[:, slot * 2 * LANES:(slot + 1) * 2 * LANES] if finish else None
        _s5_scan_group(xs_ref.at[slot], xb, tab_ref.at[g], ptab_ref.at[g], car_ref, g, chains, tb)

    @pl.when(pl.program_id(0) == 0)
    def _():
        car_ref[...] = jnp.zeros_like(car_ref)
        permute_in(um_ref[...], 0)

        def meta_trip(g, carry):
            project_in(g, 0, tb)
            scan(g, 0, [0], False)
            return carry

        lax.fori_loop(0, N_STATE // LANES, meta_trip, 0)
        for b in range(1, nb):
            car_ref[b] = car_ref[0]

    for b in range(nb):
        permute_in(u_ref[b], b)
    yacc_ref[...] = jnp.zeros_like(yacc_ref)

    def trip(t, carry):
        g0 = 2 * t
        project_in(g0, 0, nb * tb)
        project_in(g0 + 1, 1, nb * tb)
        scan(g0, 0, list(range(nb)), True)
        scan(g0 + 1, 1, list(range(nb)), True)
        project_out(g0)
        return carry

    lax.fori_loop(0, N_STATE // LANES // 2, trip, 0)

    for b in range(nb):
        rows = slice(b * tb, (b + 1) * tb)
        y = jnp.concatenate([yacc_ref[0, rows, :], yacc_ref[1, rows, :]], axis=1)
        up = jnp.concatenate([up_ref[0, rows, :], up_ref[1, rows, :]], axis=1)
        y = y + d_ref[...] * up.astype(F32)
        y = 0.5 * y * (1.0 + jnp.tanh(0.7978845608028654 * (y + 0.044715 * (y * y * y))))
        z = jnp.dot(y.astype(BF16), wglu_ref[...], preferred_element_type=F32)
        out = (y * _sigmoid(z)).astype(BF16)
        y_ref[b] = jnp.dot(unperm_ref[...], out, preferred_element_type=F32).astype(BF16)


def _s5_mixer(u, u_meta, bm, cm, tab, ptab, d_skip, w_glu_bf16):
    nb, seq, _ = u.shape
    tb = TIME_BLOCK
    r = jnp.arange(tb)
    perm = (((r % SEG_COUNT) * SEG_LEN + r // SEG_COUNT)[:, None] == r[None, :]).astype(BF16)
    kern = functools.partial(_s5_kernel, nb=nb, tb=tb)
    const2 = lambda t: (0, 0)
    const3 = lambda t: (0, 0, 0)
    const4 = lambda t: (0, 0, 0, 0)
    n_groups = N_STATE // LANES
    half_ch = S5_WIDTH // 2
    return pl.pallas_call(
        kern,
        grid=(seq // tb,),
        in_specs=[
            pl.BlockSpec((nb, tb, S5_WIDTH), lambda t: (0, t, 0)),
            pl.BlockSpec(u_meta.shape, const2),
            pl.BlockSpec(perm.shape, const2),
            pl.BlockSpec(perm.shape, const2),
            pl.BlockSpec(bm.shape, const3),
            pl.BlockSpec(cm.shape, const3),
            pl.BlockSpec(tab.shape, const4),
            pl.BlockSpec(ptab.shape, const4),
            pl.BlockSpec(d_skip.shape, const2),
            pl.BlockSpec(w_glu_bf16.shape, const2),
        ],
        out_specs=pl.BlockSpec((nb, tb, S5_WIDTH), lambda t: (0, t, 0)),
        out_shape=jax.ShapeDtypeStruct((nb, seq, S5_WIDTH), BF16),
        scratch_shapes=[pltpu.VMEM((nb, n_groups, SUBLANES, 2 * LANES), F32),
                        pltpu.VMEM((2, nb * tb, half_ch), BF16),
                        pltpu.VMEM((2, nb * tb, 2 * LANES), F32),
                        pltpu.VMEM((nb * tb, 4 * LANES), BF16),
                        pltpu.VMEM((2, nb * tb, half_ch), F32)],
        compiler_params=_cparams(("arbitrary",)),
        name="s5_mixer",
    )(u, u_meta, perm, perm.T, bm, cm, tab, ptab, d_skip, w_glu_bf16)


def _s5_params(lam_re, lam_im, log_dt, b_re, b_im, c_re, c_im):
    lam = lax.complex(lam_re.astype(F32), lam_im.astype(F32))
    dt = jnp.exp(log_dt.astype(F32))[:, None]
    lam_dt = lam * dt
    lam_bar = jnp.exp(lam_dt)
    b_bar = ((lam_bar - 1.0) / lam)[:, :, None] * lax.complex(b_re.astype(F32), b_im.astype(F32))
    gl = S5_GROUPS // 2
    eye = jnp.eye(gl, dtype=F32)

    def in_half(bh):
        def blk(part):
            t = jnp.einsum('gph,gk->ghkp', part, eye)
            return t.reshape(gl * S5_GROUP, gl * S5_STATE)
        return jnp.concatenate([blk(jnp.real(bh)), blk(jnp.imag(bh))], axis=1)

    def out_half(cr, ci):
        def blk(part):
            t = jnp.einsum('ghp,gk->gpkh', part, eye)
            return t.reshape(gl * S5_STATE, gl * S5_GROUP)
        return jnp.concatenate([blk(cr), blk(-ci)], axis=0)

    bm = jnp.stack([in_half(b_bar[:gl]), in_half(b_bar[gl:])]).astype(BF16)
    cm = jnp.stack([out_half(c_re[:gl].astype(F32), c_im[:gl].astype(F32)),
                    out_half(c_re[gl:].astype(F32), c_im[gl:].astype(F32))]).astype(BF16)
    lam_flat = lam_dt.reshape(1, N_STATE)
    rows = jnp.arange(SUBLANES, dtype=F32)[:, None]
    tabs = []
    for shift in (1, 2, 4):
        p = jnp.exp(lam_flat * float(SEG_LEN * shift)) * (rows >= shift).astype(F32)
        tabs += [jnp.real(p), jnp.imag(p)]
    for p in (jnp.exp(lam_flat * float(SEG_LEN)), jnp.exp(lam_flat)):
        tabs += [jnp.real(p), jnp.imag(p)]
    tab = jnp.stack([jnp.broadcast_to(t, (SUBLANES, N_STATE)) for t in tabs]).astype(F32)
    steps = (jnp.arange(SEG_LEN * SUBLANES) // SUBLANES + 1).astype(F32)[:, None]
    p = jnp.exp(lam_flat * steps)
    ptab = jnp.stack([jnp.real(p), jnp.imag(p)]).astype(F32)
    ng = SCAN_GROUPS_PER_HALF
    bm = bm.reshape(2, gl * S5_GROUP, 2, ng, LANES).transpose(0, 3, 1, 2, 4)
    bm = bm.reshape(2 * ng, gl * S5_GROUP, 2 * LANES)
    cm = cm.reshape(2, 2, ng, LANES, gl * S5_GROUP).transpose(0, 2, 1, 3, 4)
    cm = cm.reshape(2 * ng, 2 * LANES, gl * S5_GROUP)
    tab = tab.reshape(tab.shape[0], SUBLANES, 2 * ng, LANES).transpose(2, 0, 1, 3)
    ptab = ptab.reshape(2, SEG_LEN * SUBLANES, 2 * ng, LANES).transpose(2, 0, 1, 3)
    return bm, cm, tab, ptab


def _dot_nt(a, b):
    return lax.dot_general(a, b, (((1,), (1,)), ((), ())), preferred_element_type=F32)


def _dot_tn(a, b):
    return lax.dot_general(a, b, (((0,), (0,)), ((), ())), preferred_element_type=F32)


def _cumsum_rows(tri, x):
    hi = x.astype(BF16)
    r1 = x - hi.astype(F32)
    mid = r1.astype(BF16)
    lo = (r1 - mid.astype(F32)).astype(BF16)
    return (jnp.dot(tri, hi, preferred_element_type=F32)
            + jnp.dot(tri, mid, preferred_element_type=F32)
            + jnp.dot(tri, lo, preferred_element_type=F32))


def _segment_rows(x, seg, pos):
    n = x.shape[0]
    parts = [jnp.broadcast_to(x[s * seg + pos:s * seg + pos + 1, :], (seg, x.shape[1]))
             for s in range(n // seg)]
    return parts[0] if len(parts) == 1 else jnp.concatenate(parts, axis=0)


def _hg_gates(x, lb, row_valid=None):
    w = HG_WIDTH
    q = x[:, 0:w]
    f = lb + (1.0 - lb) * _sigmoid(x[:, w:2 * w])
    logf = jnp.log(f)
    k = 1.0 - f
    if row_valid is not None:
        logf = jnp.where(row_valid, logf, 0.0)
        k = jnp.where(row_valid, k, 0.0)
    return _silu(q), k, logf


def _hg_kernel(x_ref, xm_ref, lb_ref, gain_ref, tri_ref, y_ref, st_ref, *, nb, tb):
    w, dh = HG_WIDTH, HG_HEAD_DIM
    lb = lb_ref[...]
    tri = tri_ref[...]

    @pl.when(pl.program_id(0) == 0)
    def _():
        xm = xm_ref[...].astype(F32)
        valid = lax.broadcasted_iota(jnp.int32, (META_PAD, 1), 0) >= (META_PAD - N_META)
        _, k, logf = _hg_gates(xm, lb, valid)
        bc = _cumsum_rows(tri[0:META_PAD, 0:META_PAD], logf)
        ki = (k * jnp.exp(bc[META_PAD - 1:META_PAD, :] - bc)).astype(BF16)
        v = xm[:, 2 * w:3 * w].astype(BF16)
        for h in range(HG_HEADS):
            sl = slice(h * dh, (h + 1) * dh)
            s0 = _dot_tn(v[:, sl], ki[:, sl])
            for b in range(nb):
                st_ref[b, h] = s0

    ri = lax.broadcasted_iota(jnp.int32, (tb, tb), 0)
    ci = lax.broadcasted_iota(jnp.int32, (tb, tb), 1)
    diag_shift = HG_DIAG.bit_length() - 1
    diag_mask = ((ri >> diag_shift) == (ci >> diag_shift)) & (ci <= ri)
    levels = []
    seg = 2 * HG_DIAG
    while seg <= tb:
        levels.append(seg)
        seg *= 2
    seg_masks = [None if s == tb else
                 ((ri >> (s.bit_length() - 1)) == (ci >> (s.bit_length() - 1))).astype(F32)
                 for s in levels]
    row = lax.broadcasted_iota(jnp.int32, (tb, 1), 0)

    for b in range(nb):
        x = x_ref[b].astype(F32)
        qs, k, logf = _hg_gates(x, lb)
        v = x[:, 2 * w:3 * w].astype(BF16)
        g = x[:, 3 * w:4 * w]
        bc = _cumsum_rows(tri, logf)
        dlt = bc - _segment_rows(bc, HG_DIAG, HG_DIAG // 2 - 1)
        qk = [((qs * jnp.exp(dlt)).astype(BF16), (k * jnp.exp(-dlt)).astype(BF16))]
        for s in levels:
            e = jnp.exp(-jnp.abs(bc - _segment_rows(bc, s, s // 2 - 1)))
            upper = (row & (s - 1)) >= (s // 2)
            qk.append((jnp.where(upper, qs * e, 0.0).astype(BF16),
                       jnp.where(upper, 0.0, k * e).astype(BF16)))
        b_last = bc[tb - 1:tb, :]
        qi = (qs * jnp.exp(bc)).astype(BF16)
        ki = (k * jnp.exp(b_last - bc)).astype(BF16)
        dec = jnp.exp(b_last)
        outs = []
        for h in range(HG_HEADS):
            sl = slice(h * dh, (h + 1) * dh)
            sc = jnp.where(diag_mask, _dot_nt(qk[0][0][:, sl], qk[0][1][:, sl]), 0.0)
            for (ql, kl), m in zip(qk[1:], seg_masks):
                t = _dot_nt(ql[:, sl], kl[:, sl])
                sc = sc + (t if m is None else t * m)
            st = st_ref[b, h]
            o = (jnp.dot(sc.astype(BF16), v[:, sl], preferred_element_type=F32)
                 + _dot_nt(qi[:, sl], st.astype(BF16)))
            st_ref[b, h] = dec[:, sl] * st + _dot_tn(v[:, sl], ki[:, sl])
            ms = jnp.mean(o * o, axis=-1, keepdims=True)
            outs.append(o * lax.rsqrt(ms + EPS))
        o = jnp.concatenate(outs, axis=1) * gain_ref[...]
        y_ref[b] = (o * _silu(g)).astype(BF16)


def _hg_mixer(x, x_meta, lb, gain, tri):
    nb, seq, _ = x.shape
    tb = TIME_BLOCK
    kern = functools.partial(_hg_kernel, nb=nb, tb=tb)
    const2 = lambda t: (0, 0)
    return pl.pallas_call(
        kern,
        grid=(seq // tb,),
        in_specs=[
            pl.BlockSpec((nb, tb, 4 * HG_WIDTH), lambda t: (0, t, 0)),
            pl.BlockSpec(x_meta.shape, const2),
            pl.BlockSpec(lb.shape, const2),
            pl.BlockSpec(gain.shape, const2),
            pl.BlockSpec(tri.shape, const2),
        ],
        out_specs=pl.BlockSpec((nb, tb, HG_WIDTH), lambda t: (0, t, 0)),
        out_shape=jax.ShapeDtypeStruct((nb, seq, HG_WIDTH), BF16),
        scratch_shapes=[pltpu.VMEM((nb, HG_HEADS, HG_HEAD_DIM, HG_HEAD_DIM), F32)],
        compiler_params=_cparams(("arbitrary",)),
        name="hgrn2_mixer",
    )(x, x_meta, lb, gain, tri)


def _split3(x):
    hi = x.astype(BF16)
    lo = (x - hi.astype(F32)).astype(BF16)
    return hi, lo


def _pack_pairs(y):
    n = y.shape[1] // 2
    lo = pltpu.bitcast(y[:, :n].astype(BF16).astype(F32), jnp.uint32)
    hi = pltpu.bitcast(y[:, n:].astype(BF16).astype(F32), jnp.uint32)
    return (lo >> 16) | (hi & jnp.uint32(0xFFFF0000))


def _unpack_pairs(p):
    lo = pltpu.bitcast(p << 16, F32)
    hi = pltpu.bitcast(p & jnp.uint32(0xFFFF0000), F32)
    return lo, hi


def _merge_kernel(x_ref, ys_ref, yh_ref, gt_ref, wbs_ref, wbh_ref, wo_ref, g2_ref, wr_hi_ref,
                  wr_lo_ref, br_ref, tri_ref, h2_ref, xp_ref, eid_ref, wt_ref, rank_ref, cnt_ref,
                  run_ref):
    @pl.when(pl.program_id(0) == 0)
    def _():
        run_ref[...] = jnp.zeros_like(run_ref)

    gt = gt_ref[...].astype(F32)
    gs = _sigmoid(gt[:, :D_MODEL])
    gh = _sigmoid(gt[:, D_MODEL:])
    merged = (gs * jnp.dot(ys_ref[...], wbs_ref[...], preferred_element_type=F32)
              + gh * jnp.dot(yh_ref[...], wbh_ref[...], preferred_element_type=F32))
    h2 = x_ref[...] + jnp.dot(merged.astype(BF16), wo_ref[...], preferred_element_type=F32)
    h2_ref[...] = h2
    ms = jnp.mean(h2 * h2, axis=-1, keepdims=True)
    xn = h2 * lax.rsqrt(ms + EPS) * g2_ref[...]
    xp_ref[...] = _pack_pairs(xn)

    x_hi, x_lo = _split3(xn)
    logits = (_dot_nt(wr_hi_ref[...], x_hi) + _dot_nt(wr_lo_ref[...], x_hi)
              + _dot_nt(wr_hi_ref[...], x_lo)) + br_ref[:, 0:1]
    rows = logits.shape[1]
    expert = lax.broadcasted_iota(jnp.int32, (N_EXPERTS, rows), 0)
    tops, hots, ids = [], [], []
    sel = jnp.zeros((N_EXPERTS, rows), F32)
    for k in range(TOP_K):
        m = jnp.max(logits, axis=0, keepdims=True)
        idx = jnp.min(jnp.where(logits == m, expert, N_EXPERTS), axis=0, keepdims=True)
        hot = expert == idx
        logits = jnp.where(hot, NEG_BIG, logits)
        tops.append(m)
        hots.append(hot)
        ids.append(idx)
        sel = sel + hot.astype(F32)
    es = [jnp.exp(m - tops[0]) for m in tops]
    tot = es[0] + es[1] + es[2] + es[3]
    run = run_ref[...]
    prefix = jnp.dot(sel.astype(BF16), tri_ref[...], preferred_element_type=F32) + run[:, 0:1]
    ranks = [jnp.sum(jnp.where(hot, prefix, 0.0), axis=0, keepdims=True) for hot in hots]
    eid_ref[...] = jnp.concatenate(ids, axis=0)
    wt_ref[...] = jnp.concatenate([e / tot for e in es], axis=0)
    rank_ref[...] = jnp.concatenate(ranks, axis=0).astype(jnp.int32)
    run = run + jnp.sum(sel, axis=1, keepdims=True)
    run_ref[...] = run
    cnt_ref[...] = run


def _merge_router(x_rows, y_s5, y_hg, gates, wbs, wbh, wo, g2, wr_hi, wr_lo, br, tri_strict):
    n = x_rows.shape[0]
    rb = ROW_BLOCK
    const2 = lambda i: (0, 0)
    rowblk = lambda width: pl.BlockSpec((rb, width), lambda i: (i, 0))
    choice = pl.BlockSpec((TOP_K, rb), lambda i: (0, i))
    return pl.pallas_call(
        _merge_kernel,
        grid=(n // rb,),
        in_specs=[
            rowblk(D_MODEL), rowblk(S5_WIDTH), rowblk(HG_WIDTH), rowblk(2 * D_MODEL),
            pl.BlockSpec(wbs.shape, const2), pl.BlockSpec(wbh.shape, const2),
            pl.BlockSpec(wo.shape, const2), pl.BlockSpec(g2.shape, const2),
            pl.BlockSpec(wr_hi.shape, const2), pl.BlockSpec(wr_lo.shape, const2),
            pl.BlockSpec(br.shape, const2), pl.BlockSpec(tri_strict.shape, const2),
        ],
        out_specs=[
            rowblk(D_MODEL), rowblk(D_MODEL // 2), choice, choice, choice,
            pl.BlockSpec((N_EXPERTS, LANES), const2),
        ],
        out_shape=[
            jax.ShapeDtypeStruct((n, D_MODEL), F32),
            jax.ShapeDtypeStruct((n, D_MODEL // 2), jnp.uint32),
            jax.ShapeDtypeStruct((TOP_K, n), jnp.int32),
            jax.ShapeDtypeStruct((TOP_K, n), F32),
            jax.ShapeDtypeStruct((TOP_K, n), jnp.int32),
            jax.ShapeDtypeStruct((N_EXPERTS, LANES), F32),
        ],
        scratch_shapes=[pltpu.VMEM((N_EXPERTS, LANES), F32)],
        compiler_params=_cparams(("arbitrary",)),
        name="merge_router",
    )(x_rows, y_s5, y_hg, gates, wbs, wbh, wo, g2, wr_hi, wr_lo, br, tri_strict)


def _gather_rows(src, idx):
    n = idx.shape[0]
    width = src.shape[1]
    assert n % GATHER_ROWS == 0
    mesh = plsc.VectorSubcoreMesh(core_axis_name="core", subcore_axis_name="subcore")

    @functools.partial(pl.kernel, out_type=jax.ShapeDtypeStruct((n, width), src.dtype),
                       mesh=mesh, scratch_types=[], name="sc_gather_rows")
    def gather(src_hbm, idx_hbm, out_hbm):
        def body(idx_vmem, out_vmem):
            off = pl.multiple_of(pl.program_id(1) * GATHER_WINDOW, GATHER_WINDOW)
            pltpu.sync_copy(src_hbm.at[idx_vmem.at[0, pl.ds(off, GATHER_WINDOW)]], out_vmem)

        pltpu.emit_pipeline(
            body,
            grid=(n // INDEX_BLOCK, INDEX_SPLIT),
            in_specs=[pl.BlockSpec((1, INDEX_BLOCK), lambda i, j: (0, i))],
            out_specs=[pl.BlockSpec((GATHER_WINDOW, width), lambda i, j: (INDEX_SPLIT * i + j, 0))],
            core_axis_name=("core", "subcore"),
            dimension_semantics=(pltpu.PARALLEL, pltpu.ARBITRARY),
        )(idx_hbm, out_hbm)

    return gather(src, idx.reshape(1, n))


def _scatter_rows(src, dest, n_out):
    n_src, width = src.shape
    assert dest.shape == (TOP_K, n_src) and n_src % GATHER_ROWS == 0
    mesh = plsc.VectorSubcoreMesh(core_axis_name="core", subcore_axis_name="subcore")

    @functools.partial(pl.kernel, out_type=jax.ShapeDtypeStruct((n_out, width), src.dtype),
                       mesh=mesh, scratch_types=[], name="sc_scatter_rows")
    def scatter(src_hbm, idx_hbm, out_hbm):
        def body(src_vmem, idx_vmem):
            off = pl.multiple_of(pl.program_id(1) * GATHER_WINDOW, GATHER_WINDOW)
            for k in range(TOP_K):
                pltpu.sync_copy(src_vmem, out_hbm.at[idx_vmem.at[k, pl.ds(off, GATHER_WINDOW)]])

        pltpu.emit_pipeline(
            body,
            grid=(n_src // INDEX_BLOCK, INDEX_SPLIT),
            in_specs=[pl.BlockSpec((GATHER_WINDOW, width), lambda i, j: (INDEX_SPLIT * i + j, 0)),
                      pl.BlockSpec((TOP_K, INDEX_BLOCK), lambda i, j: (0, i))],
            out_specs=[],
            core_axis_name=("core", "subcore"),
            dimension_semantics=(pltpu.PARALLEL, pltpu.ARBITRARY),
        )(src_hbm, idx_hbm)

    return scatter(src, dest)


GU_GROUP = 256
DOWN_GROUPS = 4


def _expert_kernel(be_ref, nv_ref, nr_ref, x_ref, wgu_ref, bgu_ref, wd_ref, bd_ref, perm_ref,
                   y_ref, wgu_s, wd_s):
    i = pl.program_id(0)
    prev = be_ref[jnp.maximum(i - 1, 0)]
    fresh = (i == 0) | (be_ref[i] != prev)

    @pl.when(fresh & (i < nv_ref[0]))
    def _():
        for c in range(2 * D_EXPERT // GU_GROUP):
            cols = slice(c * GU_GROUP, (c + 1) * GU_GROUP)
            w = wgu_ref[0, :, cols].astype(BF16)
            wgu_s[:, cols] = jnp.dot(w, perm_ref[...], preferred_element_type=F32).astype(BF16)
        wd_s[...] = wd_ref[0].astype(BF16)

    @pl.when(i < nv_ref[0])
    def _():
        half = D_MODEL // 2
        live = lax.broadcasted_iota(jnp.int32, (MOE_BLOCK, 1), 0) < nr_ref[i]
        xa, xb = _unpack_pairs(jnp.where(live, x_ref[...], jnp.uint32(0)))
        x = jnp.concatenate([xa.astype(BF16), xb.astype(BF16)], axis=1)
        hw = GU_GROUP // 2
        n_groups = 2 * D_EXPERT // GU_GROUP

        def gate_up(c):
            return jnp.dot(x, wgu_s[:, c * GU_GROUP:(c + 1) * GU_GROUP],
                           preferred_element_type=F32)

        acc = None
        hcols = []
        gu_next = gate_up(0)
        for c in range(n_groups):
            gu = gu_next + bgu_ref[0, :, c * GU_GROUP:(c + 1) * GU_GROUP]
            if c + 1 < n_groups:
                gu_next = gate_up(c + 1)
            gate = jnp.minimum(gu[:, :hw], SWIGLU_LIMIT)
            up = jnp.clip(gu[:, hw:], -SWIGLU_LIMIT, SWIGLU_LIMIT)
            hcols.append(((up + 1.0) * (gate * _sigmoid(gate * SWIGLU_ALPHA))).astype(BF16))
            if (c + 1) % DOWN_GROUPS == 0:
                lo = c + 1 - DOWN_GROUPS
                part = jnp.dot(jnp.concatenate(hcols[lo:c + 1], axis=1),
                               wd_s[lo * hw:(c + 1) * hw, :], preferred_element_type=F32)
                acc = part if acc is None else acc + part
        y_ref[...] = _pack_pairs(acc + bd_ref[0])

    @pl.when(i >= nv_ref[0])
    def _():
        y_ref[...] = jnp.zeros_like(y_ref)


def _experts(block_e, n_valid, block_rows, x_rows, w_gate_up, b_gu_grouped, w_down, b_down, perm):
    n_rows = x_rows.shape[0]
    n_blocks = n_rows // MOE_BLOCK
    half = D_MODEL // 2
    by_expert = lambda i, be, nv, nr: (be[i], 0, 0)
    grid_spec = pltpu.PrefetchScalarGridSpec(
        num_scalar_prefetch=3,
        grid=(n_blocks,),
        in_specs=[
            pl.BlockSpec((MOE_BLOCK, half), lambda i, be, nv, nr: (i, 0)),
            pl.BlockSpec((1, D_MODEL, 2 * D_EXPERT), by_expert),
            pl.BlockSpec((1, 1, 2 * D_EXPERT), by_expert),
            pl.BlockSpec((1, D_EXPERT, D_MODEL), by_expert),
            pl.BlockSpec((1, 1, D_MODEL), by_expert),
            pl.BlockSpec(perm.shape, lambda i, be, nv, nr: (0, 0)),
        ],
        out_specs=pl.BlockSpec((MOE_BLOCK, half), lambda i, be, nv, nr: (i, 0)),
        scratch_shapes=[
            pltpu.VMEM((D_MODEL, 2 * D_EXPERT), BF16),
            pltpu.VMEM((D_EXPERT, D_MODEL), BF16),
        ],
    )
    return pl.pallas_call(
        _expert_kernel,
        grid_spec=grid_spec,
        out_shape=jax.ShapeDtypeStruct((n_rows, half), jnp.uint32),
        compiler_params=_cparams(("arbitrary",)),
        name="experts",
    )(block_e, n_valid, block_rows, x_rows, w_gate_up, b_gu_grouped, w_down, b_down, perm)


def _combine_kernel(h2_ref, y0_ref, y1_ref, y2_ref, y3_ref, wt_ref, gf_ref, out_ref):
    half = D_MODEL // 2
    wt = wt_ref[...]
    lo = jnp.zeros((h2_ref.shape[0], half), F32)
    hi = jnp.zeros((h2_ref.shape[0], half), F32)
    for k, yk_ref in enumerate((y0_ref, y1_ref, y2_ref, y3_ref)):
        a, b = _unpack_pairs(yk_ref[...])
        lo = lo + wt[:, k:k + 1] * a
        hi = hi + wt[:, k:k + 1] * b
    y = h2_ref[...] + jnp.concatenate([lo, hi], axis=1)
    ms = jnp.mean(y * y, axis=-1, keepdims=True)
    out_ref[...] = y * lax.rsqrt(ms + EPS) * gf_ref[...]


def _combine(h2, y_tok, wts, gain):
    n = h2.shape[0]
    rb = ROW_BLOCK
    rowblk = lambda width: pl.BlockSpec((rb, width), lambda i: (i, 0))
    steps = n // rb
    choice = lambda k: pl.BlockSpec((rb, D_MODEL // 2), lambda i: (k * steps + i, 0))
    return pl.pallas_call(
        _combine_kernel,
        grid=(steps,),
        in_specs=[rowblk(D_MODEL)] + [choice(k) for k in range(TOP_K)]
        + [rowblk(TOP_K), pl.BlockSpec((1, D_MODEL), lambda i: (0, 0))],
        out_specs=rowblk(D_MODEL),
        out_shape=jax.ShapeDtypeStruct((n, D_MODEL), F32),
        compiler_params=_cparams(("parallel",)),
        name="combine_norm",
    )(h2, y_tok, y_tok, y_tok, y_tok, wts, gain)


def _lower_tri(n, strict):
    r = lax.broadcasted_iota(jnp.int32, (n, n), 0)
    c = lax.broadcasted_iota(jnp.int32, (n, n), 1)
    return ((c < r) if strict else (c <= r)).astype(BF16)


def kernel(x, meta_tokens, norm1_gain, w_in, s5_lambda_re, s5_lambda_im, s5_log_dt, s5_b_re,
           s5_b_im, s5_c_re, s5_c_im, s5_d, s5_w_glu, hgrn_lb_logits, hgrn_norm_gain,
           w_branch_s5, w_branch_hgrn, w_out, norm2_gain, w_router, b_router, w_gate_up,
           b_gate_up, w_down, b_down, final_norm_gain):
    nb, seq, d = x.shape
    n_tok = nb * seq
    assert d == D_MODEL and seq % TIME_BLOCK == 0 and n_tok % ROW_BLOCK == 0
    x_rows = x.reshape(n_tok, d)
    w_in_b = w_in[0].astype(BF16)
    g1 = norm1_gain[0].reshape(1, d).astype(F32)

    u, qfvg, gates = _inproj(x_rows, g1, w_in_b, 2 * ROW_BLOCK)
    meta_rows = jnp.concatenate(
        [jnp.zeros((META_PAD - N_META, d), F32), meta_tokens.astype(F32)], axis=0)
    u_m, qfvg_m, _ = _inproj(meta_rows, g1, w_in_b, META_PAD)

    bm, cm, tab, ptab = _s5_params(s5_lambda_re[0], s5_lambda_im[0], s5_log_dt[0], s5_b_re[0],
                                   s5_b_im[0], s5_c_re[0], s5_c_im[0])
    y_s5 = _s5_mixer(u.reshape(nb, seq, S5_WIDTH), u_m, bm, cm, tab, ptab,
                     s5_d[0].reshape(1, S5_WIDTH).astype(F32), s5_w_glu[0].astype(BF16))

    lower_bounds = jnp.cumsum(jax.nn.softmax(hgrn_lb_logits.astype(F32), axis=0), axis=0)
    lb = lower_bounds[0].reshape(1, HG_WIDTH)
    y_hg = _hg_mixer(qfvg.reshape(nb, seq, 4 * HG_WIDTH), qfvg_m, lb,
                     hgrn_norm_gain[0].reshape(1, HG_WIDTH).astype(F32),
                     _lower_tri(TIME_BLOCK, strict=False))

    wr = w_router[0].astype(F32).T
    wr_hi = wr.astype(BF16)
    wr_lo = (wr - wr_hi.astype(F32)).astype(BF16)
    br = jnp.broadcast_to(b_router[0].astype(F32)[:, None], (N_EXPERTS, LANES))
    h2, xn_packed, eid, wts, rank, counts = _merge_router(
        x_rows, y_s5.reshape(n_tok, S5_WIDTH), y_hg.reshape(n_tok, HG_WIDTH), gates,
        w_branch_s5[0].astype(BF16), w_branch_hgrn[0].astype(BF16), w_out[0].astype(BF16),
        norm2_gain[0].reshape(1, d).astype(F32), wr_hi, wr_lo, br,
        _lower_tri(ROW_BLOCK, strict=True).T)

    n_assign = n_tok * TOP_K
    n_blocks = n_assign // MOE_BLOCK + N_EXPERTS
    n_rows = n_blocks * MOE_BLOCK
    n_rows_pad = -(-n_rows // GATHER_ROWS) * GATHER_ROWS
    cnt = counts[:, 0].astype(jnp.int32)
    padded = (cnt + MOE_BLOCK - 1) // MOE_BLOCK * MOE_BLOCK
    padded_end = jnp.cumsum(padded)
    padded_start = padded_end - padded
    experts = jnp.arange(N_EXPERTS, dtype=jnp.int32)[None, :, None]
    start_of = jnp.sum(jnp.where(eid[:, None, :] == experts, padded_start[None, :, None], 0), axis=1)
    dest = start_of + rank
    block_start = jnp.arange(n_rows_pad // MOE_BLOCK, dtype=jnp.int32) * MOE_BLOCK
    block_e = jnp.minimum(jnp.sum((block_start[:, None] >= padded_end[None, :]).astype(jnp.int32),
                                  axis=1), N_EXPERTS - 1)
    of_block = block_e[:, None] == jnp.arange(N_EXPERTS, dtype=jnp.int32)[None, :]
    end_of_block = jnp.sum(jnp.where(of_block, (padded_start + cnt)[None, :], 0), axis=1)
    block_rows = jnp.clip(end_of_block - block_start, 0, MOE_BLOCK)
    n_valid = (padded_end[-1] // MOE_BLOCK).astype(jnp.int32).reshape(1)

    x_sorted = _scatter_rows(xn_packed, dest, n_rows_pad)
    hw = GU_GROUP // 2
    pr = jnp.arange(GU_GROUP)
    src = jnp.where(pr < hw, 2 * pr, 2 * (pr - hw) + 1)
    perm = (jnp.arange(GU_GROUP)[:, None] == src[None, :]).astype(BF16)
    b_gu = b_gate_up[0].astype(F32).reshape(N_EXPERTS, 2 * D_EXPERT // GU_GROUP, hw, 2)
    b_gu = b_gu.transpose(0, 1, 3, 2).reshape(N_EXPERTS, 1, 2 * D_EXPERT)
    y_sorted = _experts(block_e, n_valid, block_rows, x_sorted, w_gate_up[0], b_gu, w_down[0],
                        b_down[0].astype(F32).reshape(N_EXPERTS, 1, d), perm)
    y_tok = _gather_rows(y_sorted, dest.reshape(-1))

    out = _combine(h2, y_tok, wts.T, final_norm_gain.reshape(1, d).astype(F32))
    return out.reshape(nb, seq, d)
```

```python
import functools

import jax
import jax.numpy as jnp
from jax import lax
from jax.experimental import pallas as pl
from jax.experimental.pallas import tpu as pltpu
from jax.experimental.pallas import tpu_sc as plsc

F32 = jnp.float32
BF16 = jnp.bfloat16

D_MODEL = 1024
N_META = 16
S5_GROUP = 16
S5_GROUPS = 32
S5_WIDTH = 512
S5_STATE = 64
HG_HEADS = 4
HG_HEAD_DIM = 128
HG_WIDTH = 512
N_EXPERTS = 32
TOP_K = 4
D_EXPERT = 1024
SWIGLU_ALPHA = 1.702
SWIGLU_LIMIT = 7.0
EPS = 1e-6

LANES = 128
SUBLANES = 8
TIME_BLOCK = 256
META_PAD = TIME_BLOCK
ROW_BLOCK = 1024
MOE_BLOCK = 512
GATHER_WINDOW = 64
INDEX_BLOCK = 128
INDEX_SPLIT = INDEX_BLOCK // GATHER_WINDOW
SC_SUBCORES = 32
GATHER_ROWS = INDEX_BLOCK * SC_SUBCORES
HG_DIAG = 32
NEG_BIG = -1e30
VMEM_LIMIT = 56 * 1024 * 1024


def _cparams(sem):
    return pltpu.CompilerParams(dimension_semantics=sem, vmem_limit_bytes=VMEM_LIMIT)


def _sigmoid(x):
    return 0.5 * jnp.tanh(0.5 * x) + 0.5


def _silu(x):
    h = 0.5 * x
    return h + h * jnp.tanh(h)


def _inproj_kernel(x_ref, g_ref, w_ref, u_ref, qfvg_ref, gates_ref):
    x = x_ref[...]
    ms = jnp.mean(x * x, axis=-1, keepdims=True)
    xn = (x * lax.rsqrt(ms + EPS) * g_ref[...]).astype(BF16)
    u_ref[...] = jnp.dot(xn, w_ref[:, 0:S5_WIDTH], preferred_element_type=F32).astype(BF16)
    a, b = S5_WIDTH, S5_WIDTH + 4 * HG_WIDTH
    qfvg_ref[...] = jnp.dot(xn, w_ref[:, a:b], preferred_element_type=F32).astype(BF16)
    gates_ref[...] = jnp.dot(xn, w_ref[:, b:], preferred_element_type=F32).astype(BF16)


def _inproj(rows, gain, w_in_bf16, row_block):
    n = rows.shape[0]
    n_gate = 2 * D_MODEL
    return pl.pallas_call(
        _inproj_kernel,
        grid=(n // row_block,),
        in_specs=[
            pl.BlockSpec((row_block, D_MODEL), lambda i: (i, 0)),
            pl.BlockSpec((1, D_MODEL), lambda i: (0, 0)),
            pl.BlockSpec(w_in_bf16.shape, lambda i: (0, 0)),
        ],
        out_specs=[
            pl.BlockSpec((row_block, S5_WIDTH), lambda i: (i, 0)),
            pl.BlockSpec((row_block, 4 * HG_WIDTH), lambda i: (i, 0)),
            pl.BlockSpec((row_block, n_gate), lambda i: (i, 0)),
        ],
        out_shape=[
            jax.ShapeDtypeStruct((n, S5_WIDTH), BF16),
            jax.ShapeDtypeStruct((n, 4 * HG_WIDTH), BF16),
            jax.ShapeDtypeStruct((n, n_gate), BF16),
        ],
        compiler_params=_cparams(("parallel",)),
        name="inproj",
    )(rows, gain, w_in_bf16)


N_STATE = S5_GROUPS * S5_STATE
HALF_STATE = N_STATE // 2
SCAN_GROUPS_PER_HALF = HALF_STATE // LANES
SEG_COUNT = SUBLANES
SEG_LEN = TIME_BLOCK // SEG_COUNT


def _s5_scan_group(xs, xb, tab, ptab, car_ref, g, chains, tb):
    sub = lax.broadcasted_iota(jnp.int32, (SUBLANES, LANES), 0)
    zero = jnp.zeros((SUBLANES, LANES), F32)
    re, im = slice(0, LANES), slice(LANES, 2 * LANES)

    def rows_of(b, tau):
        return slice(b * tb + tau * SUBLANES, b * tb + (tau + 1) * SUBLANES)

    ends = [(xs[rows_of(b, SEG_LEN - 1), re], xs[rows_of(b, SEG_LEN - 1), im]) for b in chains]
    for tau in range(SEG_LEN - 1):
        p_rows = slice((SEG_LEN - 2 - tau) * SUBLANES, (SEG_LEN - 1 - tau) * SUBLANES)
        pr, pi = ptab[0, p_rows, :], ptab[1, p_rows, :]
        vals = [(xs[rows_of(b, tau), re], xs[rows_of(b, tau), im]) for b in chains]
        ends = [(pr * br - pi * bi + er, pr * bi + pi * br + ei)
                for (er, ei), (br, bi) in zip(ends, vals)]

    m = [tab[i] for i in range(8)]
    starts = []
    for b, (er, ei) in zip(chains, ends):
        gr = jnp.where(sub == 0, car_ref[b, g, :, re], pltpu.roll(er, 1, 0))
        gi = jnp.where(sub == 0, car_ref[b, g, :, im], pltpu.roll(ei, 1, 0))
        for q, shift in enumerate((1, 2, 4)):
            ar, ai = m[2 * q], m[2 * q + 1]
            rr = pltpu.roll(gr, shift, 0)
            ri = pltpu.roll(gi, shift, 0)
            gr, gi = gr + ar * rr - ai * ri, gi + ar * ri + ai * rr
        starts.append((gr, gi))
        xr = m[6] * gr - m[7] * gi + er
        xi = m[6] * gi + m[7] * gr + ei
        car_ref[b, g, :, re] = jnp.broadcast_to(xr[SUBLANES - 1:SUBLANES, :], (SUBLANES, LANES))
        car_ref[b, g, :, im] = jnp.broadcast_to(xi[SUBLANES - 1:SUBLANES, :], (SUBLANES, LANES))

    if xb is not None:
        lr, li = tab[8], tab[9]
        state = starts
        prev = None
        for tau in range(SEG_LEN):
            vals = [(xs[rows_of(b, tau), re], xs[rows_of(b, tau), im]) for b in chains]
            state = [(lr * xr - li * xi + br, lr * xi + li * xr + bi)
                     for (xr, xi), (br, bi) in zip(state, vals)]
            if tau % 2 == 0:
                prev = state
                continue
            for b, (xr0, xi0), (xr1, xi1) in zip(chains, prev, state):
                rows = slice(b * tb + (tau - 1) * SUBLANES, b * tb + (tau + 1) * SUBLANES)
                xb[rows, re] = jnp.concatenate([xr0, xr1], axis=0).astype(BF16)
                xb[rows, im] = jnp.concatenate([xi0, xi1], axis=0).astype(BF16)


def _s5_kernel(u_ref, um_ref, perm_ref, unperm_ref, bm_ref, cm_ref, tab_ref, ptab_ref, d_ref,
               wglu_ref, y_ref, car_ref, up_ref, xs_ref, xb_ref, yacc_ref, *, nb, tb):
    half_ch = S5_WIDTH // 2

    def permute_in(u, b):
        up = jnp.dot(perm_ref[...], u, preferred_element_type=F32).astype(BF16)
        for hf in range(2):
            up_ref[hf, b * tb:(b + 1) * tb, :] = up[:, hf * half_ch:(hf + 1) * half_ch]

    def project_in(g, slot, n_rows):
        xs_ref[slot, 0:n_rows, :] = jnp.dot(up_ref[g // SCAN_GROUPS_PER_HALF, 0:n_rows, :],
                                            bm_ref[g], preferred_element_type=F32)

    def project_out(g0):
        c_pair = cm_ref[pl.ds(g0, 2)].reshape(4 * LANES, half_ch)
        yacc_ref[g0 // SCAN_GROUPS_PER_HALF] += jnp.dot(xb_ref[...], c_pair,
                                                        preferred_element_type=F32)

    def scan(g, slot, chains, finish):
        xb = xb_ref.at[:, slot * 2 * LANES:(slot + 1) * 2 * LANES] if finish else None
        _s5_scan_group(xs_ref.at[slot], xb, tab_ref.at[g], ptab_ref.at[g], car_ref, g, chains, tb)

    @pl.when(pl.program_id(0) == 0)
    def _():
        car_ref[...] = jnp.zeros_like(car_ref)
        permute_in(um_ref[...], 0)

        def meta_trip(g, carry):
            project_in(g, 0, tb)
            scan(g, 0, [0], False)
            return carry

        lax.fori_loop(0, N_STATE // LANES, meta_trip, 0)
        for b in range(1, nb):
            car_ref[b] = car_ref[0]

    for b in range(nb):
        permute_in(u_ref[b], b)
    yacc_ref[...] = jnp.zeros_like(yacc_ref)

    def trip(t, carry):
        g0 = 2 * t
        project_in(g0, 0, nb * tb)
        project_in(g0 + 1, 1, nb * tb)
        scan(g0, 0, list(range(nb)), True)
        scan(g0 + 1, 1, list(range(nb)), True)
        project_out(g0)
        return carry

    lax.fori_loop(0, N_STATE // LANES // 2, trip, 0)

    for b in range(nb):
        rows = slice(b * tb, (b + 1) * tb)
        y = jnp.concatenate([yacc_ref[0, rows, :], yacc_ref[1, rows, :]], axis=1)
        up = jnp.concatenate([up_ref[0, rows, :], up_ref[1, rows, :]], axis=1)
        y = y + d_ref[...] * up.astype(F32)
        y = 0.5 * y * (1.0 + jnp.tanh(0.7978845608028654 * (y + 0.044715 * (y * y * y))))
        z = jnp.dot(y.astype(BF16), wglu_ref[...], preferred_element_type=F32)
        out = (y * _sigmoid(z)).astype(BF16)
        y_ref[b] = jnp.dot(unperm_ref[...], out, preferred_element_type=F32).astype(BF16)


def _s5_mixer(u, u_meta, bm, cm, tab, ptab, d_skip, w_glu_bf16):
    nb, seq, _ = u.shape
    tb = TIME_BLOCK
    r = jnp.arange(tb)
    perm = (((r % SEG_COUNT) * SEG_LEN + r // SEG_COUNT)[:, None] == r[None, :]).astype(BF16)
    kern = functools.partial(_s5_kernel, nb=nb, tb=tb)
    const2 = lambda t: (0, 0)
    const3 = lambda t: (0, 0, 0)
    const4 = lambda t: (0, 0, 0, 0)
    n_groups = N_STATE // LANES
    half_ch = S5_WIDTH // 2
    return pl.pallas_call(
        kern,
        grid=(seq // tb,),
        in_specs=[
            pl.BlockSpec((nb, tb, S5_WIDTH), lambda t: (0, t, 0)),
            pl.BlockSpec(u_meta.shape, const2),
            pl.BlockSpec(perm.shape, const2),
            pl.BlockSpec(perm.shape, const2),
            pl.BlockSpec(bm.shape, const3),
            pl.BlockSpec(cm.shape, const3),
            pl.BlockSpec(tab.shape, const4),
            pl.BlockSpec(ptab.shape, const4),
            pl.BlockSpec(d_skip.shape, const2),
            pl.BlockSpec(w_glu_bf16.shape, const2),
        ],
        out_specs=pl.BlockSpec((nb, tb, S5_WIDTH), lambda t: (0, t, 0)),
        out_shape=jax.ShapeDtypeStruct((nb, seq, S5_WIDTH), BF16),
        scratch_shapes=[pltpu.VMEM((nb, n_groups, SUBLANES, 2 * LANES), F32),
                        pltpu.VMEM((2, nb * tb, half_ch), BF16),
                        pltpu.VMEM((2, nb * tb, 2 * LANES), F32),
                        pltpu.VMEM((nb * tb, 4 * LANES), BF16),
                        pltpu.VMEM((2, nb * tb, half_ch), F32)],
        compiler_params=_cparams(("arbitrary",)),
        name="s5_mixer",
    )(u, u_meta, perm, perm.T, bm, cm, tab, ptab, d_skip, w_glu_bf16)


def _s5_params(lam_re, lam_im, log_dt, b_re, b_im, c_re, c_im):
    lam = lax.complex(lam_re.astype(F32), lam_im.astype(F32))
    dt = jnp.exp(log_dt.astype(F32))[:, None]
    lam_dt = lam * dt
    lam_bar = jnp.exp(lam_dt)
    b_bar = ((lam_bar - 1.0) / lam)[:, :, None] * lax.complex(b_re.astype(F32), b_im.astype(F32))
    gl = S5_GROUPS // 2
    eye = jnp.eye(gl, dtype=F32)

    def in_half(bh):
        def blk(part):
            t = jnp.einsum('gph,gk->ghkp', part, eye)
            return t.reshape(gl * S5_GROUP, gl * S5_STATE)
        return jnp.concatenate([blk(jnp.real(bh)), blk(jnp.imag(bh))], axis=1)

    def out_half(cr, ci):
        def blk(part):
            t = jnp.einsum('ghp,gk->gpkh', part, eye)
            return t.reshape(gl * S5_STATE, gl * S5_GROUP)
        return jnp.concatenate([blk(cr), blk(-ci)], axis=0)

    bm = jnp.stack([in_half(b_bar[:gl]), in_half(b_bar[gl:])]).astype(BF16)
    cm = jnp.stack([out_half(c_re[:gl].astype(F32), c_im[:gl].astype(F32)),
                    out_half(c_re[gl:].astype(F32), c_im[gl:].astype(F32))]).astype(BF16)
    lam_flat = lam_dt.reshape(1, N_STATE)
    rows = jnp.arange(SUBLANES, dtype=F32)[:, None]
    tabs = []
    for shift in (1, 2, 4):
        p = jnp.exp(lam_flat * float(SEG_LEN * shift)) * (rows >= shift).astype(F32)
        tabs += [jnp.real(p), jnp.imag(p)]
    for p in (jnp.exp(lam_flat * float(SEG_LEN)), jnp.exp(lam_flat)):
        tabs += [jnp.real(p), jnp.imag(p)]
    tab = jnp.stack([jnp.broadcast_to(t, (SUBLANES, N_STATE)) for t in tabs]).astype(F32)
    steps = (jnp.arange(SEG_LEN * SUBLANES) // SUBLANES + 1).astype(F32)[:, None]
    p = jnp.exp(lam_flat * steps)
    ptab = jnp.stack([jnp.real(p), jnp.imag(p)]).astype(F32)
    ng = SCAN_GROUPS_PER_HALF
    bm = bm.reshape(2, gl * S5_GROUP, 2, ng, LANES).transpose(0, 3, 1, 2, 4)
    bm = bm.reshape(2 * ng, gl * S5_GROUP, 2 * LANES)
    cm = cm.reshape(2, 2, ng, LANES, gl * S5_GROUP).transpose(0, 2, 1, 3, 4)
    cm = cm.reshape(2 * ng, 2 * LANES, gl * S5_GROUP)
    tab = tab.reshape(tab.shape[0], SUBLANES, 2 * ng, LANES).transpose(2, 0, 1, 3)
    ptab = ptab.reshape(2, SEG_LEN * SUBLANES, 2 * ng, LANES).transpose(2, 0, 1, 3)
    return bm, cm, tab, ptab


def _dot_nt(a, b):
    return lax.dot_general(a, b, (((1,), (1,)), ((), ())), preferred_element_type=F32)


def _dot_tn(a, b):
    return lax.dot_general(a, b, (((0,), (0,)), ((), ())), preferred_element_type=F32)


def _cumsum_rows(tri, x):
    hi = x.astype(BF16)
    r1 = x - hi.astype(F32)
    mid = r1.astype(BF16)
    lo = (r1 - mid.astype(F32)).astype(BF16)
    return (jnp.dot(tri, hi, preferred_element_type=F32)
            + jnp.dot(tri, mid, preferred_element_type=F32)
            + jnp.dot(tri, lo, preferred_element_type=F32))


def _segment_rows(x, seg, pos):
    n = x.shape[0]
    parts = [jnp.broadcast_to(x[s * seg + pos:s * seg + pos + 1, :], (seg, x.shape[1]))
             for s in range(n // seg)]
    return parts[0] if len(parts) == 1 else jnp.concatenate(parts, axis=0)


def _hg_gates(x, lb, row_valid=None):
    w = HG_WIDTH
    q = x[:, 0:w]
    f = lb + (1.0 - lb) * _sigmoid(x[:, w:2 * w])
    logf = jnp.log(f)
    k = 1.0 - f
    if row_valid is not None:
        logf = jnp.where(row_valid, logf, 0.0)
        k = jnp.where(row_valid, k, 0.0)
    return _silu(q), k, logf


def _hg_kernel(x_ref, xm_ref, lb_ref, gain_ref, tri_ref, y_ref, st_ref, *, nb, tb):
    w, dh = HG_WIDTH, HG_HEAD_DIM
    lb = lb_ref[...]
    tri = tri_ref[...]

    @pl.when(pl.program_id(0) == 0)
    def _():
        xm = xm_ref[...].astype(F32)
        valid = lax.broadcasted_iota(jnp.int32, (META_PAD, 1), 0) >= (META_PAD - N_META)
        _, k, logf = _hg_gates(xm, lb, valid)
        bc = _cumsum_rows(tri[0:META_PAD, 0:META_PAD], logf)
        ki = (k * jnp.exp(bc[META_PAD - 1:META_PAD, :] - bc)).astype(BF16)
        v = xm[:, 2 * w:3 * w].astype(BF16)
        for h in range(HG_HEADS):
            sl = slice(h * dh, (h + 1) * dh)
            s0 = _dot_tn(v[:, sl], ki[:, sl])
            for b in range(nb):
                st_ref[b, h] = s0

    ri = lax.broadcasted_iota(jnp.int32, (tb, tb), 0)
    ci = lax.broadcasted_iota(jnp.int32, (tb, tb), 1)
    diag_shift = HG_DIAG.bit_length() - 1
    diag_mask = ((ri >> diag_shift) == (ci >> diag_shift)) & (ci <= ri)
    levels = []
    seg = 2 * HG_DIAG
    while seg <= tb:
        levels.append(seg)
        seg *= 2
    seg_masks = [None if s == tb else
                 ((ri >> (s.bit_length() - 1)) == (ci >> (s.bit_length() - 1))).astype(F32)
                 for s in levels]
    row = lax.broadcasted_iota(jnp.int32, (tb, 1), 0)

    for b in range(nb):
        x = x_ref[b].astype(F32)
        qs, k, logf = _hg_gates(x, lb)
        v = x[:, 2 * w:3 * w].astype(BF16)
        g = x[:, 3 * w:4 * w]
        bc = _cumsum_rows(tri, logf)
        dlt = bc - _segment_rows(bc, HG_DIAG, HG_DIAG // 2 - 1)
        qk = [((qs * jnp.exp(dlt)).astype(BF16), (k * jnp.exp(-dlt)).astype(BF16))]
        for s in levels:
            e = jnp.exp(-jnp.abs(bc - _segment_rows(bc, s, s // 2 - 1)))
            upper = (row & (s - 1)) >= (s // 2)
            qk.append((jnp.where(upper, qs * e, 0.0).astype(BF16),
                       jnp.where(upper, 0.0, k * e).astype(BF16)))
        b_last = bc[tb - 1:tb, :]
        qi = (qs * jnp.exp(bc)).astype(BF16)
        ki = (k * jnp.exp(b_last - bc)).astype(BF16)
        dec = jnp.exp(b_last)
        outs = []
        for h in range(HG_HEADS):
            sl = slice(h * dh, (h + 1) * dh)
            sc = jnp.where(diag_mask, _dot_nt(qk[0][0][:, sl], qk[0][1][:, sl]), 0.0)
            for (ql, kl), m in zip(qk[1:], seg_masks):
                t = _dot_nt(ql[:, sl], kl[:, sl])
                sc = sc + (t if m is None else t * m)
            st = st_ref[b, h]
            o = (jnp.dot(sc.astype(BF16), v[:, sl], preferred_element_type=F32)
                 + _dot_nt(qi[:, sl], st.astype(BF16)))
            st_ref[b, h] = dec[:, sl] * st + _dot_tn(v[:, sl], ki[:, sl])
            ms = jnp.mean(o * o, axis=-1, keepdims=True)
            outs.append(o * lax.rsqrt(ms + EPS))
        o = jnp.concatenate(outs, axis=1) * gain_ref[...]
        y_ref[b] = (o * _silu(g)).astype(BF16)


def _hg_mixer(x, x_meta, lb, gain, tri):
    nb, seq, _ = x.shape
    tb = TIME_BLOCK
    kern = functools.partial(_hg_kernel, nb=nb, tb=tb)
    const2 = lambda t: (0, 0)
    return pl.pallas_call(
        kern,
        grid=(seq // tb,),
        in_specs=[
            pl.BlockSpec((nb, tb, 4 * HG_WIDTH), lambda t: (0, t, 0)),
            pl.BlockSpec(x_meta.shape, const2),
            pl.BlockSpec(lb.shape, const2),
            pl.BlockSpec(gain.shape, const2),
            pl.BlockSpec(tri.shape, const2),
        ],
        out_specs=pl.BlockSpec((nb, tb, HG_WIDTH), lambda t: (0, t, 0)),
        out_shape=jax.ShapeDtypeStruct((nb, seq, HG_WIDTH), BF16),
        scratch_shapes=[pltpu.VMEM((nb, HG_HEADS, HG_HEAD_DIM, HG_HEAD_DIM), F32)],
        compiler_params=_cparams(("arbitrary",)),
        name="hgrn2_mixer",
    )(x, x_meta, lb, gain, tri)


def _split3(x):
    hi = x.astype(BF16)
    lo = (x - hi.astype(F32)).astype(BF16)
    return hi, lo


def _pack_pairs(y):
    n = y.shape[1] // 2
    lo = pltpu.bitcast(y[:, :n].astype(BF16).astype(F32), jnp.uint32)
    hi = pltpu.bitcast(y[:, n:].astype(BF16).astype(F32), jnp.uint32)
    return (lo >> 16) | (hi & jnp.uint32(0xFFFF0000))


def _unpack_pairs(p):
    lo = pltpu.bitcast(p << 16, F32)
    hi = pltpu.bitcast(p & jnp.uint32(0xFFFF0000), F32)
    return lo, hi


def _merge_kernel(x_ref, ys_ref, yh_ref, gt_ref, wbs_ref, wbh_ref, wo_ref, g2_ref, wr_hi_ref,
                  wr_lo_ref, br_ref, tri_ref, h2_ref, xp_ref, eid_ref, wt_ref, rank_ref, cnt_ref,
                  run_ref):
    @pl.when(pl.program_id(0) == 0)
    def _():
        run_ref[...] = jnp.zeros_like(run_ref)

    gt = gt_ref[...].astype(F32)
    gs = _sigmoid(gt[:, :D_MODEL])
    gh = _sigmoid(gt[:, D_MODEL:])
    merged = (gs * jnp.dot(ys_ref[...], wbs_ref[...], preferred_element_type=F32)
              + gh * jnp.dot(yh_ref[...], wbh_ref[...], preferred_element_type=F32))
    h2 = x_ref[...] + jnp.dot(merged.astype(BF16), wo_ref[...], preferred_element_type=F32)
    h2_ref[...] = h2
    ms = jnp.mean(h2 * h2, axis=-1, keepdims=True)
    xn = h2 * lax.rsqrt(ms + EPS) * g2_ref[...]
    xp_ref[...] = _pack_pairs(xn)

    x_hi, x_lo = _split3(xn)
    logits = (_dot_nt(wr_hi_ref[...], x_hi) + _dot_nt(wr_lo_ref[...], x_hi)
              + _dot_nt(wr_hi_ref[...], x_lo)) + br_ref[:, 0:1]
    rows = logits.shape[1]
    expert = lax.broadcasted_iota(jnp.int32, (N_EXPERTS, rows), 0)
    tops, hots, ids = [], [], []
    sel = jnp.zeros((N_EXPERTS, rows), F32)
    for k in range(TOP_K):
        m = jnp.max(logits, axis=0, keepdims=True)
        idx = jnp.min(jnp.where(logits == m, expert, N_EXPERTS), axis=0, keepdims=True)
        hot = expert == idx
        logits = jnp.where(hot, NEG_BIG, logits)
        tops.append(m)
        hots.append(hot)
        ids.append(idx)
        sel = sel + hot.astype(F32)
    es = [jnp.exp(m - tops[0]) for m in tops]
    tot = es[0] + es[1] + es[2] + es[3]
    run = run_ref[...]
    prefix = jnp.dot(sel.astype(BF16), tri_ref[...], preferred_element_type=F32) + run[:, 0:1]
    ranks = [jnp.sum(jnp.where(hot, prefix, 0.0), axis=0, keepdims=True) for hot in hots]
    eid_ref[...] = jnp.concatenate(ids, axis=0)
    wt_ref[...] = jnp.concatenate([e / tot for e in es], axis=0)
    rank_ref[...] = jnp.concatenate(ranks, axis=0).astype(jnp.int32)
    run = run + jnp.sum(sel, axis=1, keepdims=True)
    run_ref[...] = run
    cnt_ref[...] = run


def _merge_router(x_rows, y_s5, y_hg, gates, wbs, wbh, wo, g2, wr_hi, wr_lo, br, tri_strict):
    n = x_rows.shape[0]
    rb = ROW_BLOCK
    const2 = lambda i: (0, 0)
    rowblk = lambda width: pl.BlockSpec((rb, width), lambda i: (i, 0))
    choice = pl.BlockSpec((TOP_K, rb), lambda i: (0, i))
    return pl.pallas_call(
        _merge_kernel,
        grid=(n // rb,),
        in_specs=[
            rowblk(D_MODEL), rowblk(S5_WIDTH), rowblk(HG_WIDTH), rowblk(2 * D_MODEL),
            pl.BlockSpec(wbs.shape, const2), pl.BlockSpec(wbh.shape, const2),
            pl.BlockSpec(wo.shape, const2), pl.BlockSpec(g2.shape, const2),
            pl.BlockSpec(wr_hi.shape, const2), pl.BlockSpec(wr_lo.shape, const2),
            pl.BlockSpec(br.shape, const2), pl.BlockSpec(tri_strict.shape, const2),
        ],
        out_specs=[
            rowblk(D_MODEL), rowblk(D_MODEL // 2), choice, choice, choice,
            pl.BlockSpec((N_EXPERTS, LANES), const2),
        ],
        out_shape=[
            jax.ShapeDtypeStruct((n, D_MODEL), F32),
            jax.ShapeDtypeStruct((n, D_MODEL // 2), jnp.uint32),
            jax.ShapeDtypeStruct((TOP_K, n), jnp.int32),
            jax.ShapeDtypeStruct((TOP_K, n), F32),
            jax.ShapeDtypeStruct((TOP_K, n), jnp.int32),
            jax.ShapeDtypeStruct((N_EXPERTS, LANES), F32),
        ],
        scratch_shapes=[pltpu.VMEM((N_EXPERTS, LANES), F32)],
        compiler_params=_cparams(("arbitrary",)),
        name="merge_router",
    )(x_rows, y_s5, y_hg, gates, wbs, wbh, wo, g2, wr_hi, wr_lo, br, tri_strict)


def _gather_rows(src, idx):
    n = idx.shape[0]
    width = src.shape[1]
    assert n % GATHER_ROWS == 0
    mesh = plsc.VectorSubcoreMesh(core_axis_name="core", subcore_axis_name="subcore")

    @functools.partial(pl.kernel, out_type=jax.ShapeDtypeStruct((n, width), src.dtype),
                       mesh=mesh, scratch_types=[], name="sc_gather_rows")
    def gather(src_hbm, idx_hbm, out_hbm):
        def body(idx_vmem, out_vmem):
            off = pl.multiple_of(pl.program_id(1) * GATHER_WINDOW, GATHER_WINDOW)
            pltpu.sync_copy(src_hbm.at[idx_vmem.at[0, pl.ds(off, GATHER_WINDOW)]], out_vmem)

        pltpu.emit_pipeline(
            body,
            grid=(n // INDEX_BLOCK, INDEX_SPLIT),
            in_specs=[pl.BlockSpec((1, INDEX_BLOCK), lambda i, j: (0, i))],
            out_specs=[pl.BlockSpec((GATHER_WINDOW, width), lambda i, j: (INDEX_SPLIT * i + j, 0))],
            core_axis_name=("core", "subcore"),
            dimension_semantics=(pltpu.PARALLEL, pltpu.ARBITRARY),
        )(idx_hbm, out_hbm)

    return gather(src, idx.reshape(1, n))


def _scatter_rows(src, dest, n_out):
    n_src, width = src.shape
    assert dest.shape == (TOP_K, n_src) and n_src % GATHER_ROWS == 0
    mesh = plsc.VectorSubcoreMesh(core_axis_name="core", subcore_axis_name="subcore")

    @functools.partial(pl.kernel, out_type=jax.ShapeDtypeStruct((n_out, width), src.dtype),
                       mesh=mesh, scratch_types=[], name="sc_scatter_rows")
    def scatter(src_hbm, idx_hbm, out_hbm):
        def body(src_vmem, idx_vmem):
            off = pl.multiple_of(pl.program_id(1) * GATHER_WINDOW, GATHER_WINDOW)
            for k in range(TOP_K):
                pltpu.sync_copy(src_vmem, out_hbm.at[idx_vmem.at[k, pl.ds(off, GATHER_WINDOW)]])

        pltpu.emit_pipeline(
            body,
            grid=(n_src // INDEX_BLOCK, INDEX_SPLIT),
            in_specs=[pl.BlockSpec((GATHER_WINDOW, width), lambda i, j: (INDEX_SPLIT * i + j, 0)),
                      pl.BlockSpec((TOP_K, INDEX_BLOCK), lambda i, j: (0, i))],
            out_specs=[],
            core_axis_name=("core", "subcore"),
            dimension_semantics=(pltpu.PARALLEL, pltpu.ARBITRARY),
        )(src_hbm, idx_hbm)

    return scatter(src, dest)


GU_GROUP = 256
DOWN_GROUPS = 4


def _expert_kernel(be_ref, nv_ref, nr_ref, x_ref, wgu_ref, bgu_ref, wd_ref, bd_ref, perm_ref,
                   y_ref, wgu_s, wd_s):
    i = pl.program_id(0)
    prev = be_ref[jnp.maximum(i - 1, 0)]
    fresh = (i == 0) | (be_ref[i] != prev)

    @pl.when(fresh & (i < nv_ref[0]))
    def _():
        for c in range(2 * D_EXPERT // GU_GROUP):
            cols = slice(c * GU_GROUP, (c + 1) * GU_GROUP)
            w = wgu_ref[0, :, cols].astype(BF16)
            wgu_s[:, cols] = jnp.dot(w, perm_ref[...], preferred_element_type=F32).astype(BF16)
        wd_s[...] = wd_ref[0].astype(BF16)

    @pl.when(i < nv_ref[0])
    def _():
        half = D_MODEL // 2
        live = lax.broadcasted_iota(jnp.int32, (MOE_BLOCK, 1), 0) < nr_ref[i]
        xa, xb = _unpack_pairs(jnp.where(live, x_ref[...], jnp.uint32(0)))
        x = jnp.concatenate([xa.astype(BF16), xb.astype(BF16)], axis=1)
        hw = GU_GROUP // 2
        n_groups = 2 * D_EXPERT // GU_GROUP

        def gate_up(c):
            return jnp.dot(x, wgu_s[:, c * GU_GROUP:(c + 1) * GU_GROUP],
                           preferred_element_type=F32)

        acc = None
        hcols = []
        gu_next = gate_up(0)
        for c in range(n_groups):
            gu = gu_next + bgu_ref[0, :, c * GU_GROUP:(c + 1) * GU_GROUP]
            if c + 1 < n_groups:
                gu_next = gate_up(c + 1)
            gate = jnp.minimum(gu[:, :hw], SWIGLU_LIMIT)
            up = jnp.clip(gu[:, hw:], -SWIGLU_LIMIT, SWIGLU_LIMIT)
            hcols.append(((up + 1.0) * (gate * _sigmoid(gate * SWIGLU_ALPHA))).astype(BF16))
            if (c + 1) % DOWN_GROUPS == 0:
                lo = c + 1 - DOWN_GROUPS
                part = jnp.dot(jnp.concatenate(hcols[lo:c + 1], axis=1),
                               wd_s[lo * hw:(c + 1) * hw, :], preferred_element_type=F32)
                acc = part if acc is None else acc + part
        y_ref[...] = _pack_pairs(acc + bd_ref[0])

    @pl.when(i >= nv_ref[0])
    def _():
        y_ref[...] = jnp.zeros_like(y_ref)


def _experts(block_e, n_valid, block_rows, x_rows, w_gate_up, b_gu_grouped, w_down, b_down, perm):
    n_rows = x_rows.shape[0]
    n_blocks = n_rows // MOE_BLOCK
    half = D_MODEL // 2
    by_expert = lambda i, be, nv, nr: (be[i], 0, 0)
    grid_spec = pltpu.PrefetchScalarGridSpec(
        num_scalar_prefetch=3,
        grid=(n_blocks,),
        in_specs=[
            pl.BlockSpec((MOE_BLOCK, half), lambda i, be, nv, nr: (i, 0)),
            pl.BlockSpec((1, D_MODEL, 2 * D_EXPERT), by_expert),
            pl.BlockSpec((1, 1, 2 * D_EXPERT), by_expert),
            pl.BlockSpec((1, D_EXPERT, D_MODEL), by_expert),
            pl.BlockSpec((1, 1, D_MODEL), by_expert),
            pl.BlockSpec(perm.shape, lambda i, be, nv, nr: (0, 0)),
        ],
        out_specs=pl.BlockSpec((MOE_BLOCK, half), lambda i, be, nv, nr: (i, 0)),
        scratch_shapes=[
            pltpu.VMEM((D_MODEL, 2 * D_EXPERT), BF16),
            pltpu.VMEM((D_EXPERT, D_MODEL), BF16),
        ],
    )
    return pl.pallas_call(
        _expert_kernel,
        grid_spec=grid_spec,
        out_shape=jax.ShapeDtypeStruct((n_rows, half), jnp.uint32),
        compiler_params=_cparams(("arbitrary",)),
        name="experts",
    )(block_e, n_valid, block_rows, x_rows, w_gate_up, b_gu_grouped, w_down, b_down, perm)


def _combine_kernel(h2_ref, y0_ref, y1_ref, y2_ref, y3_ref, wt_ref, gf_ref, out_ref):
    half = D_MODEL // 2
    wt = wt_ref[...]
    lo = jnp.zeros((h2_ref.shape[0], half), F32)
    hi = jnp.zeros((h2_ref.shape[0], half), F32)
    for k, yk_ref in enumerate((y0_ref, y1_ref, y2_ref, y3_ref)):
        a, b = _unpack_pairs(yk_ref[...])
        lo = lo + wt[:, k:k + 1] * a
        hi = hi + wt[:, k:k + 1] * b
    y = h2_ref[...] + jnp.concatenate([lo, hi], axis=1)
    ms = jnp.mean(y * y, axis=-1, keepdims=True)
    out_ref[...] = y * lax.rsqrt(ms + EPS) * gf_ref[...]


def _combine(h2, y_tok, wts, gain):
    n = h2.shape[0]
    rb = ROW_BLOCK
    rowblk = lambda width: pl.BlockSpec((rb, width), lambda i: (i, 0))
    steps = n // rb
    choice = lambda k: pl.BlockSpec((rb, D_MODEL // 2), lambda i: (k * steps + i, 0))
    return pl.pallas_call(
        _combine_kernel,
        grid=(steps,),
        in_specs=[rowblk(D_MODEL)] + [choice(k) for k in range(TOP_K)]
        + [rowblk(TOP_K), pl.BlockSpec((1, D_MODEL), lambda i: (0, 0))],
        out_specs=rowblk(D_MODEL),
        out_shape=jax.ShapeDtypeStruct((n, D_MODEL), F32),
        compiler_params=_cparams(("parallel",)),
        name="combine_norm",
    )(h2, y_tok, y_tok, y_tok, y_tok, wts, gain)


def _lower_tri(n, strict):
    r = lax.broadcasted_iota(jnp.int32, (n, n), 0)
    c = lax.broadcasted_iota(jnp.int32, (n, n), 1)
    return ((c < r) if strict else (c <= r)).astype(BF16)


def kernel(x, meta_tokens, norm1_gain, w_in, s5_lambda_re, s5_lambda_im, s5_log_dt, s5_b_re,
           s5_b_im, s5_c_re, s5_c_im, s5_d, s5_w_glu, hgrn_lb_logits, hgrn_norm_gain,
           w_branch_s5, w_branch_hgrn, w_out, norm2_gain, w_router, b_router, w_gate_up,
           b_gate_up, w_down, b_down, final_norm_gain):
    nb, seq, d = x.shape
    n_tok = nb * seq
    assert d == D_MODEL and seq % TIME_BLOCK == 0 and n_tok % ROW_BLOCK == 0
    x_rows = x.reshape(n_tok, d)
    w_in_b = w_in[0].astype(BF16)
    g1 = norm1_gain[0].reshape(1, d).astype(F32)

    u, qfvg, gates = _inproj(x_rows, g1, w_in_b, ROW_BLOCK)
    meta_rows = jnp.concatenate(
        [jnp.zeros((META_PAD - N_META, d), F32), meta_tokens.astype(F32)], axis=0)
    u_m, qfvg_m, _ = _inproj(meta_rows, g1, w_in_b, META_PAD)

    bm, cm, tab, ptab = _s5_params(s5_lambda_re[0], s5_lambda_im[0], s5_log_dt[0], s5_b_re[0],
                                   s5_b_im[0], s5_c_re[0], s5_c_im[0])
    y_s5 = _s5_mixer(u.reshape(nb, seq, S5_WIDTH), u_m, bm, cm, tab, ptab,
                     s5_d[0].reshape(1, S5_WIDTH).astype(F32), s5_w_glu[0].astype(BF16))

    lower_bounds = jnp.cumsum(jax.nn.softmax(hgrn_lb_logits.astype(F32), axis=0), axis=0)
    lb = lower_bounds[0].reshape(1, HG_WIDTH)
    y_hg = _hg_mixer(qfvg.reshape(nb, seq, 4 * HG_WIDTH), qfvg_m, lb,
                     hgrn_norm_gain[0].reshape(1, HG_WIDTH).astype(F32),
                     _lower_tri(TIME_BLOCK, strict=False))

    wr = w_router[0].astype(F32).T
    wr_hi = wr.astype(BF16)
    wr_lo = (wr - wr_hi.astype(F32)).astype(BF16)
    br = jnp.broadcast_to(b_router[0].astype(F32)[:, None], (N_EXPERTS, LANES))
    h2, xn_packed, eid, wts, rank, counts = _merge_router(
        x_rows, y_s5.reshape(n_tok, S5_WIDTH), y_hg.reshape(n_tok, HG_WIDTH), gates,
        w_branch_s5[0].astype(BF16), w_branch_hgrn[0].astype(BF16), w_out[0].astype(BF16),
        norm2_gain[0].reshape(1, d).astype(F32), wr_hi, wr_lo, br,
        _lower_tri(ROW_BLOCK, strict=True).T)

    n_assign = n_tok * TOP_K
    n_blocks = n_assign // MOE_BLOCK + N_EXPERTS
    n_rows = n_blocks * MOE_BLOCK
    n_rows_pad = -(-n_rows // GATHER_ROWS) * GATHER_ROWS
    cnt = counts[:, 0].astype(jnp.int32)
    padded = (cnt + MOE_BLOCK - 1) // MOE_BLOCK * MOE_BLOCK
    padded_end = jnp.cumsum(padded)
    padded_start = padded_end - padded
    experts = jnp.arange(N_EXPERTS, dtype=jnp.int32)[None, :, None]
    start_of = jnp.sum(jnp.where(eid[:, None, :] == experts, padded_start[None, :, None], 0), axis=1)
    dest = start_of + rank
    block_start = jnp.arange(n_rows_pad // MOE_BLOCK, dtype=jnp.int32) * MOE_BLOCK
    block_e = jnp.minimum(jnp.sum((block_start[:, None] >= padded_end[None, :]).astype(jnp.int32),
                                  axis=1), N_EXPERTS - 1)
    of_block = block_e[:, None] == jnp.arange(N_EXPERTS, dtype=jnp.int32)[None, :]
    end_of_block = jnp.sum(jnp.where(of_block, (padded_start + cnt)[None, :], 0), axis=1)
    block_rows = jnp.clip(end_of_block - block_start, 0, MOE_BLOCK)
    n_valid = (padded_end[-1] // MOE_BLOCK).astype(jnp.int32).reshape(1)

    x_sorted = _scatter_rows(xn_packed, dest, n_rows_pad)
    hw = GU_GROUP // 2
    pr = jnp.arange(GU_GROUP)
    src = jnp.where(pr < hw, 2 * pr, 2 * (pr - hw) + 1)
    perm = (jnp.arange(GU_GROUP)[:, None] == src[None, :]).astype(BF16)
    b_gu = b_gate_up[0].astype(F32).reshape(N_EXPERTS, 2 * D_EXPERT // GU_GROUP, hw, 2)
    b_gu = b_gu.transpose(0, 1, 3, 2).reshape(N_EXPERTS, 1, 2 * D_EXPERT)
    y_sorted = _experts(block_e, n_valid, block_rows, x_sorted, w_gate_up[0], b_gu, w_down[0],
                        b_down[0].astype(F32).reshape(N_EXPERTS, 1, d), perm)
    y_tok = _gather_rows(y_sorted, dest.reshape(-1))

    out = _combine(h2, y_tok, wts.T, final_norm_gain.reshape(1, d).astype(F32))
    return out.reshape(nb, seq, d)
```
